```python
import jax, jax.numpy as jnp
from jax import lax
import numpy as np

D_MODEL = 2048
BATCH = 2
SEQ = 4096
DEPTH = 4

MEM_LEN = 256
HEAD_DIM = 128
GRID_W = 64
NA_HEADS = 6
NA_ROWS = 8
NA_COLS = 16
RET_HEADS = 4
RET_DK = 128
RET_DV = 256
RET_CHUNK = 128
ROPE_BASE = 10000.0
DIL_HEADS = 6
DIL_PAIRS = ((128, 1), (512, 4), (2048, 16))
DIL_BLOCK = 128
T5_BUCKETS = 32
T5_MAX_DIST = 1024
CROSS_HEADS = 4
D_FF = 4 * D_MODEL
EPS = 1e-6
NEG = -1e30

NA_W = NA_HEADS * HEAD_DIM
RET_QK_W = RET_HEADS * RET_DK
RET_V_W = RET_HEADS * RET_DV
DIL_W = DIL_HEADS * HEAD_DIM
CROSS_W = CROSS_HEADS * HEAD_DIM
MIX_W = NA_W + RET_V_W + DIL_W
IN_SPLITS = (NA_W, NA_W, NA_W, RET_QK_W, RET_QK_W, RET_V_W, RET_V_W,
             DIL_W, DIL_W, DIL_W, D_MODEL, D_MODEL, D_MODEL)
IN_W = sum(IN_SPLITS)

kernel_name = "hybrid_natten_retention_dilated_encoder"


def rms_norm(x, g):
    xf = x.astype(jnp.float32)
    y = xf * lax.rsqrt(jnp.mean(jnp.square(xf), axis=-1, keepdims=True) + EPS)
    return (y * g.astype(jnp.float32)).astype(x.dtype)


def split_heads(t, h):
    b, s, w = t.shape
    return t.reshape(b, s, h, w // h).transpose(0, 2, 1, 3)


def merge_heads(t):
    b, h, s, d = t.shape
    return t.transpose(0, 2, 1, 3).reshape(b, s, h * d)


def t5_bucket(rel):
    nb = T5_BUCKETS // 2
    ret = (rel > 0).astype(np.int32) * nb
    n = np.abs(rel)
    max_exact = nb // 2
    large = max_exact + (np.log(np.maximum(n, 1) / max_exact) / np.log(T5_MAX_DIST / max_exact)
                         * (nb - max_exact)).astype(np.int32)
    large = np.minimum(large, nb - 1)
    return (ret + np.where(n < max_exact, n, large)).astype(np.int32)


def neighbourhood_attention(q, k, v, rpb):
    b, h, s, hd = q.shape
    rows = s // GRID_W
    wr = min(NA_ROWS, rows)
    qg = q.reshape(b, h, rows, GRID_W, hd)
    kg = k.reshape(b, h, rows, GRID_W, hd)
    vg = v.reshape(b, h, rows, GRID_W, hd)
    c = np.arange(GRID_W)
    c0 = np.clip(c - NA_COLS // 2, 0, GRID_W - NA_COLS)
    col_idx = c0[:, None] + np.arange(NA_COLS)[None, :]
    col_off = col_idx - c[:, None] + (NA_COLS - 1)
    scale = hd ** -0.5

    def row_block(r):
        r0 = jnp.clip(r - wr // 2, 0, rows - wr)
        q_r = lax.dynamic_index_in_dim(qg, r, axis=2, keepdims=False)
        k_r = lax.dynamic_slice_in_dim(kg, r0, wr, axis=2)[:, :, :, col_idx]
        v_r = lax.dynamic_slice_in_dim(vg, r0, wr, axis=2)[:, :, :, col_idx]
        row_off = r0 + jnp.arange(wr) - r + (NA_ROWS - 1)
        bias = rpb[:, row_off][:, :, col_off].transpose(0, 2, 1, 3)
        logits = (jnp.einsum('bhcd,bhrcjd->bhcrj', q_r, k_r).astype(jnp.float32) * scale
                  + bias[None].astype(jnp.float32))
        p = jax.nn.softmax(logits.reshape(b, h, GRID_W, wr * NA_COLS), axis=-1)
        p = p.reshape(logits.shape).astype(v.dtype)
        return jnp.einsum('bhcrj,bhrcjd->bhcd', p, v_r)

    out = lax.map(row_block, jnp.arange(rows))
    return out.transpose(1, 2, 0, 3, 4).reshape(b, h, s, hd)


def dilated_attention(q, k, v, t5_table):
    b, h, s, hd = q.shape
    scale = hd ** -0.5
    nblk = s // DIL_BLOCK
    groups = []
    for (w, d) in DIL_PAIRS:
        half = w // (2 * d)
        offs = d * np.arange(-half, half + 1)
        bias = t5_table[t5_bucket(offs)].T.astype(jnp.float32)
        groups.append((offs, bias))

    def q_block(i):
        t = i * DIL_BLOCK + jnp.arange(DIL_BLOCK)
        q_b = lax.dynamic_slice_in_dim(q, i * DIL_BLOCK, DIL_BLOCK, axis=2)
        outs, lses = [], []
        for offs, bias in groups:
            idx = t[:, None] + offs[None, :]
            valid = (idx >= 0) & (idx < s)
            idx = jnp.clip(idx, 0, s - 1)
            k_g = jnp.take(k, idx, axis=2)
            v_g = jnp.take(v, idx, axis=2)
            logits = (jnp.einsum('bhqd,bhqkd->bhqk', q_b, k_g).astype(jnp.float32) * scale
                      + bias[None, :, None, :])
            logits = jnp.where(valid[None, None], logits, NEG)
            lse = jax.nn.logsumexp(logits, axis=-1)
            p = jnp.exp(logits - lse[..., None]).astype(v.dtype)
            outs.append(jnp.einsum('bhqk,bhqkd->bhqd', p, v_g))
            lses.append(lse)
        alpha = jax.nn.softmax(jnp.stack(lses, axis=0), axis=0)
        return jnp.einsum('gbhq,gbhqd->bhqd', alpha.astype(v.dtype), jnp.stack(outs, axis=0))

    out = lax.map(q_block, jnp.arange(nblk))
    return out.transpose(1, 2, 0, 3, 4).reshape(b, h, s, hd)


def rotary(x):
    s, d = x.shape[2], x.shape[3]
    inv_freq = jnp.asarray((ROPE_BASE ** (-np.arange(0, d, 2, dtype=np.float32) / d)).astype(np.float32))
    ang = jnp.arange(s, dtype=jnp.float32)[:, None] * inv_freq[None, :]
    cos, sin = jnp.cos(ang), jnp.sin(ang)
    x1, x2 = x[..., : d // 2], x[..., d // 2:]
    return jnp.concatenate([x1 * cos - x2 * sin, x1 * sin + x2 * cos], axis=-1)


def retention_dir(q, k, v, log_g, include_diag):
    b, h, s, dk = q.shape
    dv = v.shape[-1]
    n = s // RET_CHUNK
    j = jnp.arange(RET_CHUNK, dtype=jnp.float32)
    diff = j[:, None] - j[None, :]
    keep = (diff >= 0) if include_diag else (diff > 0)
    dmask = jnp.where(keep, jnp.exp(jnp.where(keep, diff, 0.0) * log_g[:, None, None]), 0.0)
    xi = jnp.exp((j + 1.0) * log_g[:, None])
    zeta = jnp.exp((RET_CHUNK - 1.0 - j) * log_g[:, None])
    g_c = jnp.exp(RET_CHUNK * log_g)

    def chunks(t):
        return t.reshape(b, h, n, RET_CHUNK, t.shape[-1]).transpose(2, 0, 1, 3, 4)

    def step(state, inp):
        qi, ki, vi = inp
        inner = jnp.einsum('bhqd,bhkd->bhqk', qi, ki) * dmask[None]
        o = (jnp.einsum('bhqk,bhkv->bhqv', inner, vi)
             + jnp.einsum('bhqd,bhdv->bhqv', qi, state) * xi[None, :, :, None])
        state = (state * g_c[None, :, None, None]
                 + jnp.einsum('bhkd,bhkv->bhdv', ki, vi * zeta[None, :, :, None]))
        return state, o

    state0 = jnp.zeros((b, h, dk, dv), jnp.float32)
    _, o = lax.scan(step, state0, (chunks(q), chunks(k), chunks(v)))
    return o.transpose(1, 2, 0, 3, 4).reshape(b, h, s, dv)


def retention_mixer(q, k, v, gate, decay_w):
    qh = rotary(split_heads(q, RET_HEADS).astype(jnp.float32))
    kh = rotary(split_heads(k, RET_HEADS).astype(jnp.float32)) * (RET_DK ** -0.5)
    vh = split_heads(v, RET_HEADS).astype(jnp.float32)
    log_g = -jnp.exp(decay_w.astype(jnp.float32))
    fwd = retention_dir(qh, kh, vh, log_g[0], True)
    flip = lambda t: jnp.flip(t, axis=2)
    bwd = flip(retention_dir(flip(qh), flip(kh), flip(vh), log_g[1], False))
    y = fwd + bwd
    mu = jnp.mean(y, axis=-1, keepdims=True)
    var = jnp.mean(jnp.square(y - mu), axis=-1, keepdims=True)
    y = (y - mu) * lax.rsqrt(var + EPS)
    return jax.nn.silu(gate) * merge_heads(y).astype(gate.dtype)


def cross_attention(xn, mn, w_q, w_kv, w_o):
    q = split_heads(xn @ w_q, CROSS_HEADS)
    k, v = jnp.split(mn @ w_kv, 2, axis=-1)
    k, v = split_heads(k, CROSS_HEADS), split_heads(v, CROSS_HEADS)
    logits = jnp.einsum('bhqd,bhkd->bhqk', q, k).astype(jnp.float32) * (HEAD_DIM ** -0.5)
    p = jax.nn.softmax(logits, axis=-1).astype(v.dtype)
    return merge_heads(jnp.einsum('bhqk,bhkd->bhqd', p, v)) @ w_o


def setup_inputs(seed: int = 0) -> dict:
    key = jax.random.key(seed)
    ks = jax.random.split(key, 20)
    f32 = jnp.float32

    def dense(k, shape, fan_in):
        return jax.random.normal(k, shape, f32) * (fan_in ** -0.5)

    def gain(k, shape):
        return 1.0 + 0.02 * jax.random.normal(k, shape, f32)

    hh = np.arange(RET_HEADS, dtype=np.float32)
    w0 = np.log(-np.log(1.0 - 2.0 ** (-5.0 - hh))).astype(np.float32)
    ret_decay = jnp.asarray(w0)[None, None, :] + 0.05 * jax.random.normal(ks[5], (DEPTH, 2, RET_HEADS), f32)
    return {
        "x": jax.random.normal(ks[0], (BATCH, SEQ, D_MODEL), f32),
        "mem": jax.random.normal(ks[1], (BATCH, MEM_LEN, D_MODEL), f32),
        "t5_bias": 0.1 * jax.random.normal(ks[2], (T5_BUCKETS, DIL_HEADS), f32),
        "norm_mix_g": gain(ks[3], (DEPTH, D_MODEL)),
        "w_in": dense(ks[4], (DEPTH, D_MODEL, IN_W), D_MODEL),
        "na_rpb": 0.1 * jax.random.normal(ks[6], (DEPTH, NA_HEADS, 2 * NA_ROWS - 1, 2 * NA_COLS - 1), f32),
        "ret_decay": ret_decay,
        "w_branch": dense(ks[7], (DEPTH, MIX_W, D_MODEL), NA_W),
        "w_out": dense(ks[8], (DEPTH, D_MODEL, D_MODEL), D_MODEL),
        "norm_cross_g": gain(ks[9], (DEPTH, D_MODEL)),
        "norm_mem_g": gain(ks[10], (DEPTH, D_MODEL)),
        "w_cq": dense(ks[11], (DEPTH, D_MODEL, CROSS_W), D_MODEL),
        "w_ckv": dense(ks[12], (DEPTH, D_MODEL, 2 * CROSS_W), D_MODEL),
        "w_co": dense(ks[13], (DEPTH, CROSS_W, D_MODEL), CROSS_W),
        "norm_mlp_g": gain(ks[14], (DEPTH, D_MODEL)),
        "w_mlp1": dense(ks[15], (DEPTH, D_MODEL, D_FF), D_MODEL),
        "w_mlp2": dense(ks[16], (DEPTH, D_FF, D_MODEL), D_FF),
        "final_norm_g": gain(ks[17], (D_MODEL,)),
    }


def reference(x, mem, t5_bias, norm_mix_g, w_in, na_rpb, ret_decay, w_branch, w_out,
              norm_cross_g, norm_mem_g, w_cq, w_ckv, w_co, norm_mlp_g, w_mlp1, w_mlp2,
              final_norm_g):
    split_at = [int(o) for o in np.cumsum(IN_SPLITS)[:-1]]
    for l in range(DEPTH):
        xn = rms_norm(x, norm_mix_g[l])
        u = xn @ w_in[l]
        (qa, ka, va, qb, kb, vb, g_ret, qc, kc, vc,
         s_a, s_b, s_c) = jnp.split(u, split_at, axis=-1)
        o_a = merge_heads(neighbourhood_attention(split_heads(qa, NA_HEADS), split_heads(ka, NA_HEADS),
                                                  split_heads(va, NA_HEADS), na_rpb[l]))
        o_b = retention_mixer(qb, kb, vb, g_ret, ret_decay[l])
        o_c = merge_heads(dilated_attention(split_heads(qc, DIL_HEADS), split_heads(kc, DIL_HEADS),
                                            split_heads(vc, DIL_HEADS), t5_bias))
        wb = w_branch[l]
        merged = (jax.nn.sigmoid(s_a) * (o_a @ wb[:NA_W])
                  + jax.nn.sigmoid(s_b) * (o_b @ wb[NA_W:NA_W + RET_V_W])
                  + jax.nn.sigmoid(s_c) * (o_c @ wb[NA_W + RET_V_W:]))
        x = x + merged @ w_out[l]
        x = x + cross_attention(rms_norm(x, norm_cross_g[l]), rms_norm(mem, norm_mem_g[l]),
                                w_cq[l], w_ckv[l], w_co[l])
        xn = rms_norm(x, norm_mlp_g[l])
        x = x + jnp.square(jax.nn.relu(xn @ w_mlp1[l])) @ w_mlp2[l]
    return rms_norm(x, final_norm_g)
```

```python
import functools

import numpy as np
import jax
import jax.numpy as jnp
from jax import lax
from jax.experimental import pallas as pl
from jax.experimental.pallas import tpu as pltpu

D_MODEL = 2048
BATCH = 2
SEQ = 4096
DEPTH = 4
MEM_LEN = 256
HEAD_DIM = 128
GRID_W = 64
NA_HEADS = 6
NA_ROWS = 8
NA_COLS = 16
RET_HEADS = 4
RET_DK = 128
RET_DV = 256
RET_CHUNK = 128
ROPE_BASE = 10000.0
DIL_HEADS = 6
DIL_PAIRS = ((128, 1), (512, 4), (2048, 16))
T5_BUCKETS = 32
T5_MAX_DIST = 1024
CROSS_HEADS = 4
D_FF = 4 * D_MODEL
EPS = 1e-6
NEG = -1e30

NA_W = NA_HEADS * HEAD_DIM
RET_QK_W = RET_HEADS * RET_DK
RET_V_W = RET_HEADS * RET_DV
DIL_W = DIL_HEADS * HEAD_DIM
CROSS_W = CROSS_HEADS * HEAD_DIM
IN_SPLITS = (NA_W, NA_W, NA_W, RET_QK_W, RET_QK_W, RET_V_W, RET_V_W,
             DIL_W, DIL_W, DIL_W, D_MODEL, D_MODEL, D_MODEL)
IN_W = sum(IN_SPLITS)
(OFF_QA, OFF_KA, OFF_VA, OFF_QB, OFF_KB, OFF_VB, OFF_GB,
 OFF_QC, OFF_KC, OFF_VC, OFF_SA, OFF_SB, OFF_SC) = [int(o) for o in np.cumsum((0,) + IN_SPLITS[:-1])]

M_TOK = BATCH * SEQ
ATT_SCALE = HEAD_DIM ** -0.5

V7X_VMEM_BYTES = 64 * 1024 * 1024

NA_QROWS = 4
NA_KROWS = NA_QROWS + NA_ROWS
NA_QBLK = NA_QROWS * GRID_W
NA_KBLK = NA_KROWS * GRID_W
GRID_H = SEQ // GRID_W
NA_NBLK = GRID_H // NA_QROWS

DIL_BLK = 256
DIL_REACH = (max(w for w, _ in DIL_PAIRS) // 2 + DIL_BLK - 1) // DIL_BLK
DIL_NDELTA = 2 * DIL_REACH + 1
DIL_NBLK = SEQ // DIL_BLK

F32 = jnp.float32
BF16 = jnp.bfloat16


def _nbytes(shape, dtype):
    return int(np.prod(shape)) * jnp.dtype(dtype).itemsize


def _params(semantics, pipelined, resident):
    need = 2 * sum(_nbytes(s, d) for s, d in pipelined) + sum(_nbytes(s, d) for s, d in resident)
    assert need < V7X_VMEM_BYTES, need
    return pltpu.CompilerParams(dimension_semantics=semantics,
                                vmem_limit_bytes=min(V7X_VMEM_BYTES, need + need // 4))


def _rms(x, g):
    ms = jnp.mean(x * x, axis=-1, keepdims=True)
    return x * lax.rsqrt(ms + EPS) * g


def _sigmoid(x):
    return 1.0 / (1.0 + jnp.exp(-x))


def _dot(a, b):
    return jnp.dot(a, b, preferred_element_type=F32)


def _dot_nt(a, b):
    return lax.dot_general(a, b, (((1,), (1,)), ((), ())), preferred_element_type=F32)


def _dot_tn(a, b):
    return lax.dot_general(a, b, (((0,), (0,)), ((), ())), preferred_element_type=F32)


def _normmm_body(x_ref, g_ref, w_ref, o_ref, xn_ref, *, act):
    @pl.when(pl.program_id(1) == 0)
    def _():
        xn_ref[...] = _rms(x_ref[...], g_ref[...]).astype(BF16)

    y = _dot(xn_ref[...], w_ref[...])
    if act == "relu2":
        y = jnp.square(jnp.maximum(y, 0.0))
    o_ref[...] = y.astype(o_ref.dtype)


def _normmm(x, g, w, layer, *, tm, tn, act=None, out_dtype=BF16, name):
    m, k = x.shape
    n = w.shape[-1]
    g2 = g[layer].reshape(1, k)
    return pl.pallas_call(
        functools.partial(_normmm_body, act=act),
        grid=(m // tm, n // tn),
        in_specs=[pl.BlockSpec((tm, k), lambda i, j: (i, 0)),
                  pl.BlockSpec((1, k), lambda i, j: (0, 0)),
                  pl.BlockSpec((None, k, tn), lambda i, j: (layer, 0, j))],
        out_specs=pl.BlockSpec((tm, tn), lambda i, j: (i, j)),
        out_shape=jax.ShapeDtypeStruct((m, n), out_dtype),
        scratch_shapes=[pltpu.VMEM((tm, k), BF16)],
        compiler_params=_params(
            ("parallel", "arbitrary"),
            [((tm, k), F32), ((1, k), F32), ((k, tn), BF16), ((tm, tn), out_dtype)],
            [((tm, k), BF16), ((tm, k), F32), ((tm, tn), F32)]),
        name=name,
    )(x, g2, w)


def _mmres_body(a_ref, w_ref, r_ref, o_ref):
    o_ref[...] = r_ref[...] + _dot(a_ref[...], w_ref[...])


def _mmres(a, w, layer, res, *, tm, tn, name):
    m, k = a.shape
    n = w.shape[-1]
    return pl.pallas_call(
        _mmres_body,
        grid=(m // tm, n // tn),
        in_specs=[pl.BlockSpec((tm, k), lambda i, j: (i, 0)),
                  pl.BlockSpec((None, k, tn), lambda i, j: (layer, 0, j)),
                  pl.BlockSpec((tm, tn), lambda i, j: (i, j))],
        out_specs=pl.BlockSpec((tm, tn), lambda i, j: (i, j)),
        out_shape=jax.ShapeDtypeStruct((m, n), F32),
        compiler_params=_params(
            ("parallel", "arbitrary"),
            [((tm, k), BF16), ((k, tn), BF16), ((tm, tn), F32), ((tm, tn), F32)],
            [((tm, tn), F32)]),
        name=name,
    )(a, w, res)


def _na_bias_tiles(rpb):
    tiles = []
    for qrow0, krow0 in ((0, 0), (NA_QROWS, 0), (GRID_H - NA_QROWS, GRID_H - NA_KROWS)):
        qi = np.arange(NA_QBLK)
        kj = np.arange(NA_KBLK)
        r = (qrow0 + qi // GRID_W)[:, None]
        c = (qi % GRID_W)[:, None]
        rk = (krow0 + kj // GRID_W)[None, :]
        ck = (kj % GRID_W)[None, :]
        r0 = np.clip(r - NA_ROWS // 2, 0, GRID_H - NA_ROWS)
        c0 = np.clip(c - NA_COLS // 2, 0, GRID_W - NA_COLS)
        valid = (rk >= r0) & (rk < r0 + NA_ROWS) & (ck >= c0) & (ck < c0 + NA_COLS)
        row_off = np.clip(rk - r + (NA_ROWS - 1), 0, 2 * NA_ROWS - 2)
        col_off = np.clip(ck - c + (NA_COLS - 1), 0, 2 * NA_COLS - 2)
        flat = (row_off * (2 * NA_COLS - 1) + col_off).astype(np.int32)
        vals = jnp.take(rpb.reshape(NA_HEADS, -1), jnp.asarray(flat), axis=1)
        tiles.append(jnp.where(jnp.asarray(valid)[None], vals, NEG))
    return jnp.stack(tiles, axis=1).astype(F32)


def _na_body(q_ref, k_ref, v_ref, bias_ref, o_ref):
    def blk(i, carry):
        qrow = i * NA_QROWS
        krow = jnp.clip(qrow - NA_ROWS // 2, 0, GRID_H - NA_KROWS)
        qs = pl.multiple_of(qrow * GRID_W, NA_QBLK)
        ks = pl.multiple_of(krow * GRID_W, NA_QBLK)
        tile = jnp.where(i == 0, 0, jnp.where(i == NA_NBLK - 1, 2, 1))
        q = q_ref[pl.ds(qs, NA_QBLK), :]
        k = k_ref[pl.ds(ks, NA_KBLK), :]
        v = v_ref[pl.ds(ks, NA_KBLK), :]
        s = _dot_nt(q, k) * ATT_SCALE + bias_ref[tile]
        m = jnp.max(s, axis=-1, keepdims=True)
        p = jnp.exp(s - m)
        l = jnp.sum(p, axis=-1, keepdims=True)
        o = _dot(p.astype(BF16), v) * (1.0 / l)
        o_ref[pl.ds(qs, NA_QBLK), :] = o.astype(o_ref.dtype)
        return carry

    lax.fori_loop(0, NA_NBLK, blk, 0)


def _na_attention(u, bias):
    qb, kb, vb = OFF_QA // HEAD_DIM, OFF_KA // HEAD_DIM, OFF_VA // HEAD_DIM
    blk = (SEQ, HEAD_DIM)
    return pl.pallas_call(
        _na_body,
        grid=(NA_HEADS, BATCH),
        in_specs=[pl.BlockSpec(blk, lambda h, b: (b, qb + h)),
                  pl.BlockSpec(blk, lambda h, b: (b, kb + h)),
                  pl.BlockSpec(blk, lambda h, b: (b, vb + h)),
                  pl.BlockSpec((None, 3, NA_QBLK, NA_KBLK), lambda h, b: (h, 0, 0, 0))],
        out_specs=pl.BlockSpec(blk, lambda h, b: (b, h)),
        out_shape=jax.ShapeDtypeStruct((M_TOK, NA_W), BF16),
        compiler_params=_params(
            ("parallel", "parallel"),
            [(blk, BF16)] * 4 + [((3, NA_QBLK, NA_KBLK), F32)],
            [((NA_QBLK, NA_KBLK), F32)] * 3),
        name="na_attention",
    )(u, u, u, bias)


def _t5_bucket(rel):
    nb = T5_BUCKETS // 2
    ret = (rel > 0).astype(np.int32) * nb
    n = np.abs(rel)
    max_exact = nb // 2
    large = max_exact + (np.log(np.maximum(n, 1) / max_exact) / np.log(T5_MAX_DIST / max_exact)
                         * (nb - max_exact)).astype(np.int32)
    large = np.minimum(large, nb - 1)
    return (ret + np.where(n < max_exact, n, large)).astype(np.int32)


def _dil_bias_tiles(t5_bias):
    qi = np.arange(DIL_BLK)[:, None]
    kj = np.arange(DIL_BLK)[None, :]
    tiles = []
    for delta in range(-DIL_REACH, DIL_REACH + 1):
        off = delta * DIL_BLK + kj - qi
        count = np.zeros(off.shape, np.int32)
        for w, d in DIL_PAIRS:
            count += ((off % d == 0) & (np.abs(off) <= w // 2)).astype(np.int32)
        bucket = _t5_bucket(np.clip(off, -T5_MAX_DIST, T5_MAX_DIST))
        bias = jnp.take(t5_bias.T.astype(F32), jnp.asarray(bucket.reshape(-1)), axis=1)
        bias = bias.reshape(DIL_HEADS, DIL_BLK, DIL_BLK)
        logc = jnp.log(jnp.asarray(np.maximum(count, 1), F32))
        tiles.append(jnp.where(jnp.asarray(count > 0)[None], bias + logc[None], NEG))
    return jnp.stack(tiles, axis=1)


def _dil_body(q_ref, k_ref, v_ref, t_ref, o_ref, s_ref):
    def blk(i, carry):
        qs = pl.multiple_of(i * DIL_BLK, DIL_BLK)
        q = q_ref[pl.ds(qs, DIL_BLK), :]
        starts = []
        for d in range(DIL_NDELTA):
            j = i + (d - DIL_REACH)
            inside = (j >= 0) & (j < DIL_NBLK)
            ks = pl.multiple_of(jnp.clip(j, 0, DIL_NBLK - 1) * DIL_BLK, DIL_BLK)
            starts.append(ks)
            s = _dot_nt(q, k_ref[pl.ds(ks, DIL_BLK), :]) * ATT_SCALE + t_ref[d]
            s_ref[:, d * DIL_BLK:(d + 1) * DIL_BLK] = jnp.where(inside, s, NEG)
        m = jnp.max(s_ref[...], axis=-1, keepdims=True)
        l = jnp.zeros((DIL_BLK, 1), F32)
        acc = jnp.zeros((DIL_BLK, HEAD_DIM), F32)
        for d in range(DIL_NDELTA):
            p = jnp.exp(s_ref[:, d * DIL_BLK:(d + 1) * DIL_BLK] - m)
            l = l + jnp.sum(p, axis=-1, keepdims=True)
            acc = acc + _dot(p.astype(BF16), v_ref[pl.ds(starts[d], DIL_BLK), :])
        o_ref[pl.ds(qs, DIL_BLK), :] = (acc * (1.0 / l)).astype(o_ref.dtype)
        return carry

    lax.fori_loop(0, DIL_NBLK, blk, 0)


def _dil_attention(u, tiles):
    qb, kb, vb = OFF_QC // HEAD_DIM, OFF_KC // HEAD_DIM, OFF_VC // HEAD_DIM
    blk = (SEQ, HEAD_DIM)
    return pl.pallas_call(
        _dil_body,
        grid=(DIL_HEADS, BATCH),
        in_specs=[pl.BlockSpec(blk, lambda h, b: (b, qb + h)),
                  pl.BlockSpec(blk, lambda h, b: (b, kb + h)),
                  pl.BlockSpec(blk, lambda h, b: (b, vb + h)),
                  pl.BlockSpec((None, DIL_NDELTA, DIL_BLK, DIL_BLK), lambda h, b: (h, 0, 0, 0))],
        out_specs=pl.BlockSpec(blk, lambda h, b: (b, h)),
        out_shape=jax.ShapeDtypeStruct((M_TOK, DIL_W), BF16),
        scratch_shapes=[pltpu.VMEM((DIL_BLK, DIL_NDELTA * DIL_BLK), F32)],
        compiler_params=_params(
            ("parallel", "parallel"),
            [(blk, BF16)] * 4 + [((DIL_NDELTA, DIL_BLK, DIL_BLK), F32)],
            [((DIL_BLK, DIL_NDELTA * DIL_BLK), F32)] * 2),
        name="dil_attention",
    )(u, u, u, tiles)


def _rotary_tables():
    inv_freq = jnp.asarray((ROPE_BASE ** (-np.arange(0, RET_DK, 2, dtype=np.float32) / RET_DK)).astype(np.float32))
    ang = jnp.arange(SEQ, dtype=F32)[:, None] * inv_freq[None, :]
    cos, sin = jnp.cos(ang), jnp.sin(ang)
    return jnp.concatenate([cos, cos], axis=-1), jnp.concatenate([-sin, sin], axis=-1)


def _decay_tables(w, forward):
    c = RET_CHUNK
    row = lax.broadcasted_iota(jnp.int32, (c, c), 0)
    col = lax.broadcasted_iota(jnp.int32, (c, c), 1)
    diff = ((row - col) if forward else (col - row)).astype(F32)
    keep = (diff >= 0.0) if forward else (diff > 0.0)
    log_g = -jnp.exp(jnp.full((c, c), w, F32))
    dmask = jnp.where(keep, jnp.exp(jnp.where(keep, diff, 0.0) * log_g), 0.0)
    j = lax.broadcasted_iota(jnp.int32, (c, RET_DV), 0).astype(F32)
    log_g2 = -jnp.exp(jnp.full((c, RET_DV), w, F32))
    xi = jnp.exp(((j + 1.0) if forward else (c - j)) * log_g2)
    zeta = jnp.exp(((c - 1.0 - j) if forward else j) * log_g2)
    g_c = jnp.exp(c * log_g2)
    return dmask, xi, zeta, g_c


def _ret_body(dec_ref, q_ref, k_ref, v_ref, gate_ref, cos_ref, sin_ref, o_ref,
              qr_ref, kr_ref, fwd_ref, state_ref):
    h = pl.program_id(0)
    c = RET_CHUNK
    nchunk = SEQ // c

    def rot(i, carry):
        sl = pl.ds(pl.multiple_of(i * c, c), c)
        cs, sn = cos_ref[sl, :], sin_ref[sl, :]
        q = q_ref[sl, :].astype(F32)
        k = k_ref[sl, :].astype(F32)
        qr_ref[sl, :] = (q * cs + pltpu.roll(q, RET_DK // 2, 1) * sn).astype(BF16)
        kr_ref[sl, :] = ((k * cs + pltpu.roll(k, RET_DK // 2, 1) * sn) * (RET_DK ** -0.5)).astype(BF16)
        return carry

    lax.fori_loop(0, nchunk, rot, 0)

    def chunk(sl, tables):
        dmask, xi, zeta, g_c = tables
        q, k = qr_ref[sl, :], kr_ref[sl, :]
        v = v_ref[sl, :]
        inner = _dot_nt(q, k) * dmask
        state = state_ref[...]
        o = _dot(inner.astype(BF16), v) + _dot(q, state.astype(BF16)) * xi
        state_ref[...] = state * g_c + _dot_tn(k, (v.astype(F32) * zeta).astype(BF16))
        return o

    tables_f = _decay_tables(dec_ref[0, h], True)
    state_ref[...] = jnp.zeros_like(state_ref)

    def fwd(i, carry):
        sl = pl.ds(pl.multiple_of(i * c, c), c)
        fwd_ref[sl, :] = chunk(sl, tables_f)
        return carry

    lax.fori_loop(0, nchunk, fwd, 0)

    tables_b = _decay_tables(dec_ref[1, h], False)
    state_ref[...] = jnp.zeros_like(state_ref)

    def bwd(ii, carry):
        sl = pl.ds(pl.multiple_of((nchunk - 1 - ii) * c, c), c)
        y = fwd_ref[sl, :] + chunk(sl, tables_b)
        mu = jnp.mean(y, axis=-1, keepdims=True)
        yc = y - mu
        var = jnp.mean(yc * yc, axis=-1, keepdims=True)
        gate = gate_ref[sl, :].astype(F32)
        o_ref[sl, :] = (gate * _sigmoid(gate) * (yc * lax.rsqrt(var + EPS))).astype(o_ref.dtype)
        return carry

    lax.fori_loop(0, nchunk, bwd, 0)


def _retention(u, decay, cos2, sin2):
    qb, kb = OFF_QB // RET_DK, OFF_KB // RET_DK
    vb, gb = OFF_VB // RET_DV, OFF_GB // RET_DV
    qk_blk, v_blk = (SEQ, RET_DK), (SEQ, RET_DV)
    return pl.pallas_call(
        _ret_body,
        grid=(RET_HEADS, BATCH),
        in_specs=[pl.BlockSpec(memory_space=pltpu.SMEM),
                  pl.BlockSpec(qk_blk, lambda h, b: (b, qb + h)),
                  pl.BlockSpec(qk_blk, lambda h, b: (b, kb + h)),
                  pl.BlockSpec(v_blk, lambda h, b: (b, vb + h)),
                  pl.BlockSpec(v_blk, lambda h, b: (b, gb + h)),
                  pl.BlockSpec(qk_blk, lambda h, b: (0, 0)),
                  pl.BlockSpec(qk_blk, lambda h, b: (0, 0))],
        out_specs=pl.BlockSpec(v_blk, lambda h, b: (b, h)),
        out_shape=jax.ShapeDtypeStruct((M_TOK, RET_V_W), BF16),
        scratch_shapes=[pltpu.VMEM(qk_blk, BF16), pltpu.VMEM(qk_blk, BF16),
                        pltpu.VMEM(v_blk, F32), pltpu.VMEM((RET_DK, RET_DV), F32)],
        compiler_params=_params(
            ("parallel", "parallel"),
            [(qk_blk, BF16)] * 2 + [(v_blk, BF16)] * 3 + [(qk_blk, F32)] * 2,
            [(qk_blk, BF16)] * 2 + [(v_blk, F32), ((RET_DK, RET_DV), F32)]),
        name="retention",
    )(decay, u, u, u, u, cos2, sin2)


def _merge_body(oa_ref, ob_ref, oc_ref, wa_ref, wb_ref, wc_ref, sa_ref, sb_ref, sc_ref, o_ref):
    def branch(o, w, s):
        return _sigmoid(s[...].astype(F32)) * _dot(o[...], w[...])

    o_ref[...] = (branch(oa_ref, wa_ref, sa_ref) + branch(ob_ref, wb_ref, sb_ref)
                  + branch(oc_ref, wc_ref, sc_ref)).astype(o_ref.dtype)


def _merge(o_a, o_b, o_c, w_a, w_b, w_c, layer, u, *, tm, tn):
    sa, sb, sc = OFF_SA // tn, OFF_SB // tn, OFF_SC // tn

    def wspec(k):
        return pl.BlockSpec((None, k, tn), lambda i, j: (layer, 0, j))

    def ospec(k):
        return pl.BlockSpec((tm, k), lambda i, j: (i, 0))

    return pl.pallas_call(
        _merge_body,
        grid=(M_TOK // tm, D_MODEL // tn),
        in_specs=[ospec(NA_W), ospec(RET_V_W), ospec(DIL_W), wspec(NA_W), wspec(RET_V_W), wspec(DIL_W),
                  pl.BlockSpec((tm, tn), lambda i, j: (i, sa + j)),
                  pl.BlockSpec((tm, tn), lambda i, j: (i, sb + j)),
                  pl.BlockSpec((tm, tn), lambda i, j: (i, sc + j))],
        out_specs=pl.BlockSpec((tm, tn), lambda i, j: (i, j)),
        out_shape=jax.ShapeDtypeStruct((M_TOK, D_MODEL), BF16),
        compiler_params=_params(
            ("parallel", "arbitrary"),
            [((tm, NA_W + RET_V_W + DIL_W), BF16), ((NA_W + RET_V_W + DIL_W, tn), BF16),
             ((tm, tn), BF16)] + [((tm, tn), BF16)] * 3,
            [((tm, tn), F32)] * 4),
        name="branch_merge",
    )(o_a, o_b, o_c, w_a, w_b, w_c, u, u, u)


def _cross_body(x_ref, g_ref, wq_ref, kv_ref, wo_ref, o_ref):
    x = x_ref[...]
    q = _dot(_rms(x, g_ref[...]).astype(BF16), wq_ref[...]).astype(BF16)
    heads = []
    for h in range(CROSS_HEADS):
        lo = h * HEAD_DIM
        k = kv_ref[:, lo:lo + HEAD_DIM]
        v = kv_ref[:, CROSS_W + lo:CROSS_W + lo + HEAD_DIM]
        s = _dot_nt(q[:, lo:lo + HEAD_DIM], k) * ATT_SCALE
        m = jnp.max(s, axis=-1, keepdims=True)
        p = jnp.exp(s - m)
        l = jnp.sum(p, axis=-1, keepdims=True)
        heads.append((_dot(p.astype(BF16), v) * (1.0 / l)).astype(BF16))
    o_ref[...] = x + _dot(jnp.concatenate(heads, axis=-1), wo_ref[...])


def _cross(x, g, w_q, kv, w_o, layer, *, tm):
    per_batch = SEQ // tm
    g2 = g[layer].reshape(1, D_MODEL)
    return pl.pallas_call(
        _cross_body,
        grid=(M_TOK // tm,),
        in_specs=[pl.BlockSpec((tm, D_MODEL), lambda i: (i, 0)),
                  pl.BlockSpec((1, D_MODEL), lambda i: (0, 0)),
                  pl.BlockSpec((None, D_MODEL, CROSS_W), lambda i: (layer, 0, 0)),
                  pl.BlockSpec((MEM_LEN, 2 * CROSS_W), lambda i: (i // per_batch, 0)),
                  pl.BlockSpec((None, CROSS_W, D_MODEL), lambda i: (layer, 0, 0))],
        out_specs=pl.BlockSpec((tm, D_MODEL), lambda i: (i, 0)),
        out_shape=jax.ShapeDtypeStruct((M_TOK, D_MODEL), F32),
        compiler_params=_params(
            ("parallel",),
            [((tm, D_MODEL), F32)] * 2 + [((D_MODEL, CROSS_W), BF16), ((MEM_LEN, 2 * CROSS_W), BF16),
                                          ((CROSS_W, D_MODEL), BF16)],
            [((tm, D_MODEL), F32)] * 2 + [((tm, CROSS_W), F32)] * 2 + [((tm, MEM_LEN), F32)] * 2),
        name="cross_attention",
    )(x, g2, w_q, kv, w_o)


def _final_norm_body(x_ref, g_ref, o_ref):
    o_ref[...] = _rms(x_ref[...], g_ref[...])


def _final_norm(x, g, *, tm):
    return pl.pallas_call(
        _final_norm_body,
        grid=(M_TOK // tm,),
        in_specs=[pl.BlockSpec((tm, D_MODEL), lambda i: (i, 0)),
                  pl.BlockSpec((1, D_MODEL), lambda i: (0, 0))],
        out_specs=pl.BlockSpec((tm, D_MODEL), lambda i: (i, 0)),
        out_shape=jax.ShapeDtypeStruct((M_TOK, D_MODEL), F32),
        compiler_params=_params(("parallel",), [((tm, D_MODEL), F32)] * 2, [((tm, D_MODEL), F32)]),
        name="final_norm",
    )(x, g.reshape(1, D_MODEL))


def kernel(x, mem, t5_bias, norm_mix_g, w_in, na_rpb, ret_decay, w_branch, w_out, norm_cross_g, norm_mem_g,
           w_cq, w_ckv, w_co, norm_mlp_g, w_mlp1, w_mlp2, final_norm_g):
    xs = x.reshape(M_TOK, D_MODEL)
    mem2 = mem.reshape(BATCH * MEM_LEN, D_MODEL)
    w_in_h, w_out_h = w_in.astype(BF16), w_out.astype(BF16)
    w_up_a = w_branch[:, :NA_W].astype(BF16)
    w_up_b = w_branch[:, NA_W:NA_W + RET_V_W].astype(BF16)
    w_up_c = w_branch[:, NA_W + RET_V_W:].astype(BF16)
    w_cq_h, w_ckv_h, w_co_h = w_cq.astype(BF16), w_ckv.astype(BF16), w_co.astype(BF16)
    w_mlp1_h, w_mlp2_h = w_mlp1.astype(BF16), w_mlp2.astype(BF16)
    cos2, sin2 = _rotary_tables()
    dil_tiles = _dil_bias_tiles(t5_bias)

    for layer in range(DEPTH):
        u = _normmm(xs, norm_mix_g, w_in_h, layer, tm=1024, tn=512, name="in_proj")
        o_a = _na_attention(u, _na_bias_tiles(na_rpb[layer]))
        o_b = _retention(u, ret_decay[layer], cos2, sin2)
        o_c = _dil_attention(u, dil_tiles)
        merged = _merge(o_a, o_b, o_c, w_up_a, w_up_b, w_up_c, layer, u, tm=1024, tn=512)
        xs = _mmres(merged, w_out_h, layer, xs, tm=1024, tn=512, name="out_proj")
        kv = _normmm(mem2, norm_mem_g, w_ckv_h, layer, tm=BATCH * MEM_LEN, tn=512, name="mem_kv_proj")
        xs = _cross(xs, norm_cross_g, w_cq_h, kv, w_co_h, layer, tm=512)
        hid = _normmm(xs, norm_mlp_g, w_mlp1_h, layer, tm=1024, tn=512, act="relu2", name="mlp_up")
        xs = _mmres(hid, w_mlp2_h, layer, xs, tm=512, tn=512, name="mlp_down")
    return _final_norm(xs, final_norm_g, tm=512).reshape(BATCH, SEQ, D_MODEL)
```

```python
import functools

import numpy as np
import jax
import jax.numpy as jnp
from jax import lax
from jax.experimental import pallas as pl
from jax.experimental.pallas import tpu as pltpu

D_MODEL = 2048
BATCH = 2
SEQ = 4096
DEPTH = 4
MEM_LEN = 256
HEAD_DIM = 128
GRID_W = 64
NA_HEADS = 6
NA_ROWS = 8
NA_COLS = 16
RET_HEADS = 4
RET_DK = 128
RET_DV = 256
RET_CHUNK = 128
ROPE_BASE = 10000.0
DIL_HEADS = 6
DIL_PAIRS = ((128, 1), (512, 4), (2048, 16))
T5_BUCKETS = 32
T5_MAX_DIST = 1024
CROSS_HEADS = 4
D_FF = 4 * D_MODEL
EPS = 1e-6
NEG = -1e30

NA_W = NA_HEADS * HEAD_DIM
RET_QK_W = RET_HEADS * RET_DK
RET_V_W = RET_HEADS * RET_DV
DIL_W = DIL_HEADS * HEAD_DIM
CROSS_W = CROSS_HEADS * HEAD_DIM
IN_SPLITS = (NA_W, NA_W, NA_W, RET_QK_W, RET_QK_W, RET_V_W, RET_V_W,
             DIL_W, DIL_W, DIL_W, D_MODEL, D_MODEL, D_MODEL)
IN_W = sum(IN_SPLITS)
(OFF_QA, OFF_KA, OFF_VA, OFF_QB, OFF_KB, OFF_VB, OFF_GB,
 OFF_QC, OFF_KC, OFF_VC, OFF_SA, OFF_SB, OFF_SC) = [int(o) for o in np.cumsum((0,) + IN_SPLITS[:-1])]

M_TOK = BATCH * SEQ
ATT_SCALE = HEAD_DIM ** -0.5

V7X_VMEM_BYTES = 64 * 1024 * 1024

NA_QROWS = 4
NA_KROWS = NA_QROWS + NA_ROWS
NA_QBLK = NA_QROWS * GRID_W
NA_KBLK = NA_KROWS * GRID_W
GRID_H = SEQ // GRID_W
NA_NBLK = GRID_H // NA_QROWS

DIL_BLK = 256
DIL_REACH = (max(w for w, _ in DIL_PAIRS) // 2 + DIL_BLK - 1) // DIL_BLK
DIL_NDELTA = 2 * DIL_REACH + 1
DIL_NBLK = SEQ // DIL_BLK

F32 = jnp.float32
BF16 = jnp.bfloat16


def _nbytes(shape, dtype):
    return int(np.prod(shape)) * jnp.dtype(dtype).itemsize


def _params(semantics, pipelined, resident):
    need = 2 * sum(_nbytes(s, d) for s, d in pipelined) + sum(_nbytes(s, d) for s, d in resident)
    assert need < V7X_VMEM_BYTES, need
    return pltpu.CompilerParams(dimension_semantics=semantics,
                                vmem_limit_bytes=min(V7X_VMEM_BYTES, need + need // 4))


def _rms(x, g):
    ms = jnp.mean(x * x, axis=-1, keepdims=True)
    return x * lax.rsqrt(ms + EPS) * g


def _sigmoid(x):
    return 1.0 / (1.0 + jnp.exp(-x))


def _dot(a, b):
    return jnp.dot(a, b, preferred_element_type=F32)


def _dot_nt(a, b):
    return lax.dot_general(a, b, (((1,), (1,)), ((), ())), preferred_element_type=F32)


def _dot_tn(a, b):
    return lax.dot_general(a, b, (((0,), (0,)), ((), ())), preferred_element_type=F32)


def _normmm_body(x_ref, g_ref, w_ref, o_ref, xn_ref, *, act):
    @pl.when(pl.program_id(1) == 0)
    def _():
        xn_ref[...] = _rms(x_ref[...], g_ref[...]).astype(BF16)

    y = _dot(xn_ref[...], w_ref[...])
    if act == "relu2":
        y = jnp.square(jnp.maximum(y, 0.0))
    o_ref[...] = y.astype(o_ref.dtype)


def _normmm(x, g, w, layer, *, tm, tn, act=None, out_dtype=BF16, name):
    m, k = x.shape
    n = w.shape[-1]
    g2 = g[layer].reshape(1, k)
    return pl.pallas_call(
        functools.partial(_normmm_body, act=act),
        grid=(m // tm, n // tn),
        in_specs=[pl.BlockSpec((tm, k), lambda i, j: (i, 0)),
                  pl.BlockSpec((1, k), lambda i, j: (0, 0)),
                  pl.BlockSpec((None, k, tn), lambda i, j: (layer, 0, j))],
        out_specs=pl.BlockSpec((tm, tn), lambda i, j: (i, j)),
        out_shape=jax.ShapeDtypeStruct((m, n), out_dtype),
        scratch_shapes=[pltpu.VMEM((tm, k), BF16)],
        compiler_params=_params(
            ("parallel", "arbitrary"),
            [((tm, k), F32), ((1, k), F32), ((k, tn), BF16), ((tm, tn), out_dtype)],
            [((tm, k), BF16), ((tm, k), F32), ((tm, tn), F32)]),
        name=name,
    )(x, g2, w)


def _mmres_body(a_ref, w_ref, r_ref, o_ref):
    o_ref[...] = r_ref[...] + _dot(a_ref[...], w_ref[...])


def _mmres(a, w, layer, res, *, tm, tn, name):
    m, k = a.shape
    n = w.shape[-1]
    return pl.pallas_call(
        _mmres_body,
        grid=(m // tm, n // tn),
        in_specs=[pl.BlockSpec((tm, k), lambda i, j: (i, 0)),
                  pl.BlockSpec((None, k, tn), lambda i, j: (layer, 0, j)),
                  pl.BlockSpec((tm, tn), lambda i, j: (i, j))],
        out_specs=pl.BlockSpec((tm, tn), lambda i, j: (i, j)),
        out_shape=jax.ShapeDtypeStruct((m, n), F32),
        compiler_params=_params(
            ("parallel", "arbitrary"),
            [((tm, k), BF16), ((k, tn), BF16), ((tm, tn), F32), ((tm, tn), F32)],
            [((tm, tn), F32)]),
        name=name,
    )(a, w, res)


NA_RPB_ROWS = 2 * NA_ROWS - 1
NA_RPB_COLS = 2 * NA_COLS - 1
LANES = 128
SUBLANES = 8
NA_TILE_ROWS = ((0, 0), (NA_QROWS, 0), (GRID_H - NA_QROWS, GRID_H - NA_KROWS))


def _na_tiles_body(rpb_ref, o_ref):
    shape = (GRID_W, LANES)
    lane = lax.broadcasted_iota(jnp.int32, shape, 1)
    cq = lax.broadcasted_iota(jnp.int32, shape, 0)
    ck = lane & (GRID_W - 1)
    c0 = jnp.clip(cq - NA_COLS // 2, 0, GRID_W - NA_COLS)
    col_ok = (ck >= c0) & (ck < c0 + NA_COLS)
    left = lane < GRID_W
    neg = jnp.full(shape, NEG, F32)
    lo, hi = [], []
    for a in range(NA_RPB_ROWS):
        row = jnp.broadcast_to(rpb_ref[a:a + 1, :], shape)
        lo.append(pltpu.roll(row, LANES - (NA_COLS - 1), 1, stride=1, stride_axis=0))
        hi.append(pltpu.roll(row, GRID_W - (NA_COLS - 1), 1, stride=1, stride_axis=0))
    for t, (qrow0, krow0) in enumerate(NA_TILE_ROWS):
        for rq in range(NA_QROWS):
            r = qrow0 + rq
            r0 = min(max(r - NA_ROWS // 2, 0), GRID_H - NA_ROWS)
            for pair in range(NA_KROWS // 2):
                halves = []
                for side, table in ((0, lo), (1, hi)):
                    rk = krow0 + 2 * pair + side
                    halves.append(table[rk - r + NA_ROWS - 1] if r0 <= rk < r0 + NA_ROWS else neg)
                blk = jnp.where(col_ok, jnp.where(left, halves[0], halves[1]), NEG)
                o_ref[t, rq * GRID_W:(rq + 1) * GRID_W, pair * LANES:(pair + 1) * LANES] = blk


def _na_bias_tiles(rpb):
    padded = jnp.pad(rpb.astype(F32), ((0, 0), (0, 2 * SUBLANES - NA_RPB_ROWS), (0, LANES - NA_RPB_COLS)))
    return pl.pallas_call(
        _na_tiles_body,
        grid=(NA_HEADS,),
        in_specs=[pl.BlockSpec((None, 2 * SUBLANES, LANES), lambda h: (h, 0, 0))],
        out_specs=pl.BlockSpec((None, 3, NA_QBLK, NA_KBLK), lambda h: (h, 0, 0, 0)),
        out_shape=jax.ShapeDtypeStruct((NA_HEADS, 3, NA_QBLK, NA_KBLK), F32),
        compiler_params=_params(("parallel",), [((3, NA_QBLK, NA_KBLK), F32)], [((NA_QBLK, NA_KBLK), F32)]),
        name="na_bias_tiles",
    )(padded)


def _na_body(q_ref, k_ref, v_ref, bias_ref, o_ref):
    def blk(i, carry):
        qrow = i * NA_QROWS
        krow = jnp.clip(qrow - NA_ROWS // 2, 0, GRID_H - NA_KROWS)
        qs = pl.multiple_of(qrow * GRID_W, NA_QBLK)
        ks = pl.multiple_of(krow * GRID_W, NA_QBLK)
        tile = jnp.where(i == 0, 0, jnp.where(i == NA_NBLK - 1, 2, 1))
        q = q_ref[pl.ds(qs, NA_QBLK), :]
        k = k_ref[pl.ds(ks, NA_KBLK), :]
        v = v_ref[pl.ds(ks, NA_KBLK), :]
        s = _dot_nt(q, k) * ATT_SCALE + bias_ref[tile]
        m = jnp.max(s, axis=-1, keepdims=True)
        p = jnp.exp(s - m)
        l = jnp.sum(p, axis=-1, keepdims=True)
        o = _dot(p.astype(BF16), v) * (1.0 / l)
        o_ref[pl.ds(qs, NA_QBLK), :] = o.astype(o_ref.dtype)
        return carry

    lax.fori_loop(0, NA_NBLK, blk, 0)


def _na_attention(u, bias):
    qb, kb, vb = OFF_QA // HEAD_DIM, OFF_KA // HEAD_DIM, OFF_VA // HEAD_DIM
    blk = (SEQ, HEAD_DIM)
    return pl.pallas_call(
        _na_body,
        grid=(NA_HEADS, BATCH),
        in_specs=[pl.BlockSpec(blk, lambda h, b: (b, qb + h)),
                  pl.BlockSpec(blk, lambda h, b: (b, kb + h)),
                  pl.BlockSpec(blk, lambda h, b: (b, vb + h)),
                  pl.BlockSpec((None, 3, NA_QBLK, NA_KBLK), lambda h, b: (h, 0, 0, 0))],
        out_specs=pl.BlockSpec(blk, lambda h, b: (b, h)),
        out_shape=jax.ShapeDtypeStruct((M_TOK, NA_W), BF16),
        compiler_params=_params(
            ("parallel", "parallel"),
            [(blk, BF16)] * 4 + [((3, NA_QBLK, NA_KBLK), F32)],
            [((NA_QBLK, NA_KBLK), F32)] * 3),
        name="na_attention",
    )(u, u, u, bias)


def _t5_bucket(rel):
    nb = T5_BUCKETS // 2
    ret = (rel > 0).astype(np.int32) * nb
    n = np.abs(rel)
    max_exact = nb // 2
    large = max_exact + (np.log(np.maximum(n, 1) / max_exact) / np.log(T5_MAX_DIST / max_exact)
                         * (nb - max_exact)).astype(np.int32)
    large = np.minimum(large, nb - 1)
    return (ret + np.where(n < max_exact, n, large)).astype(np.int32)


def _dil_tiles_body(f_ref, o_ref):
    for d in range(DIL_NDELTA):
        row = jnp.broadcast_to(f_ref[d:d + 1, :], (DIL_BLK, 2 * DIL_BLK))
        o_ref[d] = pltpu.roll(row, DIL_BLK, 1, stride=1, stride_axis=0)[:, :DIL_BLK]


def _dil_bias_tiles(t5_bias):
    delta = np.arange(-DIL_REACH, DIL_REACH + 1)[:, None]
    off = delta * DIL_BLK + np.arange(-DIL_BLK, DIL_BLK)[None, :]
    count = np.zeros(off.shape, np.int32)
    for w, d in DIL_PAIRS:
        count += ((off % d == 0) & (np.abs(off) <= w // 2)).astype(np.int32)
    bucket = _t5_bucket(np.clip(off, -T5_MAX_DIST, T5_MAX_DIST))
    bias = jnp.take(t5_bias.T.astype(F32), jnp.asarray(bucket.reshape(-1)), axis=1).reshape((DIL_HEADS,) + off.shape)
    logc = jnp.log(jnp.asarray(np.maximum(count, 1), F32))
    f = jnp.where(jnp.asarray(count > 0)[None], bias + logc[None], NEG)
    f = jnp.pad(f, ((0, 0), (0, 2 * SUBLANES - DIL_NDELTA), (0, 0)))
    return pl.pallas_call(
        _dil_tiles_body,
        grid=(DIL_HEADS,),
        in_specs=[pl.BlockSpec((None, 2 * SUBLANES, 2 * DIL_BLK), lambda h: (h, 0, 0))],
        out_specs=pl.BlockSpec((None, DIL_NDELTA, DIL_BLK, DIL_BLK), lambda h: (h, 0, 0, 0)),
        out_shape=jax.ShapeDtypeStruct((DIL_HEADS, DIL_NDELTA, DIL_BLK, DIL_BLK), F32),
        compiler_params=_params(("parallel",), [((DIL_NDELTA, DIL_BLK, DIL_BLK), F32)],
                                [((DIL_BLK, 2 * DIL_BLK), F32)] * 2),
        name="dil_bias_tiles",
    )(f)


def _dil_body(q_ref, k_ref, v_ref, t_ref, o_ref, s_ref):
    def blk(i, carry):
        qs = pl.multiple_of(i * DIL_BLK, DIL_BLK)
        q = q_ref[pl.ds(qs, DIL_BLK), :]
        starts = []
        for d in range(DIL_NDELTA):
            j = i + (d - DIL_REACH)
            inside = (j >= 0) & (j < DIL_NBLK)
            ks = pl.multiple_of(jnp.clip(j, 0, DIL_NBLK - 1) * DIL_BLK, DIL_BLK)
            starts.append(ks)
            s = _dot_nt(q, k_ref[pl.ds(ks, DIL_BLK), :]) * ATT_SCALE + t_ref[d]
            s_ref[:, d * DIL_BLK:(d + 1) * DIL_BLK] = jnp.where(inside, s, NEG)
        m = jnp.max(s_ref[...], axis=-1, keepdims=True)
        l = jnp.zeros((DIL_BLK, 1), F32)
        acc = jnp.zeros((DIL_BLK, HEAD_DIM), F32)
        for d in range(DIL_NDELTA):
            p = jnp.exp(s_ref[:, d * DIL_BLK:(d + 1) * DIL_BLK] - m)
            l = l + jnp.sum(p, axis=-1, keepdims=True)
            acc = acc + _dot(p.astype(BF16), v_ref[pl.ds(starts[d], DIL_BLK), :])
        o_ref[pl.ds(qs, DIL_BLK), :] = (acc * (1.0 / l)).astype(o_ref.dtype)
        return carry

    lax.fori_loop(0, DIL_NBLK, blk, 0)


def _dil_attention(u, tiles):
    qb, kb, vb = OFF_QC // HEAD_DIM, OFF_KC // HEAD_DIM, OFF_VC // HEAD_DIM
    blk = (SEQ, HEAD_DIM)
    return pl.pallas_call(
        _dil_body,
        grid=(DIL_HEADS, BATCH),
        in_specs=[pl.BlockSpec(blk, lambda h, b: (b, qb + h)),
                  pl.BlockSpec(blk, lambda h, b: (b, kb + h)),
                  pl.BlockSpec(blk, lambda h, b: (b, vb + h)),
                  pl.BlockSpec((None, DIL_NDELTA, DIL_BLK, DIL_BLK), lambda h, b: (h, 0, 0, 0))],
        out_specs=pl.BlockSpec(blk, lambda h, b: (b, h)),
        out_shape=jax.ShapeDtypeStruct((M_TOK, DIL_W), BF16),
        scratch_shapes=[pltpu.VMEM((DIL_BLK, DIL_NDELTA * DIL_BLK), F32)],
        compiler_params=_params(
            ("parallel", "parallel"),
            [(blk, BF16)] * 4 + [((DIL_NDELTA, DIL_BLK, DIL_BLK), F32)],
            [((DIL_BLK, DIL_NDELTA * DIL_BLK), F32)] * 2),
        name="dil_attention",
    )(u, u, u, tiles)


def _rotary_tables():
    inv_freq = jnp.asarray((ROPE_BASE ** (-np.arange(0, RET_DK, 2, dtype=np.float32) / RET_DK)).astype(np.float32))
    ang = jnp.arange(SEQ, dtype=F32)[:, None] * inv_freq[None, :]
    cos, sin = jnp.cos(ang), jnp.sin(ang)
    return jnp.concatenate([cos, cos], axis=-1), jnp.concatenate([-sin, sin], axis=-1)


def _decay_tables(w, forward):
    c = RET_CHUNK
    row = lax.broadcasted_iota(jnp.int32, (c, c), 0)
    col = lax.broadcasted_iota(jnp.int32, (c, c), 1)
    diff = ((row - col) if forward else (col - row)).astype(F32)
    keep = (diff >= 0.0) if forward else (diff > 0.0)
    log_g = -jnp.exp(jnp.full((c, c), w, F32))
    dmask = jnp.where(keep, jnp.exp(jnp.where(keep, diff, 0.0) * log_g), 0.0)
    j = lax.broadcasted_iota(jnp.int32, (c, RET_DV), 0).astype(F32)
    log_g2 = -jnp.exp(jnp.full((c, RET_DV), w, F32))
    xi = jnp.exp(((j + 1.0) if forward else (c - j)) * log_g2)
    zeta = jnp.exp(((c - 1.0 - j) if forward else j) * log_g2)
    g_c = jnp.exp(c * log_g2)
    return dmask, xi, zeta, g_c


def _ret_body(dec_ref, q_ref, k_ref, v_ref, gate_ref, cos_ref, sin_ref, o_ref,
              qr_ref, kr_ref, fwd_ref, state_ref):
    h = pl.program_id(0)
    c = RET_CHUNK
    nchunk = SEQ // c

    def rot(i, carry):
        sl = pl.ds(pl.multiple_of(i * c, c), c)
        cs, sn = cos_ref[sl, :], sin_ref[sl, :]
        q = q_ref[sl, :].astype(F32)
        k = k_ref[sl, :].astype(F32)
        qr_ref[sl, :] = (q * cs + pltpu.roll(q, RET_DK // 2, 1) * sn).astype(BF16)
        kr_ref[sl, :] = ((k * cs + pltpu.roll(k, RET_DK // 2, 1) * sn) * (RET_DK ** -0.5)).astype(BF16)
        return carry

    lax.fori_loop(0, nchunk, rot, 0)

    def chunk(sl, tables):
        dmask, xi, zeta, g_c = tables
        q, k = qr_ref[sl, :], kr_ref[sl, :]
        v = v_ref[sl, :]
        inner = _dot_nt(q, k) * dmask
        state = state_ref[...]
        o = _dot(inner.astype(BF16), v) + _dot(q, state.astype(BF16)) * xi
        state_ref[...] = state * g_c + _dot_tn(k, (v.astype(F32) * zeta).astype(BF16))
        return o

    tables_f = _decay_tables(dec_ref[0, h], True)
    state_ref[...] = jnp.zeros_like(state_ref)

    def fwd(i, carry):
        sl = pl.ds(pl.multiple_of(i * c, c), c)
        fwd_ref[sl, :] = chunk(sl, tables_f)
        return carry

    lax.fori_loop(0, nchunk, fwd, 0)

    tables_b = _decay_tables(dec_ref[1, h], False)
    state_ref[...] = jnp.zeros_like(state_ref)

    def bwd(ii, carry):
        sl = pl.ds(pl.multiple_of((nchunk - 1 - ii) * c, c), c)
        y = fwd_ref[sl, :] + chunk(sl, tables_b)
        mu = jnp.mean(y, axis=-1, keepdims=True)
        yc = y - mu
        var = jnp.mean(yc * yc, axis=-1, keepdims=True)
        gate = gate_ref[sl, :].astype(F32)
        o_ref[sl, :] = (gate * _sigmoid(gate) * (yc * lax.rsqrt(var + EPS))).astype(o_ref.dtype)
        return carry

    lax.fori_loop(0, nchunk, bwd, 0)


def _retention(u, decay, cos2, sin2):
    qb, kb = OFF_QB // RET_DK, OFF_KB // RET_DK
    vb, gb = OFF_VB // RET_DV, OFF_GB // RET_DV
    qk_blk, v_blk = (SEQ, RET_DK), (SEQ, RET_DV)
    return pl.pallas_call(
        _ret_body,
        grid=(RET_HEADS, BATCH),
        in_specs=[pl.BlockSpec(memory_space=pltpu.SMEM),
                  pl.BlockSpec(qk_blk, lambda h, b: (b, qb + h)),
                  pl.BlockSpec(qk_blk, lambda h, b: (b, kb + h)),
                  pl.BlockSpec(v_blk, lambda h, b: (b, vb + h)),
                  pl.BlockSpec(v_blk, lambda h, b: (b, gb + h)),
                  pl.BlockSpec(qk_blk, lambda h, b: (0, 0)),
                  pl.BlockSpec(qk_blk, lambda h, b: (0, 0))],
        out_specs=pl.BlockSpec(v_blk, lambda h, b: (b, h)),
        out_shape=jax.ShapeDtypeStruct((M_TOK, RET_V_W), BF16),
        scratch_shapes=[pltpu.VMEM(qk_blk, BF16), pltpu.VMEM(qk_blk, BF16),
                        pltpu.VMEM(v_blk, F32), pltpu.VMEM((RET_DK, RET_DV), F32)],
        compiler_params=_params(
            ("parallel", "parallel"),
            [(qk_blk, BF16)] * 2 + [(v_blk, BF16)] * 3 + [(qk_blk, F32)] * 2,
            [(qk_blk, BF16)] * 2 + [(v_blk, F32), ((RET_DK, RET_DV), F32)]),
        name="retention",
    )(decay, u, u, u, u, cos2, sin2)


def _merge_body(oa_ref, ob_ref, oc_ref, wa_ref, wb_ref, wc_ref, sa_ref, sb_ref, sc_ref, o_ref):
    def branch(o, w, s):
        return _sigmoid(s[...].astype(F32)) * _dot(o[...], w[...])

    o_ref[...] = (branch(oa_ref, wa_ref, sa_ref) + branch(ob_ref, wb_ref, sb_ref)
                  + branch(oc_ref, wc_ref, sc_ref)).astype(o_ref.dtype)


def _merge(o_a, o_b, o_c, w_a, w_b, w_c, layer, u, *, tm, tn):
    sa, sb, sc = OFF_SA // tn, OFF_SB // tn, OFF_SC // tn

    def wspec(k):
        return pl.BlockSpec((None, k, tn), lambda i, j: (layer, 0, j))

    def ospec(k):
        return pl.BlockSpec((tm, k), lambda i, j: (i, 0))

    return pl.pallas_call(
        _merge_body,
        grid=(M_TOK // tm, D_MODEL // tn),
        in_specs=[ospec(NA_W), ospec(RET_V_W), ospec(DIL_W), wspec(NA_W), wspec(RET_V_W), wspec(DIL_W),
                  pl.BlockSpec((tm, tn), lambda i, j: (i, sa + j)),
                  pl.BlockSpec((tm, tn), lambda i, j: (i, sb + j)),
                  pl.BlockSpec((tm, tn), lambda i, j: (i, sc + j))],
        out_specs=pl.BlockSpec((tm, tn), lambda i, j: (i, j)),
        out_shape=jax.ShapeDtypeStruct((M_TOK, D_MODEL), BF16),
        compiler_params=_params(
            ("parallel", "arbitrary"),
            [((tm, NA_W + RET_V_W + DIL_W), BF16), ((NA_W + RET_V_W + DIL_W, tn), BF16),
             ((tm, tn), BF16)] + [((tm, tn), BF16)] * 3,
            [((tm, tn), F32)] * 4),
        name="branch_merge",
    )(o_a, o_b, o_c, w_a, w_b, w_c, u, u, u)


def _cross_body(x_ref, g_ref, wq_ref, kv_ref, wo_ref, o_ref):
    x = x_ref[...]
    q = _dot(_rms(x, g_ref[...]).astype(BF16), wq_ref[...]).astype(BF16)
    heads = []
    for h in range(CROSS_HEADS):
        lo = h * HEAD_DIM
        k = kv_ref[:, lo:lo + HEAD_DIM]
        v = kv_ref[:, CROSS_W + lo:CROSS_W + lo + HEAD_DIM]
        s = _dot_nt(q[:, lo:lo + HEAD_DIM], k) * ATT_SCALE
        m = jnp.max(s, axis=-1, keepdims=True)
        p = jnp.exp(s - m)
        l = jnp.sum(p, axis=-1, keepdims=True)
        heads.append((_dot(p.astype(BF16), v) * (1.0 / l)).astype(BF16))
    o_ref[...] = x + _dot(jnp.concatenate(heads, axis=-1), wo_ref[...])


def _cross(x, g, w_q, kv, w_o, layer, *, tm):
    per_batch = SEQ // tm
    g2 = g[layer].reshape(1, D_MODEL)
    return pl.pallas_call(
        _cross_body,
        grid=(M_TOK // tm,),
        in_specs=[pl.BlockSpec((tm, D_MODEL), lambda i: (i, 0)),
                  pl.BlockSpec((1, D_MODEL), lambda i: (0, 0)),
                  pl.BlockSpec((None, D_MODEL, CROSS_W), lambda i: (layer, 0, 0)),
                  pl.BlockSpec((MEM_LEN, 2 * CROSS_W), lambda i: (i // per_batch, 0)),
                  pl.BlockSpec((None, CROSS_W, D_MODEL), lambda i: (layer, 0, 0))],
        out_specs=pl.BlockSpec((tm, D_MODEL), lambda i: (i, 0)),
        out_shape=jax.ShapeDtypeStruct((M_TOK, D_MODEL), F32),
        compiler_params=_params(
            ("parallel",),
            [((tm, D_MODEL), F32)] * 2 + [((D_MODEL, CROSS_W), BF16), ((MEM_LEN, 2 * CROSS_W), BF16),
                                          ((CROSS_W, D_MODEL), BF16)],
            [((tm, D_MODEL), F32)] * 2 + [((tm, CROSS_W), F32)] * 2 + [((tm, MEM_LEN), F32)] * 2),
        name="cross_attention",
    )(x, g2, w_q, kv, w_o)


def _final_norm_body(x_ref, g_ref, o_ref):
    o_ref[...] = _rms(x_ref[...], g_ref[...])


def _final_norm(x, g, *, tm):
    return pl.pallas_call(
        _final_norm_body,
        grid=(M_TOK // tm,),
        in_specs=[pl.BlockSpec((tm, D_MODEL), lambda i: (i, 0)),
                  pl.BlockSpec((1, D_MODEL), lambda i: (0, 0))],
        out_specs=pl.BlockSpec((tm, D_MODEL), lambda i: (i, 0)),
        out_shape=jax.ShapeDtypeStruct((M_TOK, D_MODEL), F32),
        compiler_params=_params(("parallel",), [((tm, D_MODEL), F32)] * 2, [((tm, D_MODEL), F32)]),
        name="final_norm",
    )(x, g.reshape(1, D_MODEL))


def kernel(x, mem, t5_bias, norm_mix_g, w_in, na_rpb, ret_decay, w_branch, w_out, norm_cross_g, norm_mem_g,
           w_cq, w_ckv, w_co, norm_mlp_g, w_mlp1, w_mlp2, final_norm_g):
    xs = x.reshape(M_TOK, D_MODEL)
    mem2 = mem.reshape(BATCH * MEM_LEN, D_MODEL)
    w_in_h, w_out_h = w_in.astype(BF16), w_out.astype(BF16)
    w_up_a = w_branch[:, :NA_W].astype(BF16)
    w_up_b = w_branch[:, NA_W:NA_W + RET_V_W].astype(BF16)
    w_up_c = w_branch[:, NA_W + RET_V_W:].astype(BF16)
    w_cq_h, w_ckv_h, w_co_h = w_cq.astype(BF16), w_ckv.astype(BF16), w_co.astype(BF16)
    w_mlp1_h, w_mlp2_h = w_mlp1.astype(BF16), w_mlp2.astype(BF16)
    cos2, sin2 = _rotary_tables()
    dil_tiles = _dil_bias_tiles(t5_bias)

    for layer in range(DEPTH):
        u = _normmm(xs, norm_mix_g, w_in_h, layer, tm=1024, tn=512, name="in_proj")
        o_a = _na_attention(u, _na_bias_tiles(na_rpb[layer]))
        o_b = _retention(u, ret_decay[layer], cos2, sin2)
        o_c = _dil_attention(u, dil_tiles)
        merged = _merge(o_a, o_b, o_c, w_up_a, w_up_b, w_up_c, layer, u, tm=1024, tn=512)
        xs = _mmres(merged, w_out_h, layer, xs, tm=1024, tn=512, name="out_proj")
        kv = _normmm(mem2, norm_mem_g, w_ckv_h, layer, tm=BATCH * MEM_LEN, tn=512, name="mem_kv_proj")
        xs = _cross(xs, norm_cross_g, w_cq_h, kv, w_co_h, layer, tm=512)
        hid = _normmm(xs, norm_mlp_g, w_mlp1_h, layer, tm=1024, tn=512, act="relu2", name="mlp_up")
        xs = _mmres(hid, w_mlp2_h, layer, xs, tm=512, tn=512, name="mlp_down")
    return _final_norm(xs, final_norm_g, tm=512).reshape(BATCH, SEQ, D_MODEL)
```

```python
import numpy as np
import jax
import jax.numpy as jnp
from jax import lax
from jax.experimental import pallas as pl
from jax.experimental.pallas import tpu as pltpu

D_MODEL = 2048
BATCH = 2
SEQ = 4096
DEPTH = 4
MEM_LEN = 256
HEAD_DIM = 128
GRID_W = 64
NA_HEADS = 6
NA_ROWS = 8
NA_COLS = 16
RET_HEADS = 4
RET_DK = 128
RET_DV = 256
RET_BLK = 256
ROPE_BASE = 10000.0
DIL_HEADS = 6
DIL_PAIRS = ((128, 1), (512, 4), (2048, 16))
T5_BUCKETS = 32
T5_MAX_DIST = 1024
CROSS_HEADS = 4
D_FF = 4 * D_MODEL
EPS = 1e-6
NEG = -1e30

NA_W = NA_HEADS * HEAD_DIM
RET_QK_W = RET_HEADS * RET_DK
RET_V_W = RET_HEADS * RET_DV
DIL_W = DIL_HEADS * HEAD_DIM
CROSS_W = CROSS_HEADS * HEAD_DIM
IN_SPLITS = (NA_W, NA_W, NA_W, RET_QK_W, RET_QK_W, RET_V_W, RET_V_W,
             DIL_W, DIL_W, DIL_W, D_MODEL, D_MODEL, D_MODEL)
IN_W = sum(IN_SPLITS)
(OFF_QA, OFF_KA, OFF_VA, OFF_QB, OFF_KB, OFF_VB, OFF_GB,
 OFF_QC, OFF_KC, OFF_VC, OFF_SA, OFF_SB, OFF_SC) = [int(o) for o in np.cumsum((0,) + IN_SPLITS[:-1])]

M_TOK = BATCH * SEQ
ATT_SCALE = HEAD_DIM ** -0.5
LOG2E = float(np.log2(np.e))
QK_SCALE2 = ATT_SCALE * LOG2E

V7X_VMEM_BYTES = 64 * 1024 * 1024
LANES = 128
SUBLANES = 8

NA_QROWS = 4
NA_KROWS = NA_QROWS + NA_ROWS
NA_QBLK = NA_QROWS * GRID_W
NA_KBLK = NA_KROWS * GRID_W
GRID_H = SEQ // GRID_W
NA_NBLK = GRID_H // NA_QROWS

DIL_BLK = 256
DIL_REACH = (max(w for w, _ in DIL_PAIRS) // 2 + DIL_BLK - 1) // DIL_BLK
DIL_NDELTA = 2 * DIL_REACH + 1
DIL_NBLK = SEQ // DIL_BLK

F32 = jnp.float32
BF16 = jnp.bfloat16


def _nbytes(shape, dtype):
    return int(np.prod(shape)) * jnp.dtype(dtype).itemsize


def _params(semantics, pipelined, resident):
    need = 2 * sum(_nbytes(s, d) for s, d in pipelined) + sum(_nbytes(s, d) for s, d in resident)
    assert need < V7X_VMEM_BYTES, need
    return pltpu.CompilerParams(dimension_semantics=semantics,
                                vmem_limit_bytes=min(V7X_VMEM_BYTES, need + need // 4))


def _rms(x, g):
    ms = jnp.mean(x * x, axis=-1, keepdims=True)
    return x * lax.rsqrt(ms + EPS) * g


def _sigmoid(x):
    return 1.0 / (1.0 + jnp.exp(-x))


def _dot(a, b):
    return jnp.dot(a, b, preferred_element_type=F32)


def _dot_nt(a, b):
    return lax.dot_general(a, b, (((1,), (1,)), ((), ())), preferred_element_type=F32)


def _dot_tn(a, b):
    return lax.dot_general(a, b, (((0,), (0,)), ((), ())), preferred_element_type=F32)


def _normmm_body(x_ref, g_ref, w_ref, o_ref, xn_ref):
    @pl.when(pl.program_id(1) == 0)
    def _():
        xn_ref[...] = _rms(x_ref[...], g_ref[...]).astype(BF16)

    o_ref[...] = _dot(xn_ref[...], w_ref[...]).astype(o_ref.dtype)


def _normmm(x, g, w, layer, *, tm, tn, name):
    m, k = x.shape
    n = w.shape[-1]
    g2 = g[layer].reshape(1, k)
    return pl.pallas_call(
        _normmm_body,
        grid=(m // tm, n // tn),
        in_specs=[pl.BlockSpec((tm, k), lambda i, j: (i, 0)),
                  pl.BlockSpec((1, k), lambda i, j: (0, 0)),
                  pl.BlockSpec((None, k, tn), lambda i, j: (layer, 0, j))],
        out_specs=pl.BlockSpec((tm, tn), lambda i, j: (i, j)),
        out_shape=jax.ShapeDtypeStruct((m, n), BF16),
        scratch_shapes=[pltpu.VMEM((tm, k), BF16)],
        compiler_params=_params(
            ("parallel", "arbitrary"),
            [((tm, k), F32), ((1, k), F32), ((k, tn), BF16), ((tm, tn), BF16)],
            [((tm, k), BF16), ((tm, k), F32), ((tm, tn), F32)]),
        name=name,
    )(x, g2, w)


def _mmres_body(a_ref, w_ref, r_ref, o_ref):
    o_ref[...] = r_ref[...] + _dot(a_ref[...], w_ref[...])


def _mmres(a, w, layer, res, *, tm, tn, name):
    m, k = a.shape
    n = w.shape[-1]
    return pl.pallas_call(
        _mmres_body,
        grid=(m // tm, n // tn),
        in_specs=[pl.BlockSpec((tm, k), lambda i, j: (i, 0)),
                  pl.BlockSpec((None, k, tn), lambda i, j: (layer, 0, j)),
                  pl.BlockSpec((tm, tn), lambda i, j: (i, j))],
        out_specs=pl.BlockSpec((tm, tn), lambda i, j: (i, j)),
        out_shape=jax.ShapeDtypeStruct((m, n), F32),
        compiler_params=_params(
            ("parallel", "arbitrary"),
            [((tm, k), BF16), ((k, tn), BF16), ((tm, tn), F32), ((tm, tn), F32)],
            [((tm, tn), F32)]),
        name=name,
    )(a, w, res)


NA_RPB_ROWS = 2 * NA_ROWS - 1
NA_RPB_COLS = 2 * NA_COLS - 1
NA_TILE_ROWS = ((0, 0), (NA_QROWS, 0), (GRID_H - NA_QROWS, GRID_H - NA_KROWS))


def _na_tiles_body(rpb_ref, o_ref):
    shape = (GRID_W, LANES)
    lane = lax.broadcasted_iota(jnp.int32, shape, 1)
    cq = lax.broadcasted_iota(jnp.int32, shape, 0)
    ck = lane & (GRID_W - 1)
    c0 = jnp.clip(cq - NA_COLS // 2, 0, GRID_W - NA_COLS)
    col_ok = (ck >= c0) & (ck < c0 + NA_COLS)
    left = lane < GRID_W
    neg = jnp.full(shape, NEG, F32)
    lo, hi = [], []
    for a in range(NA_RPB_ROWS):
        row = jnp.broadcast_to(rpb_ref[a:a + 1, :], shape)
        lo.append(pltpu.roll(row, LANES - (NA_COLS - 1), 1, stride=1, stride_axis=0))
        hi.append(pltpu.roll(row, GRID_W - (NA_COLS - 1), 1, stride=1, stride_axis=0))
    for t, (qrow0, krow0) in enumerate(NA_TILE_ROWS):
        for rq in range(NA_QROWS):
            r = qrow0 + rq
            r0 = min(max(r - NA_ROWS // 2, 0), GRID_H - NA_ROWS)
            for pair in range(NA_KROWS // 2):
                halves = []
                for side, table in ((0, lo), (1, hi)):
                    rk = krow0 + 2 * pair + side
                    halves.append(table[rk - r + NA_ROWS - 1] if r0 <= rk < r0 + NA_ROWS else neg)
                blk = jnp.where(col_ok, jnp.where(left, halves[0], halves[1]) * LOG2E, NEG)
                o_ref[t, rq * GRID_W:(rq + 1) * GRID_W, pair * LANES:(pair + 1) * LANES] = blk


def _na_bias_tiles(rpb):
    padded = jnp.pad(rpb.astype(F32), ((0, 0), (0, 2 * SUBLANES - NA_RPB_ROWS), (0, LANES - NA_RPB_COLS)))
    return pl.pallas_call(
        _na_tiles_body,
        grid=(NA_HEADS,),
        in_specs=[pl.BlockSpec((None, 2 * SUBLANES, LANES), lambda h: (h, 0, 0))],
        out_specs=pl.BlockSpec((None, 3, NA_QBLK, NA_KBLK), lambda h: (h, 0, 0, 0)),
        out_shape=jax.ShapeDtypeStruct((NA_HEADS, 3, NA_QBLK, NA_KBLK), F32),
        compiler_params=_params(("parallel",), [((3, NA_QBLK, NA_KBLK), F32)], [((NA_QBLK, NA_KBLK), F32)]),
        name="na_bias_tiles",
    )(padded)


def _na_body(q_ref, k_ref, v_ref, bias_ref, o_ref):
    def blk(i, carry):
        qrow = i * NA_QROWS
        krow = jnp.clip(qrow - NA_ROWS // 2, 0, GRID_H - NA_KROWS)
        qs = pl.multiple_of(qrow * GRID_W, NA_QBLK)
        ks = pl.multiple_of(krow * GRID_W, NA_QBLK)
        tile = jnp.where(i == 0, 0, jnp.where(i == NA_NBLK - 1, 2, 1))
        q = q_ref[pl.ds(qs, NA_QBLK), :]
        k = k_ref[pl.ds(ks, NA_KBLK), :]
        v = v_ref[pl.ds(ks, NA_KBLK), :]
        s = _dot_nt(q, k) * QK_SCALE2 + bias_ref[tile]
        m = jnp.max(s, axis=-1, keepdims=True)
        p = jnp.exp2(s - m)
        l = jnp.sum(p, axis=-1, keepdims=True)
        o = _dot(p.astype(BF16), v) * (1.0 / l)
        o_ref[pl.ds(qs, NA_QBLK), :] = o.astype(o_ref.dtype)
        return carry

    lax.fori_loop(0, NA_NBLK, blk, 0, unroll=2)


def _na_attention(u, bias):
    qb, kb, vb = OFF_QA // HEAD_DIM, OFF_KA // HEAD_DIM, OFF_VA // HEAD_DIM
    blk = (SEQ, HEAD_DIM)
    return pl.pallas_call(
        _na_body,
        grid=(NA_HEADS, BATCH),
        in_specs=[pl.BlockSpec(blk, lambda h, b: (b, qb + h)),
                  pl.BlockSpec(blk, lambda h, b: (b, kb + h)),
                  pl.BlockSpec(blk, lambda h, b: (b, vb + h)),
                  pl.BlockSpec((None, 3, NA_QBLK, NA_KBLK), lambda h, b: (h, 0, 0, 0))],
        out_specs=pl.BlockSpec(blk, lambda h, b: (b, h)),
        out_shape=jax.ShapeDtypeStruct((M_TOK, NA_W), BF16),
        compiler_params=_params(
            ("parallel", "parallel"),
            [(blk, BF16)] * 4 + [((3, NA_QBLK, NA_KBLK), F32)],
            [((NA_QBLK, NA_KBLK), F32)] * 3),
        name="na_attention",
    )(u, u, u, bias)


def _t5_bucket(rel):
    nb = T5_BUCKETS // 2
    ret = (rel > 0).astype(np.int32) * nb
    n = np.abs(rel)
    max_exact = nb // 2
    large = max_exact + (np.log(np.maximum(n, 1) / max_exact) / np.log(T5_MAX_DIST / max_exact)
                         * (nb - max_exact)).astype(np.int32)
    large = np.minimum(large, nb - 1)
    return (ret + np.where(n < max_exact, n, large)).astype(np.int32)


def _dil_tiles_body(f_ref, o_ref):
    for d in range(DIL_NDELTA):
        row = jnp.broadcast_to(f_ref[d:d + 1, :], (DIL_BLK, 2 * DIL_BLK))
        o_ref[d] = pltpu.roll(row, DIL_BLK, 1, stride=1, stride_axis=0)[:, :DIL_BLK]
    o_ref[DIL_NDELTA] = jnp.full((DIL_BLK, DIL_BLK), NEG, F32)


def _dil_bias_tiles(t5_bias):
    delta = np.arange(-DIL_REACH, DIL_REACH + 1)[:, None]
    off = delta * DIL_BLK + np.arange(-DIL_BLK, DIL_BLK)[None, :]
    count = np.zeros(off.shape, np.int32)
    for w, d in DIL_PAIRS:
        count += ((off % d == 0) & (np.abs(off) <= w // 2)).astype(np.int32)
    bucket = _t5_bucket(np.clip(off, -T5_MAX_DIST, T5_MAX_DIST))
    bias = jnp.take(t5_bias.T.astype(F32), jnp.asarray(bucket.reshape(-1)), axis=1).reshape((DIL_HEADS,) + off.shape)
    logc = jnp.log(jnp.asarray(np.maximum(count, 1), F32))
    f = jnp.where(jnp.asarray(count > 0)[None], (bias + logc[None]) * LOG2E, NEG)
    f = jnp.pad(f, ((0, 0), (0, 2 * SUBLANES - DIL_NDELTA), (0, 0)))
    return pl.pallas_call(
        _dil_tiles_body,
        grid=(DIL_HEADS,),
        in_specs=[pl.BlockSpec((None, 2 * SUBLANES, 2 * DIL_BLK), lambda h: (h, 0, 0))],
        out_specs=pl.BlockSpec((None, DIL_NDELTA + 1, DIL_BLK, DIL_BLK), lambda h: (h, 0, 0, 0)),
        out_shape=jax.ShapeDtypeStruct((DIL_HEADS, DIL_NDELTA + 1, DIL_BLK, DIL_BLK), F32),
        compiler_params=_params(("parallel",), [((DIL_NDELTA + 1, DIL_BLK, DIL_BLK), F32)],
                                [((DIL_BLK, 2 * DIL_BLK), F32)] * 2),
        name="dil_bias_tiles",
    )(f)


def _dil_body(q_ref, k_ref, v_ref, t_ref, o_ref, s_ref):
    def blk(i, carry):
        qs = pl.multiple_of(i * DIL_BLK, DIL_BLK)
        q = q_ref[pl.ds(qs, DIL_BLK), :]
        starts = []
        for d in range(DIL_NDELTA):
            j = i + (d - DIL_REACH)
            inside = (j >= 0) & (j < DIL_NBLK)
            ks = pl.multiple_of(jnp.clip(j, 0, DIL_NBLK - 1) * DIL_BLK, DIL_BLK)
            starts.append(ks)
            tile = jnp.where(inside, d, DIL_NDELTA)
            s_ref[:, d * DIL_BLK:(d + 1) * DIL_BLK] = (
                _dot_nt(q, k_ref[pl.ds(ks, DIL_BLK), :]) * QK_SCALE2 + t_ref[tile])
        m = jnp.max(s_ref[...], axis=-1, keepdims=True)
        l = jnp.zeros((DIL_BLK, 1), F32)
        acc = jnp.zeros((DIL_BLK, HEAD_DIM), F32)
        for d in range(DIL_NDELTA):
            p = jnp.exp2(s_ref[:, d * DIL_BLK:(d + 1) * DIL_BLK] - m)
            l = l + jnp.sum(p, axis=-1, keepdims=True)
            acc = acc + _dot(p.astype(BF16), v_ref[pl.ds(starts[d], DIL_BLK), :])
        o_ref[pl.ds(qs, DIL_BLK), :] = (acc * (1.0 / l)).astype(o_ref.dtype)
        return carry

    lax.fori_loop(0, DIL_NBLK, blk, 0)


def _dil_attention(u, tiles):
    qb, kb, vb = OFF_QC // HEAD_DIM, OFF_KC // HEAD_DIM, OFF_VC // HEAD_DIM
    blk = (SEQ, HEAD_DIM)
    return pl.pallas_call(
        _dil_body,
        grid=(DIL_HEADS, BATCH),
        in_specs=[pl.BlockSpec(blk, lambda h, b: (b, qb + h)),
                  pl.BlockSpec(blk, lambda h, b: (b, kb + h)),
                  pl.BlockSpec(blk, lambda h, b: (b, vb + h)),
                  pl.BlockSpec((None, DIL_NDELTA + 1, DIL_BLK, DIL_BLK), lambda h, b: (h, 0, 0, 0))],
        out_specs=pl.BlockSpec(blk, lambda h, b: (b, h)),
        out_shape=jax.ShapeDtypeStruct((M_TOK, DIL_W), BF16),
        scratch_shapes=[pltpu.VMEM((DIL_BLK, DIL_NDELTA * DIL_BLK), F32)],
        compiler_params=_params(
            ("parallel", "parallel"),
            [(blk, BF16)] * 4 + [((DIL_NDELTA + 1, DIL_BLK, DIL_BLK), F32)],
            [((DIL_BLK, DIL_NDELTA * DIL_BLK), F32)] * 2),
        name="dil_attention",
    )(u, u, u, tiles)


def _rotary_tables():
    inv_freq = jnp.asarray((ROPE_BASE ** (-np.arange(0, RET_DK, 2, dtype=np.float32) / RET_DK)).astype(np.float32))
    ang = jnp.arange(SEQ, dtype=F32)[:, None] * inv_freq[None, :]
    cos, sin = jnp.cos(ang), jnp.sin(ang)
    return jnp.concatenate([cos, cos], axis=-1), jnp.concatenate([-sin, sin], axis=-1)


def _ret_tables(dec_ref, h, d_ref, xz_ref, gc_ref):
    c = RET_BLK
    row = lax.broadcasted_iota(jnp.int32, (c, c), 0)
    col = lax.broadcasted_iota(jnp.int32, (c, c), 1)
    diff = (row - col).astype(F32)
    log_f = -jnp.exp(jnp.full((c, c), dec_ref[0, h], F32))
    log_b = -jnp.exp(jnp.full((c, c), dec_ref[1, h], F32))
    d_ref[...] = jnp.where(diff >= 0.0, jnp.exp(jnp.maximum(diff, 0.0) * log_f),
                           jnp.exp(jnp.maximum(-diff, 0.0) * log_b))
    j = lax.broadcasted_iota(jnp.int32, (c, RET_DV), 0).astype(F32)
    log_f2 = -jnp.exp(jnp.full((c, RET_DV), dec_ref[0, h], F32))
    log_b2 = -jnp.exp(jnp.full((c, RET_DV), dec_ref[1, h], F32))
    xz_ref[0] = jnp.exp((j + 1.0) * log_f2)
    xz_ref[1] = jnp.exp((c - 1.0 - j) * log_f2)
    xz_ref[2] = jnp.exp((c - j) * log_b2)
    xz_ref[3] = jnp.exp(j * log_b2)
    gc_ref[0] = jnp.exp(c * -jnp.exp(jnp.full((RET_DK, RET_DV), dec_ref[0, h], F32)))
    gc_ref[1] = jnp.exp(c * -jnp.exp(jnp.full((RET_DK, RET_DV), dec_ref[1, h], F32)))


def _ret_body(dec_ref, q_ref, k_ref, v_ref, gate_ref, cos_ref, sin_ref, o_ref,
              qr_ref, kr_ref, y_ref, d_ref, xz_ref, gc_ref, sf_ref, sb_ref):
    c = RET_BLK
    nblk = SEQ // c
    _ret_tables(dec_ref, pl.program_id(0), d_ref, xz_ref, gc_ref)
    sf_ref[...] = jnp.zeros_like(sf_ref)
    sb_ref[...] = jnp.zeros_like(sb_ref)

    def rotate(x, cs, sn):
        return x * cs + pltpu.roll(x, RET_DK // 2, 1) * sn

    def up(i, carry):
        sl = pl.ds(pl.multiple_of(i * c, c), c)
        cs, sn = cos_ref[sl, :], sin_ref[sl, :]
        q = rotate(q_ref[sl, :].astype(F32), cs, sn).astype(BF16)
        k = (rotate(k_ref[sl, :].astype(F32), cs, sn) * (RET_DK ** -0.5)).astype(BF16)
        qr_ref[sl, :] = q
        kr_ref[sl, :] = k
        v = v_ref[sl, :]
        inner = (_dot_nt(q, k) * d_ref[...]).astype(BF16)
        state = sf_ref[...]
        y_ref[sl, :] = _dot(inner, v) + _dot(q, state.astype(BF16)) * xz_ref[0]
        sf_ref[...] = state * gc_ref[0] + _dot_tn(k, (v.astype(F32) * xz_ref[1]).astype(BF16))
        return carry

    lax.fori_loop(0, nblk, up, 0, unroll=2)

    def down(ii, carry):
        sl = pl.ds(pl.multiple_of((nblk - 1 - ii) * c, c), c)
        q, k, v = qr_ref[sl, :], kr_ref[sl, :], v_ref[sl, :]
        state = sb_ref[...]
        y = y_ref[sl, :] + _dot(q, state.astype(BF16)) * xz_ref[2]
        sb_ref[...] = state * gc_ref[1] + _dot_tn(k, (v.astype(F32) * xz_ref[3]).astype(BF16))
        mu = jnp.mean(y, axis=-1, keepdims=True)
        yc = y - mu
        var = jnp.mean(yc * yc, axis=-1, keepdims=True)
        gate = gate_ref[sl, :].astype(F32)
        o_ref[sl, :] = (gate * _sigmoid(gate) * (yc * lax.rsqrt(var + EPS))).astype(o_ref.dtype)
        return carry

    lax.fori_loop(0, nblk, down, 0, unroll=2)


def _retention(u, decay, cos2, sin2):
    qb, kb = OFF_QB // RET_DK, OFF_KB // RET_DK
    vb, gb = OFF_VB // RET_DV, OFF_GB // RET_DV
    qk_blk, v_blk = (SEQ, RET_DK), (SEQ, RET_DV)
    scratch = [(qk_blk, BF16), (qk_blk, BF16), (v_blk, F32), ((RET_BLK, RET_BLK), F32),
               ((4, RET_BLK, RET_DV), F32), ((2, RET_DK, RET_DV), F32),
               ((RET_DK, RET_DV), F32), ((RET_DK, RET_DV), F32)]
    return pl.pallas_call(
        _ret_body,
        grid=(RET_HEADS, BATCH),
        in_specs=[pl.BlockSpec(memory_space=pltpu.SMEM),
                  pl.BlockSpec(qk_blk, lambda h, b: (b, qb + h)),
                  pl.BlockSpec(qk_blk, lambda h, b: (b, kb + h)),
                  pl.BlockSpec(v_blk, lambda h, b: (b, vb + h)),
                  pl.BlockSpec(v_blk, lambda h, b: (b, gb + h)),
                  pl.BlockSpec(qk_blk, lambda h, b: (0, 0)),
                  pl.BlockSpec(qk_blk, lambda h, b: (0, 0))],
        out_specs=pl.BlockSpec(v_blk, lambda h, b: (b, h)),
        out_shape=jax.ShapeDtypeStruct((M_TOK, RET_V_W), BF16),
        scratch_shapes=[pltpu.VMEM(s, d) for s, d in scratch],
        compiler_params=_params(
            ("parallel", "parallel"),
            [(qk_blk, BF16)] * 2 + [(v_blk, BF16)] * 3 + [(qk_blk, F32)] * 2,
            scratch + [((RET_BLK, RET_DV), F32)] * 4),
        name="retention",
    )(decay, u, u, u, u, cos2, sin2)


def _merge_body(oa_ref, ob_ref, oc_ref, wa_ref, wb_ref, wc_ref, sa_ref, sb_ref, sc_ref, o_ref):
    def branch(o, w, s):
        return _sigmoid(s[...].astype(F32)) * _dot(o[...], w[...])

    o_ref[...] = (branch(oa_ref, wa_ref, sa_ref) + branch(ob_ref, wb_ref, sb_ref)
                  + branch(oc_ref, wc_ref, sc_ref)).astype(o_ref.dtype)


def _merge(o_a, o_b, o_c, w_a, w_b, w_c, layer, u, *, tm, tn):
    sa, sb, sc = OFF_SA // tn, OFF_SB // tn, OFF_SC // tn

    def wspec(k):
        return pl.BlockSpec((None, k, tn), lambda i, j: (layer, 0, j))

    def ospec(k):
        return pl.BlockSpec((tm, k), lambda i, j: (i, 0))

    return pl.pallas_call(
        _merge_body,
        grid=(M_TOK // tm, D_MODEL // tn),
        in_specs=[ospec(NA_W), ospec(RET_V_W), ospec(DIL_W), wspec(NA_W), wspec(RET_V_W), wspec(DIL_W),
                  pl.BlockSpec((tm, tn), lambda i, j: (i, sa + j)),
                  pl.BlockSpec((tm, tn), lambda i, j: (i, sb + j)),
                  pl.BlockSpec((tm, tn), lambda i, j: (i, sc + j))],
        out_specs=pl.BlockSpec((tm, tn), lambda i, j: (i, j)),
        out_shape=jax.ShapeDtypeStruct((M_TOK, D_MODEL), BF16),
        compiler_params=_params(
            ("parallel", "arbitrary"),
            [((tm, NA_W + RET_V_W + DIL_W), BF16), ((NA_W + RET_V_W + DIL_W, tn), BF16),
             ((tm, tn), BF16)] + [((tm, tn), BF16)] * 3,
            [((tm, tn), F32)] * 4),
        name="branch_merge",
    )(o_a, o_b, o_c, w_a, w_b, w_c, u, u, u)


def _cross_body(x_ref, g_ref, wq_ref, kv_ref, wo_ref, o_ref):
    x = x_ref[...]
    q = _dot(_rms(x, g_ref[...]).astype(BF16), wq_ref[...]).astype(BF16)
    heads = []
    for h in range(CROSS_HEADS):
        lo = h * HEAD_DIM
        k = kv_ref[:, lo:lo + HEAD_DIM]
        v = kv_ref[:, CROSS_W + lo:CROSS_W + lo + HEAD_DIM]
        s = _dot_nt(q[:, lo:lo + HEAD_DIM], k) * QK_SCALE2
        m = jnp.max(s, axis=-1, keepdims=True)
        p = jnp.exp2(s - m)
        l = jnp.sum(p, axis=-1, keepdims=True)
        heads.append((_dot(p.astype(BF16), v) * (1.0 / l)).astype(BF16))
    o_ref[...] = x + _dot(jnp.concatenate(heads, axis=-1), wo_ref[...])


def _cross(x, g, w_q, kv, w_o, layer, *, tm):
    per_batch = SEQ // tm
    g2 = g[layer].reshape(1, D_MODEL)
    return pl.pallas_call(
        _cross_body,
        grid=(M_TOK // tm,),
        in_specs=[pl.BlockSpec((tm, D_MODEL), lambda i: (i, 0)),
                  pl.BlockSpec((1, D_MODEL), lambda i: (0, 0)),
                  pl.BlockSpec((None, D_MODEL, CROSS_W), lambda i: (layer, 0, 0)),
                  pl.BlockSpec((MEM_LEN, 2 * CROSS_W), lambda i: (i // per_batch, 0)),
                  pl.BlockSpec((None, CROSS_W, D_MODEL), lambda i: (layer, 0, 0))],
        out_specs=pl.BlockSpec((tm, D_MODEL), lambda i: (i, 0)),
        out_shape=jax.ShapeDtypeStruct((M_TOK, D_MODEL), F32),
        compiler_params=_params(
            ("parallel",),
            [((tm, D_MODEL), F32)] * 2 + [((D_MODEL, CROSS_W), BF16), ((MEM_LEN, 2 * CROSS_W), BF16),
                                          ((CROSS_W, D_MODEL), BF16)],
            [((tm, D_MODEL), F32)] * 2 + [((tm, CROSS_W), F32)] * 2 + [((tm, MEM_LEN), F32)] * 2),
        name="cross_attention",
    )(x, g2, w_q, kv, w_o)


def _mlp_body(x_ref, g_ref, w1_ref, w2_ref, o_ref, xn_ref):
    @pl.when(pl.program_id(1) == 0)
    def _():
        x = x_ref[...]
        xn_ref[...] = _rms(x, g_ref[...]).astype(BF16)
        o_ref[...] = x

    hid = jnp.square(jnp.maximum(_dot(xn_ref[...], w1_ref[...]), 0.0)).astype(BF16)
    o_ref[...] += _dot(hid, w2_ref[...])


def _mlp(x, g, w1, w2, layer, *, tm, tf):
    m, k = x.shape
    return pl.pallas_call(
        _mlp_body,
        grid=(m // tm, D_FF // tf),
        in_specs=[pl.BlockSpec((tm, k), lambda i, f: (i, 0)),
                  pl.BlockSpec((1, k), lambda i, f: (0, 0)),
                  pl.BlockSpec((None, k, tf), lambda i, f: (layer, 0, f)),
                  pl.BlockSpec((None, tf, k), lambda i, f: (layer, f, 0))],
        out_specs=pl.BlockSpec((tm, k), lambda i, f: (i, 0)),
        out_shape=jax.ShapeDtypeStruct((m, k), F32),
        scratch_shapes=[pltpu.VMEM((tm, k), BF16)],
        compiler_params=_params(
            ("parallel", "arbitrary"),
            [((tm, k), F32), ((1, k), F32), ((k, tf), BF16), ((tf, k), BF16), ((tm, k), F32)],
            [((tm, k), BF16), ((tm, tf), F32), ((tm, tf), BF16), ((tm, k), F32)]),
        name="mlp",
    )(x, g[layer].reshape(1, k), w1, w2)


def _final_norm_body(x_ref, g_ref, o_ref):
    o_ref[...] = _rms(x_ref[...], g_ref[...])


def _final_norm(x, g, *, tm):
    return pl.pallas_call(
        _final_norm_body,
        grid=(M_TOK // tm,),
        in_specs=[pl.BlockSpec((tm, D_MODEL), lambda i: (i, 0)),
                  pl.BlockSpec((1, D_MODEL), lambda i: (0, 0))],
        out_specs=pl.BlockSpec((tm, D_MODEL), lambda i: (i, 0)),
        out_shape=jax.ShapeDtypeStruct((M_TOK, D_MODEL), F32),
        compiler_params=_params(("parallel",), [((tm, D_MODEL), F32)] * 2, [((tm, D_MODEL), F32)]),
        name="final_norm",
    )(x, g.reshape(1, D_MODEL))


def kernel(x, mem, t5_bias, norm_mix_g, w_in, na_rpb, ret_decay, w_branch, w_out, norm_cross_g, norm_mem_g,
           w_cq, w_ckv, w_co, norm_mlp_g, w_mlp1, w_mlp2, final_norm_g):
    xs = x.reshape(M_TOK, D_MODEL)
    mem2 = mem.reshape(BATCH * MEM_LEN, D_MODEL)
    w_in_h, w_out_h = w_in.astype(BF16), w_out.astype(BF16)
    w_up_a = w_branch[:, :NA_W].astype(BF16)
    w_up_b = w_branch[:, NA_W:NA_W + RET_V_W].astype(BF16)
    w_up_c = w_branch[:, NA_W + RET_V_W:].astype(BF16)
    w_cq_h, w_ckv_h, w_co_h = w_cq.astype(BF16), w_ckv.astype(BF16), w_co.astype(BF16)
    w_mlp1_h, w_mlp2_h = w_mlp1.astype(BF16), w_mlp2.astype(BF16)
    cos2, sin2 = _rotary_tables()
    dil_tiles = _dil_bias_tiles(t5_bias)

    for layer in range(DEPTH):
        u = _normmm(xs, norm_mix_g, w_in_h, layer, tm=1024, tn=1536, name="in_proj")
        o_a = _na_attention(u, _na_bias_tiles(na_rpb[layer]))
        o_b = _retention(u, ret_decay[layer], cos2, sin2)
        o_c = _dil_attention(u, dil_tiles)
        merged = _merge(o_a, o_b, o_c, w_up_a, w_up_b, w_up_c, layer, u, tm=1024, tn=512)
        xs = _mmres(merged, w_out_h, layer, xs, tm=1024, tn=1024, name="out_proj")
        kv = _normmm(mem2, norm_mem_g, w_ckv_h, layer, tm=BATCH * MEM_LEN, tn=512, name="mem_kv_proj")
        xs = _cross(xs, norm_cross_g, w_cq_h, kv, w_co_h, layer, tm=512)
        xs = _mlp(xs, norm_mlp_g, w_mlp1_h, w_mlp2_h, layer, tm=512, tf=1024)
    return _final_norm(xs, final_norm_g, tm=512).reshape(BATCH, SEQ, D_MODEL)
```

```python
import numpy as np
import jax
import jax.numpy as jnp
from jax import lax
from jax.experimental import pallas as pl
from jax.experimental.pallas import tpu as pltpu

D_MODEL = 2048
BATCH = 2
SEQ = 4096
DEPTH = 4
MEM_LEN = 256
HEAD_DIM = 128
GRID_W = 64
NA_HEADS = 6
NA_ROWS = 8
NA_COLS = 16
RET_HEADS = 4
RET_DK = 128
RET_DV = 256
RET_BLK = 256
ROPE_BASE = 10000.0
DIL_HEADS = 6
DIL_PAIRS = ((128, 1), (512, 4), (2048, 16))
T5_BUCKETS = 32
T5_MAX_DIST = 1024
CROSS_HEADS = 4
D_FF = 4 * D_MODEL
EPS = 1e-6
NEG = -1e30

NA_W = NA_HEADS * HEAD_DIM
RET_QK_W = RET_HEADS * RET_DK
RET_V_W = RET_HEADS * RET_DV
DIL_W = DIL_HEADS * HEAD_DIM
CROSS_W = CROSS_HEADS * HEAD_DIM
IN_SPLITS = (NA_W, NA_W, NA_W, RET_QK_W, RET_QK_W, RET_V_W, RET_V_W,
             DIL_W, DIL_W, DIL_W, D_MODEL, D_MODEL, D_MODEL)
IN_W = sum(IN_SPLITS)
(OFF_QA, OFF_KA, OFF_VA, OFF_QB, OFF_KB, OFF_VB, OFF_GB,
 OFF_QC, OFF_KC, OFF_VC, OFF_SA, OFF_SB, OFF_SC) = [int(o) for o in np.cumsum((0,) + IN_SPLITS[:-1])]

M_TOK = BATCH * SEQ
ATT_SCALE = HEAD_DIM ** -0.5
LOG2E = float(np.log2(np.e))
QK_SCALE2 = ATT_SCALE * LOG2E

V7X_VMEM_BYTES = 64 * 1024 * 1024
LANES = 128
SUBLANES = 8

NA_QROWS = 4
NA_KROWS = NA_QROWS + NA_ROWS
NA_QBLK = NA_QROWS * GRID_W
NA_KBLK = NA_KROWS * GRID_W
GRID_H = SEQ // GRID_W
NA_NBLK = GRID_H // NA_QROWS

DIL_BLK = 256
DIL_REACH = (max(w for w, _ in DIL_PAIRS) // 2 + DIL_BLK - 1) // DIL_BLK
DIL_NDELTA = 2 * DIL_REACH + 1
DIL_NBLK = SEQ // DIL_BLK

F32 = jnp.float32
BF16 = jnp.bfloat16


def _nbytes(shape, dtype):
    return int(np.prod(shape)) * jnp.dtype(dtype).itemsize


def _params(semantics, pipelined, resident):
    need = 2 * sum(_nbytes(s, d) for s, d in pipelined) + sum(_nbytes(s, d) for s, d in resident)
    assert need < V7X_VMEM_BYTES, need
    return pltpu.CompilerParams(dimension_semantics=semantics,
                                vmem_limit_bytes=min(V7X_VMEM_BYTES, need + need // 4))


def _rms(x, g):
    ms = jnp.mean(x * x, axis=-1, keepdims=True)
    return x * lax.rsqrt(ms + EPS) * g


def _sigmoid(x):
    return 1.0 / (1.0 + jnp.exp(-x))


def _dot(a, b):
    return jnp.dot(a, b, preferred_element_type=F32)


def _dot_nt(a, b):
    return lax.dot_general(a, b, (((1,), (1,)), ((), ())), preferred_element_type=F32)


def _dot_tn(a, b):
    return lax.dot_general(a, b, (((0,), (0,)), ((), ())), preferred_element_type=F32)


def _normmm_body(x_ref, g_ref, w_ref, o_ref, xn_ref):
    @pl.when(pl.program_id(1) == 0)
    def _():
        xn_ref[...] = _rms(x_ref[...], g_ref[...]).astype(BF16)

    o_ref[...] = _dot(xn_ref[...], w_ref[...].astype(BF16)).astype(o_ref.dtype)


def _normmm(x, g, w, layer, *, tm, tn, name):
    m, k = x.shape
    n = w.shape[-1]
    g2 = g[layer].reshape(1, k)
    return pl.pallas_call(
        _normmm_body,
        grid=(m // tm, n // tn),
        in_specs=[pl.BlockSpec((tm, k), lambda i, j: (i, 0)),
                  pl.BlockSpec((1, k), lambda i, j: (0, 0)),
                  pl.BlockSpec((None, k, tn), lambda i, j: (layer, 0, j))],
        out_specs=pl.BlockSpec((tm, tn), lambda i, j: (i, j)),
        out_shape=jax.ShapeDtypeStruct((m, n), BF16),
        scratch_shapes=[pltpu.VMEM((tm, k), BF16)],
        compiler_params=_params(
            ("parallel", "arbitrary"),
            [((tm, k), F32), ((1, k), F32), ((k, tn), F32), ((tm, tn), BF16)],
            [((tm, k), BF16), ((tm, k), F32), ((tm, tn), F32), ((k, tn), BF16)]),
        name=name,
    )(x, g2, w)


def _mmres_body(a_ref, w_ref, r_ref, o_ref):
    o_ref[...] = r_ref[...] + _dot(a_ref[...], w_ref[...].astype(BF16))


def _mmres(a, w, layer, res, *, tm, tn, name):
    m, k = a.shape
    n = w.shape[-1]
    return pl.pallas_call(
        _mmres_body,
        grid=(m // tm, n // tn),
        in_specs=[pl.BlockSpec((tm, k), lambda i, j: (i, 0)),
                  pl.BlockSpec((None, k, tn), lambda i, j: (layer, 0, j)),
                  pl.BlockSpec((tm, tn), lambda i, j: (i, j))],
        out_specs=pl.BlockSpec((tm, tn), lambda i, j: (i, j)),
        out_shape=jax.ShapeDtypeStruct((m, n), F32),
        compiler_params=_params(
            ("parallel", "arbitrary"),
            [((tm, k), BF16), ((k, tn), F32), ((tm, tn), F32), ((tm, tn), F32)],
            [((tm, tn), F32), ((k, tn), BF16)]),
        name=name,
    )(a, w, res)


NA_RPB_ROWS = 2 * NA_ROWS - 1
NA_RPB_COLS = 2 * NA_COLS - 1
NA_TILE_ROWS = ((0, 0), (NA_QROWS, 0), (GRID_H - NA_QROWS, GRID_H - NA_KROWS))


def _na_tiles_body(rpb_ref, o_ref):
    shape = (GRID_W, LANES)
    lane = lax.broadcasted_iota(jnp.int32, shape, 1)
    cq = lax.broadcasted_iota(jnp.int32, shape, 0)
    ck = lane & (GRID_W - 1)
    c0 = jnp.clip(cq - NA_COLS // 2, 0, GRID_W - NA_COLS)
    col_ok = (ck >= c0) & (ck < c0 + NA_COLS)
    left = lane < GRID_W
    neg = jnp.full(shape, NEG, F32)
    lo, hi = [], []
    for a in range(NA_RPB_ROWS):
        row = jnp.broadcast_to(rpb_ref[a:a + 1, :], shape)
        lo.append(pltpu.roll(row, LANES - (NA_COLS - 1), 1, stride=1, stride_axis=0))
        hi.append(pltpu.roll(row, GRID_W - (NA_COLS - 1), 1, stride=1, stride_axis=0))
    for t, (qrow0, krow0) in enumerate(NA_TILE_ROWS):
        for rq in range(NA_QROWS):
            r = qrow0 + rq
            r0 = min(max(r - NA_ROWS // 2, 0), GRID_H - NA_ROWS)
            for pair in range(NA_KROWS // 2):
                halves = []
                for side, table in ((0, lo), (1, hi)):
                    rk = krow0 + 2 * pair + side
                    halves.append(table[rk - r + NA_ROWS - 1] if r0 <= rk < r0 + NA_ROWS else neg)
                blk = jnp.where(col_ok, jnp.where(left, halves[0], halves[1]) * LOG2E, NEG)
                o_ref[t, rq * GRID_W:(rq + 1) * GRID_W, pair * LANES:(pair + 1) * LANES] = blk


def _na_bias_tiles(rpb):
    padded = jnp.pad(rpb.astype(F32), ((0, 0), (0, 2 * SUBLANES - NA_RPB_ROWS), (0, LANES - NA_RPB_COLS)))
    return pl.pallas_call(
        _na_tiles_body,
        grid=(NA_HEADS,),
        in_specs=[pl.BlockSpec((None, 2 * SUBLANES, LANES), lambda h: (h, 0, 0))],
        out_specs=pl.BlockSpec((None, 3, NA_QBLK, NA_KBLK), lambda h: (h, 0, 0, 0)),
        out_shape=jax.ShapeDtypeStruct((NA_HEADS, 3, NA_QBLK, NA_KBLK), F32),
        compiler_params=_params(("parallel",), [((3, NA_QBLK, NA_KBLK), F32)], [((NA_QBLK, NA_KBLK), F32)]),
        name="na_bias_tiles",
    )(padded)


def _na_body(q_ref, k_ref, v_ref, bias_ref, o_ref, s_ref):
    def starts(i):
        qrow = i * NA_QROWS
        krow = jnp.clip(qrow - NA_ROWS // 2, 0, GRID_H - NA_KROWS)
        return pl.multiple_of(qrow * GRID_W, NA_QBLK), pl.multiple_of(krow * GRID_W, NA_QBLK)

    def logits(i, buf):
        qs, ks = starts(i)
        tile = jnp.where(i == 0, 0, jnp.where(i == NA_NBLK - 1, 2, 1))
        s_ref[buf] = (_dot_nt(q_ref[pl.ds(qs, NA_QBLK), :], k_ref[pl.ds(ks, NA_KBLK), :]) * QK_SCALE2
                      + bias_ref[tile])

    def finish(i, buf):
        qs, ks = starts(i)
        s = s_ref[buf]
        m = jnp.max(s, axis=-1, keepdims=True)
        p = jnp.exp2(s - m)
        l = jnp.sum(p, axis=-1, keepdims=True)
        o = _dot(p.astype(BF16), v_ref[pl.ds(ks, NA_KBLK), :]) * (1.0 / l)
        o_ref[pl.ds(qs, NA_QBLK), :] = o.astype(o_ref.dtype)

    logits(0, 0)

    def pair(ii, carry):
        i = 2 * ii
        logits(i + 1, 1)
        finish(i, 0)
        logits(jnp.minimum(i + 2, NA_NBLK - 1), 0)
        finish(i + 1, 1)
        return carry

    lax.fori_loop(0, NA_NBLK // 2, pair, 0)


def _na_attention(u, bias):
    qb, kb, vb = OFF_QA // HEAD_DIM, OFF_KA // HEAD_DIM, OFF_VA // HEAD_DIM
    blk = (SEQ, HEAD_DIM)
    return pl.pallas_call(
        _na_body,
        grid=(NA_HEADS, BATCH),
        in_specs=[pl.BlockSpec(blk, lambda h, b: (b, qb + h)),
                  pl.BlockSpec(blk, lambda h, b: (b, kb + h)),
                  pl.BlockSpec(blk, lambda h, b: (b, vb + h)),
                  pl.BlockSpec((None, 3, NA_QBLK, NA_KBLK), lambda h, b: (h, 0, 0, 0))],
        out_specs=pl.BlockSpec(blk, lambda h, b: (b, h)),
        out_shape=jax.ShapeDtypeStruct((M_TOK, NA_W), BF16),
        scratch_shapes=[pltpu.VMEM((2, NA_QBLK, NA_KBLK), F32)],
        compiler_params=_params(
            ("parallel", "parallel"),
            [(blk, BF16)] * 4 + [((3, NA_QBLK, NA_KBLK), F32)],
            [((NA_QBLK, NA_KBLK), F32)] * 3),
        name="na_attention",
    )(u, u, u, bias)


def _t5_bucket(rel):
    nb = T5_BUCKETS // 2
    ret = (rel > 0).astype(np.int32) * nb
    n = np.abs(rel)
    max_exact = nb // 2
    large = max_exact + (np.log(np.maximum(n, 1) / max_exact) / np.log(T5_MAX_DIST / max_exact)
                         * (nb - max_exact)).astype(np.int32)
    large = np.minimum(large, nb - 1)
    return (ret + np.where(n < max_exact, n, large)).astype(np.int32)


def _dil_tiles_body(f_ref, o_ref):
    for d in range(DIL_NDELTA):
        row = jnp.broadcast_to(f_ref[d:d + 1, :], (DIL_BLK, 2 * DIL_BLK))
        o_ref[d] = pltpu.roll(row, DIL_BLK, 1, stride=1, stride_axis=0)[:, :DIL_BLK]
    o_ref[DIL_NDELTA] = jnp.full((DIL_BLK, DIL_BLK), NEG, F32)


def _dil_bias_tiles(t5_bias):
    delta = np.arange(-DIL_REACH, DIL_REACH + 1)[:, None]
    off = delta * DIL_BLK + np.arange(-DIL_BLK, DIL_BLK)[None, :]
    count = np.zeros(off.shape, np.int32)
    for w, d in DIL_PAIRS:
        count += ((off % d == 0) & (np.abs(off) <= w // 2)).astype(np.int32)
    bucket = _t5_bucket(np.clip(off, -T5_MAX_DIST, T5_MAX_DIST))
    bias = jnp.take(t5_bias.T.astype(F32), jnp.asarray(bucket.reshape(-1)), axis=1).reshape((DIL_HEADS,) + off.shape)
    logc = jnp.log(jnp.asarray(np.maximum(count, 1), F32))
    f = jnp.where(jnp.asarray(count > 0)[None], (bias + logc[None]) * LOG2E, NEG)
    f = jnp.pad(f, ((0, 0), (0, 2 * SUBLANES - DIL_NDELTA), (0, 0)))
    return pl.pallas_call(
        _dil_tiles_body,
        grid=(DIL_HEADS,),
        in_specs=[pl.BlockSpec((None, 2 * SUBLANES, 2 * DIL_BLK), lambda h: (h, 0, 0))],
        out_specs=pl.BlockSpec((None, DIL_NDELTA + 1, DIL_BLK, DIL_BLK), lambda h: (h, 0, 0, 0)),
        out_shape=jax.ShapeDtypeStruct((DIL_HEADS, DIL_NDELTA + 1, DIL_BLK, DIL_BLK), F32),
        compiler_params=_params(("parallel",), [((DIL_NDELTA + 1, DIL_BLK, DIL_BLK), F32)],
                                [((DIL_BLK, 2 * DIL_BLK), F32)] * 2),
        name="dil_bias_tiles",
    )(f)


def _dil_body(q_ref, k_ref, v_ref, t_ref, o_ref, s_ref):
    def key_start(i, d):
        return pl.multiple_of(jnp.clip(i + (d - DIL_REACH), 0, DIL_NBLK - 1) * DIL_BLK, DIL_BLK)

    def logits(i, buf):
        q = q_ref[pl.ds(pl.multiple_of(i * DIL_BLK, DIL_BLK), DIL_BLK), :]
        for d in range(DIL_NDELTA):
            j = i + (d - DIL_REACH)
            tile = jnp.where((j >= 0) & (j < DIL_NBLK), d, DIL_NDELTA)
            s_ref[buf, :, d * DIL_BLK:(d + 1) * DIL_BLK] = (
                _dot_nt(q, k_ref[pl.ds(key_start(i, d), DIL_BLK), :]) * QK_SCALE2 + t_ref[tile])

    def finish(i, buf):
        m = jnp.max(s_ref[buf], axis=-1, keepdims=True)
        l = jnp.zeros((DIL_BLK, 1), F32)
        acc = jnp.zeros((DIL_BLK, HEAD_DIM), F32)
        for d in range(DIL_NDELTA):
            p = jnp.exp2(s_ref[buf, :, d * DIL_BLK:(d + 1) * DIL_BLK] - m)
            l = l + jnp.sum(p, axis=-1, keepdims=True)
            acc = acc + _dot(p.astype(BF16), v_ref[pl.ds(key_start(i, d), DIL_BLK), :])
        o_ref[pl.ds(pl.multiple_of(i * DIL_BLK, DIL_BLK), DIL_BLK), :] = (acc * (1.0 / l)).astype(o_ref.dtype)

    logits(0, 0)

    def pair(ii, carry):
        i = 2 * ii
        logits(i + 1, 1)
        finish(i, 0)
        logits(jnp.minimum(i + 2, DIL_NBLK - 1), 0)
        finish(i + 1, 1)
        return carry

    lax.fori_loop(0, DIL_NBLK // 2, pair, 0)


def _dil_attention(u, tiles):
    qb, kb, vb = OFF_QC // HEAD_DIM, OFF_KC // HEAD_DIM, OFF_VC // HEAD_DIM
    blk = (SEQ, HEAD_DIM)
    return pl.pallas_call(
        _dil_body,
        grid=(DIL_HEADS, BATCH),
        in_specs=[pl.BlockSpec(blk, lambda h, b: (b, qb + h)),
                  pl.BlockSpec(blk, lambda h, b: (b, kb + h)),
                  pl.BlockSpec(blk, lambda h, b: (b, vb + h)),
                  pl.BlockSpec((None, DIL_NDELTA + 1, DIL_BLK, DIL_BLK), lambda h, b: (h, 0, 0, 0))],
        out_specs=pl.BlockSpec(blk, lambda h, b: (b, h)),
        out_shape=jax.ShapeDtypeStruct((M_TOK, DIL_W), BF16),
        scratch_shapes=[pltpu.VMEM((2, DIL_BLK, DIL_NDELTA * DIL_BLK), F32)],
        compiler_params=_params(
            ("parallel", "parallel"),
            [(blk, BF16)] * 4 + [((DIL_NDELTA + 1, DIL_BLK, DIL_BLK), F32)],
            [((DIL_BLK, DIL_NDELTA * DIL_BLK), F32)] * 3),
        name="dil_attention",
    )(u, u, u, tiles)


def _rotary_tables():
    inv_freq = jnp.asarray((ROPE_BASE ** (-np.arange(0, RET_DK, 2, dtype=np.float32) / RET_DK)).astype(np.float32))
    ang = jnp.arange(SEQ, dtype=F32)[:, None] * inv_freq[None, :]
    cos, sin = jnp.cos(ang), jnp.sin(ang)
    return jnp.concatenate([cos, cos], axis=-1), jnp.concatenate([-sin, sin], axis=-1)


def _ret_tables(dec_ref, h, d_ref, xz_ref, gc_ref):
    c = RET_BLK
    row = lax.broadcasted_iota(jnp.int32, (c, c), 0)
    col = lax.broadcasted_iota(jnp.int32, (c, c), 1)
    diff = (row - col).astype(F32)
    log_f = -jnp.exp(jnp.full((c, c), dec_ref[0, h], F32))
    log_b = -jnp.exp(jnp.full((c, c), dec_ref[1, h], F32))
    d_ref[...] = jnp.where(diff >= 0.0, jnp.exp(jnp.maximum(diff, 0.0) * log_f),
                           jnp.exp(jnp.maximum(-diff, 0.0) * log_b))
    j = lax.broadcasted_iota(jnp.int32, (c, RET_DV), 0).astype(F32)
    log_f2 = -jnp.exp(jnp.full((c, RET_DV), dec_ref[0, h], F32))
    log_b2 = -jnp.exp(jnp.full((c, RET_DV), dec_ref[1, h], F32))
    xz_ref[0] = jnp.exp((j + 1.0) * log_f2)
    xz_ref[1] = jnp.exp((c - 1.0 - j) * log_f2)
    xz_ref[2] = jnp.exp((c - j) * log_b2)
    xz_ref[3] = jnp.exp(j * log_b2)
    gc_ref[0] = jnp.exp(c * -jnp.exp(jnp.full((RET_DK, RET_DV), dec_ref[0, h], F32)))
    gc_ref[1] = jnp.exp(c * -jnp.exp(jnp.full((RET_DK, RET_DV), dec_ref[1, h], F32)))


def _ret_body(dec_ref, q_ref, k_ref, v_ref, gate_ref, cos_ref, sin_ref, o_ref,
              qr_ref, kr_ref, y_ref, d_ref, xz_ref, gc_ref, sf_ref, sb_ref):
    c = RET_BLK
    nblk = SEQ // c
    _ret_tables(dec_ref, pl.program_id(0), d_ref, xz_ref, gc_ref)
    sf_ref[...] = jnp.zeros_like(sf_ref)
    sb_ref[...] = jnp.zeros_like(sb_ref)

    def rotate(x, cs, sn):
        return x * cs + pltpu.roll(x, RET_DK // 2, 1) * sn

    def up(i, carry):
        sl = pl.ds(pl.multiple_of(i * c, c), c)
        cs, sn = cos_ref[sl, :], sin_ref[sl, :]
        q = rotate(q_ref[sl, :].astype(F32), cs, sn).astype(BF16)
        k = (rotate(k_ref[sl, :].astype(F32), cs, sn) * (RET_DK ** -0.5)).astype(BF16)
        qr_ref[sl, :] = q
        kr_ref[sl, :] = k
        v = v_ref[sl, :]
        inner = (_dot_nt(q, k) * d_ref[...]).astype(BF16)
        state = sf_ref[...]
        y_ref[sl, :] = _dot(inner, v) + _dot(q, state.astype(BF16)) * xz_ref[0]
        sf_ref[...] = state * gc_ref[0] + _dot_tn(k, (v.astype(F32) * xz_ref[1]).astype(BF16))
        return carry

    lax.fori_loop(0, nblk, up, 0, unroll=4)

    def down(ii, carry):
        sl = pl.ds(pl.multiple_of((nblk - 1 - ii) * c, c), c)
        q, k, v = qr_ref[sl, :], kr_ref[sl, :], v_ref[sl, :]
        state = sb_ref[...]
        y = y_ref[sl, :] + _dot(q, state.astype(BF16)) * xz_ref[2]
        sb_ref[...] = state * gc_ref[1] + _dot_tn(k, (v.astype(F32) * xz_ref[3]).astype(BF16))
        mu = jnp.mean(y, axis=-1, keepdims=True)
        yc = y - mu
        var = jnp.mean(yc * yc, axis=-1, keepdims=True)
        gate = gate_ref[sl, :].astype(F32)
        o_ref[sl, :] = (gate * _sigmoid(gate) * (yc * lax.rsqrt(var + EPS))).astype(o_ref.dtype)
        return carry

    lax.fori_loop(0, nblk, down, 0, unroll=4)


def _retention(u, decay, cos2, sin2):
    qb, kb = OFF_QB // RET_DK, OFF_KB // RET_DK
    vb, gb = OFF_VB // RET_DV, OFF_GB // RET_DV
    qk_blk, v_blk = (SEQ, RET_DK), (SEQ, RET_DV)
    scratch = [(qk_blk, BF16), (qk_blk, BF16), (v_blk, F32), ((RET_BLK, RET_BLK), F32),
               ((4, RET_BLK, RET_DV), F32), ((2, RET_DK, RET_DV), F32),
               ((RET_DK, RET_DV), F32), ((RET_DK, RET_DV), F32)]
    return pl.pallas_call(
        _ret_body,
        grid=(RET_HEADS, BATCH),
        in_specs=[pl.BlockSpec(memory_space=pltpu.SMEM),
                  pl.BlockSpec(qk_blk, lambda h, b: (b, qb + h)),
                  pl.BlockSpec(qk_blk, lambda h, b: (b, kb + h)),
                  pl.BlockSpec(v_blk, lambda h, b: (b, vb + h)),
                  pl.BlockSpec(v_blk, lambda h, b: (b, gb + h)),
                  pl.BlockSpec(qk_blk, lambda h, b: (0, 0)),
                  pl.BlockSpec(qk_blk, lambda h, b: (0, 0))],
        out_specs=pl.BlockSpec(v_blk, lambda h, b: (b, h)),
        out_shape=jax.ShapeDtypeStruct((M_TOK, RET_V_W), BF16),
        scratch_shapes=[pltpu.VMEM(s, d) for s, d in scratch],
        compiler_params=_params(
            ("parallel", "parallel"),
            [(qk_blk, BF16)] * 2 + [(v_blk, BF16)] * 3 + [(qk_blk, F32)] * 2,
            scratch + [((RET_BLK, RET_DV), F32)] * 4),
        name="retention",
    )(decay, u, u, u, u, cos2, sin2)


def _merge_body(oa_ref, ob_ref, oc_ref, w_ref, sa_ref, sb_ref, sc_ref, o_ref):
    def branch(o, row0, s):
        w = w_ref[row0:row0 + o.shape[1], :].astype(BF16)
        return _sigmoid(s[...].astype(F32)) * _dot(o[...], w)

    o_ref[...] = (branch(oa_ref, 0, sa_ref) + branch(ob_ref, NA_W, sb_ref)
                  + branch(oc_ref, NA_W + RET_V_W, sc_ref)).astype(o_ref.dtype)


def _merge(o_a, o_b, o_c, w_branch, layer, u, *, tm, tn):
    sa, sb, sc = OFF_SA // tn, OFF_SB // tn, OFF_SC // tn
    mix_w = NA_W + RET_V_W + DIL_W

    def ospec(k):
        return pl.BlockSpec((tm, k), lambda i, j: (i, 0))

    return pl.pallas_call(
        _merge_body,
        grid=(M_TOK // tm, D_MODEL // tn),
        in_specs=[ospec(NA_W), ospec(RET_V_W), ospec(DIL_W),
                  pl.BlockSpec((None, mix_w, tn), lambda i, j: (layer, 0, j)),
                  pl.BlockSpec((tm, tn), lambda i, j: (i, sa + j)),
                  pl.BlockSpec((tm, tn), lambda i, j: (i, sb + j)),
                  pl.BlockSpec((tm, tn), lambda i, j: (i, sc + j))],
        out_specs=pl.BlockSpec((tm, tn), lambda i, j: (i, j)),
        out_shape=jax.ShapeDtypeStruct((M_TOK, D_MODEL), BF16),
        compiler_params=_params(
            ("parallel", "arbitrary"),
            [((tm, mix_w), BF16), ((mix_w, tn), F32), ((tm, tn), BF16)] + [((tm, tn), BF16)] * 3,
            [((tm, tn), F32)] * 4 + [((RET_V_W, tn), BF16)]),
        name="branch_merge",
    )(o_a, o_b, o_c, w_branch, u, u, u)


def _cross_body(x_ref, g_ref, wq_ref, kv_ref, wo_ref, o_ref):
    x = x_ref[...]
    q = _dot(_rms(x, g_ref[...]).astype(BF16), wq_ref[...].astype(BF16)).astype(BF16)
    heads = []
    for h in range(CROSS_HEADS):
        lo = h * HEAD_DIM
        k = kv_ref[:, lo:lo + HEAD_DIM]
        v = kv_ref[:, CROSS_W + lo:CROSS_W + lo + HEAD_DIM]
        s = _dot_nt(q[:, lo:lo + HEAD_DIM], k) * QK_SCALE2
        m = jnp.max(s, axis=-1, keepdims=True)
        p = jnp.exp2(s - m)
        l = jnp.sum(p, axis=-1, keepdims=True)
        heads.append((_dot(p.astype(BF16), v) * (1.0 / l)).astype(BF16))
    o_ref[...] = x + _dot(jnp.concatenate(heads, axis=-1), wo_ref[...].astype(BF16))


def _cross(x, g, w_q, kv, w_o, layer, *, tm):
    per_batch = SEQ // tm
    g2 = g[layer].reshape(1, D_MODEL)
    return pl.pallas_call(
        _cross_body,
        grid=(M_TOK // tm,),
        in_specs=[pl.BlockSpec((tm, D_MODEL), lambda i: (i, 0)),
                  pl.BlockSpec((1, D_MODEL), lambda i: (0, 0)),
                  pl.BlockSpec((None, D_MODEL, CROSS_W), lambda i: (layer, 0, 0)),
                  pl.BlockSpec((MEM_LEN, 2 * CROSS_W), lambda i: (i // per_batch, 0)),
                  pl.BlockSpec((None, CROSS_W, D_MODEL), lambda i: (layer, 0, 0))],
        out_specs=pl.BlockSpec((tm, D_MODEL), lambda i: (i, 0)),
        out_shape=jax.ShapeDtypeStruct((M_TOK, D_MODEL), F32),
        compiler_params=_params(
            ("parallel",),
            [((tm, D_MODEL), F32)] * 2 + [((D_MODEL, CROSS_W), F32), ((MEM_LEN, 2 * CROSS_W), BF16),
                                          ((CROSS_W, D_MODEL), F32)],
            [((tm, D_MODEL), F32)] * 2 + [((tm, CROSS_W), F32)] * 2 + [((tm, MEM_LEN), F32)] * 2
            + [((D_MODEL, CROSS_W), BF16)] * 2),
        name="cross_attention",
    )(x, g2, w_q, kv, w_o)


def _mlp_body(x_ref, g_ref, w1_ref, w2_ref, o_ref, xn_ref):
    @pl.when(pl.program_id(1) == 0)
    def _():
        x = x_ref[...]
        xn_ref[...] = _rms(x, g_ref[...]).astype(BF16)
        o_ref[...] = x

    hid = jnp.square(jnp.maximum(_dot(xn_ref[...], w1_ref[...]), 0.0)).astype(BF16)
    o_ref[...] += _dot(hid, w2_ref[...])


def _mlp(x, g, w1, w2, layer, *, tm, tf):
    m, k = x.shape
    return pl.pallas_call(
        _mlp_body,
        grid=(m // tm, D_FF // tf),
        in_specs=[pl.BlockSpec((tm, k), lambda i, f: (i, 0)),
                  pl.BlockSpec((1, k), lambda i, f: (0, 0)),
                  pl.BlockSpec((None, k, tf), lambda i, f: (layer, 0, f)),
                  pl.BlockSpec((None, tf, k), lambda i, f: (layer, f, 0))],
        out_specs=pl.BlockSpec((tm, k), lambda i, f: (i, 0)),
        out_shape=jax.ShapeDtypeStruct((m, k), F32),
        scratch_shapes=[pltpu.VMEM((tm, k), BF16)],
        compiler_params=_params(
            ("parallel", "arbitrary"),
            [((tm, k), F32), ((1, k), F32), ((k, tf), BF16), ((tf, k), BF16), ((tm, k), F32)],
            [((tm, k), BF16), ((tm, tf), F32), ((tm, tf), BF16), ((tm, k), F32)]),
        name="mlp",
    )(x, g[layer].reshape(1, k), w1, w2)


def _final_norm_body(x_ref, g_ref, o_ref):
    o_ref[...] = _rms(x_ref[...], g_ref[...])


def _final_norm(x, g, *, tm):
    return pl.pallas_call(
        _final_norm_body,
        grid=(M_TOK // tm,),
        in_specs=[pl.BlockSpec((tm, D_MODEL), lambda i: (i, 0)),
                  pl.BlockSpec((1, D_MODEL), lambda i: (0, 0))],
        out_specs=pl.BlockSpec((tm, D_MODEL), lambda i: (i, 0)),
        out_shape=jax.ShapeDtypeStruct((M_TOK, D_MODEL), F32),
        compiler_params=_params(("parallel",), [((tm, D_MODEL), F32)] * 2, [((tm, D_MODEL), F32)]),
        name="final_norm",
    )(x, g.reshape(1, D_MODEL))


def kernel(x, mem, t5_bias, norm_mix_g, w_in, na_rpb, ret_decay, w_branch, w_out, norm_cross_g, norm_mem_g,
           w_cq, w_ckv, w_co, norm_mlp_g, w_mlp1, w_mlp2, final_norm_g):
    xs = x.reshape(M_TOK, D_MODEL)
    mem2 = mem.reshape(BATCH * MEM_LEN, D_MODEL)
    w_mlp1_h, w_mlp2_h = w_mlp1.astype(BF16), w_mlp2.astype(BF16)
    cos2, sin2 = _rotary_tables()
    dil_tiles = _dil_bias_tiles(t5_bias)

    for layer in range(DEPTH):
        u = _normmm(xs, norm_mix_g, w_in, layer, tm=1024, tn=768, name="in_proj")
        o_a = _na_attention(u, _na_bias_tiles(na_rpb[layer]))
        o_b = _retention(u, ret_decay[layer], cos2, sin2)
        o_c = _dil_attention(u, dil_tiles)
        merged = _merge(o_a, o_b, o_c, w_branch, layer, u, tm=1024, tn=512)
        xs = _mmres(merged, w_out, layer, xs, tm=1024, tn=1024, name="out_proj")
        kv = _normmm(mem2, norm_mem_g, w_ckv, layer, tm=BATCH * MEM_LEN, tn=512, name="mem_kv_proj")
        xs = _cross(xs, norm_cross_g, w_cq, kv, w_co, layer, tm=512)
        xs = _mlp(xs, norm_mlp_g, w_mlp1_h, w_mlp2_h, layer, tm=512, tf=1024)
    return _final_norm(xs, final_norm_g, tm=512).reshape(BATCH, SEQ, D_MODEL)
```

```python
import numpy as np
import jax
import jax.numpy as jnp
from jax import lax
from jax.experimental import pallas as pl
from jax.experimental.pallas import tpu as pltpu

D_MODEL = 2048
BATCH = 2
SEQ = 4096
DEPTH = 4
MEM_LEN = 256
HEAD_DIM = 128
GRID_W = 64
NA_HEADS = 6
NA_ROWS = 8
NA_COLS = 16
RET_HEADS = 4
RET_DK = 128
RET_DV = 256
RET_BLK = 256
ROPE_BASE = 10000.0
DIL_HEADS = 6
DIL_PAIRS = ((128, 1), (512, 4), (2048, 16))
T5_BUCKETS = 32
T5_MAX_DIST = 1024
CROSS_HEADS = 4
D_FF = 4 * D_MODEL
EPS = 1e-6
NEG = -1e30

NA_W = NA_HEADS * HEAD_DIM
RET_QK_W = RET_HEADS * RET_DK
RET_V_W = RET_HEADS * RET_DV
DIL_W = DIL_HEADS * HEAD_DIM
CROSS_W = CROSS_HEADS * HEAD_DIM
IN_SPLITS = (NA_W, NA_W, NA_W, RET_QK_W, RET_QK_W, RET_V_W, RET_V_W,
             DIL_W, DIL_W, DIL_W, D_MODEL, D_MODEL, D_MODEL)
IN_W = sum(IN_SPLITS)
(OFF_QA, OFF_KA, OFF_VA, OFF_QB, OFF_KB, OFF_VB, OFF_GB,
 OFF_QC, OFF_KC, OFF_VC, OFF_SA, OFF_SB, OFF_SC) = [int(o) for o in np.cumsum((0,) + IN_SPLITS[:-1])]

M_TOK = BATCH * SEQ
ATT_SCALE = HEAD_DIM ** -0.5
LOG2E = float(np.log2(np.e))
QK_SCALE2 = ATT_SCALE * LOG2E

V7X_VMEM_BYTES = 64 * 1024 * 1024
LANES = 128
SUBLANES = 8

NA_QROWS = 4
NA_KROWS = NA_QROWS + NA_ROWS
NA_QBLK = NA_QROWS * GRID_W
NA_KBLK = NA_KROWS * GRID_W
GRID_H = SEQ // GRID_W
NA_NBLK = GRID_H // NA_QROWS

DIL_BLK = 256
DIL_REACH = (max(w for w, _ in DIL_PAIRS) // 2 + DIL_BLK - 1) // DIL_BLK
DIL_NDELTA = 2 * DIL_REACH + 1
DIL_NBLK = SEQ // DIL_BLK

F32 = jnp.float32
BF16 = jnp.bfloat16


def _nbytes(shape, dtype):
    return int(np.prod(shape)) * jnp.dtype(dtype).itemsize


def _params(semantics, pipelined, resident):
    need = 2 * sum(_nbytes(s, d) for s, d in pipelined) + sum(_nbytes(s, d) for s, d in resident)
    assert need < V7X_VMEM_BYTES, need
    return pltpu.CompilerParams(dimension_semantics=semantics,
                                vmem_limit_bytes=min(V7X_VMEM_BYTES, need + need // 4))


def _rms(x, g):
    ms = jnp.mean(x * x, axis=-1, keepdims=True)
    return x * lax.rsqrt(ms + EPS) * g


def _sigmoid(x):
    return 1.0 / (1.0 + jnp.exp(-x))


def _dot(a, b):
    return jnp.dot(a, b, preferred_element_type=F32)


def _dot_nt(a, b):
    return lax.dot_general(a, b, (((1,), (1,)), ((), ())), preferred_element_type=F32)


def _dot_tn(a, b):
    return lax.dot_general(a, b, (((0,), (0,)), ((), ())), preferred_element_type=F32)


def _normmm_body(x_ref, g_ref, w_ref, o_ref, xn_ref):
    @pl.when(pl.program_id(1) == 0)
    def _():
        xn_ref[...] = _rms(x_ref[...], g_ref[...]).astype(BF16)

    o_ref[...] = _dot(xn_ref[...], w_ref[...]).astype(o_ref.dtype)


def _normmm(x, g, w, layer, *, tm, tn, name):
    m, k = x.shape
    n = w.shape[-1]
    g2 = g[layer].reshape(1, k)
    return pl.pallas_call(
        _normmm_body,
        grid=(m // tm, n // tn),
        in_specs=[pl.BlockSpec((tm, k), lambda i, j: (i, 0)),
                  pl.BlockSpec((1, k), lambda i, j: (0, 0)),
                  pl.BlockSpec((None, k, tn), lambda i, j: (layer, 0, j))],
        out_specs=pl.BlockSpec((tm, tn), lambda i, j: (i, j)),
        out_shape=jax.ShapeDtypeStruct((m, n), BF16),
        scratch_shapes=[pltpu.VMEM((tm, k), BF16)],
        compiler_params=_params(
            ("parallel", "arbitrary"),
            [((tm, k), F32), ((1, k), F32), ((k, tn), BF16), ((tm, tn), BF16)],
            [((tm, k), BF16), ((tm, k), F32), ((tm, tn), F32)]),
        name=name,
    )(x, g2, w)


NA_RPB_ROWS = 2 * NA_ROWS - 1
NA_RPB_COLS = 2 * NA_COLS - 1
NA_TILE_ROWS = ((0, 0), (NA_QROWS, 0), (GRID_H - NA_QROWS, GRID_H - NA_KROWS))


def _na_tiles_body(rpb_ref, o_ref):
    shape = (GRID_W, LANES)
    lane = lax.broadcasted_iota(jnp.int32, shape, 1)
    cq = lax.broadcasted_iota(jnp.int32, shape, 0)
    ck = lane & (GRID_W - 1)
    c0 = jnp.clip(cq - NA_COLS // 2, 0, GRID_W - NA_COLS)
    col_ok = (ck >= c0) & (ck < c0 + NA_COLS)
    left = lane < GRID_W
    neg = jnp.full(shape, NEG, F32)
    lo, hi = [], []
    for a in range(NA_RPB_ROWS):
        row = jnp.broadcast_to(rpb_ref[a:a + 1, :], shape)
        lo.append(pltpu.roll(row, LANES - (NA_COLS - 1), 1, stride=1, stride_axis=0))
        hi.append(pltpu.roll(row, GRID_W - (NA_COLS - 1), 1, stride=1, stride_axis=0))
    for t, (qrow0, krow0) in enumerate(NA_TILE_ROWS):
        for rq in range(NA_QROWS):
            r = qrow0 + rq
            r0 = min(max(r - NA_ROWS // 2, 0), GRID_H - NA_ROWS)
            for pair in range(NA_KROWS // 2):
                halves = []
                for side, table in ((0, lo), (1, hi)):
                    rk = krow0 + 2 * pair + side
                    halves.append(table[rk - r + NA_ROWS - 1] if r0 <= rk < r0 + NA_ROWS else neg)
                blk = jnp.where(col_ok, jnp.where(left, halves[0], halves[1]) * LOG2E, NEG)
                o_ref[t, rq * GRID_W:(rq + 1) * GRID_W, pair * LANES:(pair + 1) * LANES] = blk


def _na_bias_tiles(rpb):
    padded = jnp.pad(rpb.astype(F32), ((0, 0), (0, 2 * SUBLANES - NA_RPB_ROWS), (0, LANES - NA_RPB_COLS)))
    return pl.pallas_call(
        _na_tiles_body,
        grid=(NA_HEADS,),
        in_specs=[pl.BlockSpec((None, 2 * SUBLANES, LANES), lambda h: (h, 0, 0))],
        out_specs=pl.BlockSpec((None, 3, NA_QBLK, NA_KBLK), lambda h: (h, 0, 0, 0)),
        out_shape=jax.ShapeDtypeStruct((NA_HEADS, 3, NA_QBLK, NA_KBLK), F32),
        compiler_params=_params(("parallel",), [((3, NA_QBLK, NA_KBLK), F32)], [((NA_QBLK, NA_KBLK), F32)]),
        name="na_bias_tiles",
    )(padded)


def _na_body(q_ref, k_ref, v_ref, bias_ref, o_ref, s_ref):
    def starts(i):
        qrow = i * NA_QROWS
        krow = jnp.clip(qrow - NA_ROWS // 2, 0, GRID_H - NA_KROWS)
        return pl.multiple_of(qrow * GRID_W, NA_QBLK), pl.multiple_of(krow * GRID_W, NA_QBLK)

    def logits(i, buf):
        qs, ks = starts(i)
        tile = jnp.where(i == 0, 0, jnp.where(i == NA_NBLK - 1, 2, 1))
        s_ref[buf] = (_dot_nt(q_ref[pl.ds(qs, NA_QBLK), :], k_ref[pl.ds(ks, NA_KBLK), :]) * QK_SCALE2
                      + bias_ref[tile])

    def finish(i, buf):
        qs, ks = starts(i)
        s = s_ref[buf]
        m = jnp.max(s, axis=-1, keepdims=True)
        p = jnp.exp2(s - m)
        l = jnp.sum(p, axis=-1, keepdims=True)
        o = _dot(p.astype(BF16), v_ref[pl.ds(ks, NA_KBLK), :]) * (1.0 / l)
        o_ref[pl.ds(qs, NA_QBLK), :] = o.astype(o_ref.dtype)

    logits(0, 0)

    def pair(ii, carry):
        i = 2 * ii
        logits(i + 1, 1)
        finish(i, 0)
        logits(jnp.minimum(i + 2, NA_NBLK - 1), 0)
        finish(i + 1, 1)
        return carry

    lax.fori_loop(0, NA_NBLK // 2, pair, 0)


def _na_attention(u, bias):
    qb, kb, vb = OFF_QA // HEAD_DIM, OFF_KA // HEAD_DIM, OFF_VA // HEAD_DIM
    blk = (SEQ, HEAD_DIM)
    return pl.pallas_call(
        _na_body,
        grid=(NA_HEADS, BATCH),
        in_specs=[pl.BlockSpec(blk, lambda h, b: (b, qb + h)),
                  pl.BlockSpec(blk, lambda h, b: (b, kb + h)),
                  pl.BlockSpec(blk, lambda h, b: (b, vb + h)),
                  pl.BlockSpec((None, 3, NA_QBLK, NA_KBLK), lambda h, b: (h, 0, 0, 0))],
        out_specs=pl.BlockSpec(blk, lambda h, b: (b, h)),
        out_shape=jax.ShapeDtypeStruct((M_TOK, NA_W), BF16),
        scratch_shapes=[pltpu.VMEM((2, NA_QBLK, NA_KBLK), F32)],
        compiler_params=_params(
            ("parallel", "parallel"),
            [(blk, BF16)] * 4 + [((3, NA_QBLK, NA_KBLK), F32)],
            [((NA_QBLK, NA_KBLK), F32)] * 3),
        name="na_attention",
    )(u, u, u, bias)


def _t5_bucket(rel):
    nb = T5_BUCKETS // 2
    ret = (rel > 0).astype(np.int32) * nb
    n = np.abs(rel)
    max_exact = nb // 2
    large = max_exact + (np.log(np.maximum(n, 1) / max_exact) / np.log(T5_MAX_DIST / max_exact)
                         * (nb - max_exact)).astype(np.int32)
    large = np.minimum(large, nb - 1)
    return (ret + np.where(n < max_exact, n, large)).astype(np.int32)


def _dil_tiles_body(f_ref, o_ref):
    for d in range(DIL_NDELTA):
        row = jnp.broadcast_to(f_ref[d:d + 1, :], (DIL_BLK, 2 * DIL_BLK))
        o_ref[d] = pltpu.roll(row, DIL_BLK, 1, stride=1, stride_axis=0)[:, :DIL_BLK]
    o_ref[DIL_NDELTA] = jnp.full((DIL_BLK, DIL_BLK), NEG, F32)


def _dil_bias_tiles(t5_bias):
    delta = np.arange(-DIL_REACH, DIL_REACH + 1)[:, None]
    off = delta * DIL_BLK + np.arange(-DIL_BLK, DIL_BLK)[None, :]
    count = np.zeros(off.shape, np.int32)
    for w, d in DIL_PAIRS:
        count += ((off % d == 0) & (np.abs(off) <= w // 2)).astype(np.int32)
    bucket = _t5_bucket(np.clip(off, -T5_MAX_DIST, T5_MAX_DIST))
    bias = jnp.take(t5_bias.T.astype(F32), jnp.asarray(bucket.reshape(-1)), axis=1).reshape((DIL_HEADS,) + off.shape)
    logc = jnp.log(jnp.asarray(np.maximum(count, 1), F32))
    f = jnp.where(jnp.asarray(count > 0)[None], (bias + logc[None]) * LOG2E, NEG)
    f = jnp.pad(f, ((0, 0), (0, 2 * SUBLANES - DIL_NDELTA), (0, 0)))
    return pl.pallas_call(
        _dil_tiles_body,
        grid=(DIL_HEADS,),
        in_specs=[pl.BlockSpec((None, 2 * SUBLANES, 2 * DIL_BLK), lambda h: (h, 0, 0))],
        out_specs=pl.BlockSpec((None, DIL_NDELTA + 1, DIL_BLK, DIL_BLK), lambda h: (h, 0, 0, 0)),
        out_shape=jax.ShapeDtypeStruct((DIL_HEADS, DIL_NDELTA + 1, DIL_BLK, DIL_BLK), F32),
        compiler_params=_params(("parallel",), [((DIL_NDELTA + 1, DIL_BLK, DIL_BLK), F32)],
                                [((DIL_BLK, 2 * DIL_BLK), F32)] * 2),
        name="dil_bias_tiles",
    )(f)


def _dil_body(q_ref, k_ref, v_ref, t_ref, o_ref, s_ref):
    def key_start(i, d):
        return pl.multiple_of(jnp.clip(i + (d - DIL_REACH), 0, DIL_NBLK - 1) * DIL_BLK, DIL_BLK)

    def logits(i, buf):
        q = q_ref[pl.ds(pl.multiple_of(i * DIL_BLK, DIL_BLK), DIL_BLK), :]
        for d in range(DIL_NDELTA):
            j = i + (d - DIL_REACH)
            tile = jnp.where((j >= 0) & (j < DIL_NBLK), d, DIL_NDELTA)
            s_ref[buf, :, d * DIL_BLK:(d + 1) * DIL_BLK] = (
                _dot_nt(q, k_ref[pl.ds(key_start(i, d), DIL_BLK), :]) * QK_SCALE2 + t_ref[tile])

    def finish(i, buf):
        m = jnp.max(s_ref[buf], axis=-1, keepdims=True)
        l = jnp.zeros((DIL_BLK, 1), F32)
        acc = jnp.zeros((DIL_BLK, HEAD_DIM), F32)
        for d in range(DIL_NDELTA):
            p = jnp.exp2(s_ref[buf, :, d * DIL_BLK:(d + 1) * DIL_BLK] - m)
            l = l + jnp.sum(p, axis=-1, keepdims=True)
            acc = acc + _dot(p.astype(BF16), v_ref[pl.ds(key_start(i, d), DIL_BLK), :])
        o_ref[pl.ds(pl.multiple_of(i * DIL_BLK, DIL_BLK), DIL_BLK), :] = (acc * (1.0 / l)).astype(o_ref.dtype)

    logits(0, 0)

    def pair(ii, carry):
        i = 2 * ii
        logits(i + 1, 1)
        finish(i, 0)
        logits(jnp.minimum(i + 2, DIL_NBLK - 1), 0)
        finish(i + 1, 1)
        return carry

    lax.fori_loop(0, DIL_NBLK // 2, pair, 0)


def _dil_attention(u, tiles):
    qb, kb, vb = OFF_QC // HEAD_DIM, OFF_KC // HEAD_DIM, OFF_VC // HEAD_DIM
    blk = (SEQ, HEAD_DIM)
    return pl.pallas_call(
        _dil_body,
        grid=(DIL_HEADS, BATCH),
        in_specs=[pl.BlockSpec(blk, lambda h, b: (b, qb + h)),
                  pl.BlockSpec(blk, lambda h, b: (b, kb + h)),
                  pl.BlockSpec(blk, lambda h, b: (b, vb + h)),
                  pl.BlockSpec((None, DIL_NDELTA + 1, DIL_BLK, DIL_BLK), lambda h, b: (h, 0, 0, 0))],
        out_specs=pl.BlockSpec(blk, lambda h, b: (b, h)),
        out_shape=jax.ShapeDtypeStruct((M_TOK, DIL_W), BF16),
        scratch_shapes=[pltpu.VMEM((2, DIL_BLK, DIL_NDELTA * DIL_BLK), F32)],
        compiler_params=_params(
            ("parallel", "parallel"),
            [(blk, BF16)] * 4 + [((DIL_NDELTA + 1, DIL_BLK, DIL_BLK), F32)],
            [((DIL_BLK, DIL_NDELTA * DIL_BLK), F32)] * 3),
        name="dil_attention",
    )(u, u, u, tiles)


def _rotary_tables():
    inv_freq = jnp.asarray((ROPE_BASE ** (-np.arange(0, RET_DK, 2, dtype=np.float32) / RET_DK)).astype(np.float32))
    ang = jnp.arange(SEQ, dtype=F32)[:, None] * inv_freq[None, :]
    cos, sin = jnp.cos(ang), jnp.sin(ang)
    return jnp.concatenate([cos, cos], axis=-1), jnp.concatenate([-sin, sin], axis=-1)


def _ret_tables(dec_ref, h, d_ref, xz_ref, gc_ref):
    c = RET_BLK
    row = lax.broadcasted_iota(jnp.int32, (c, c), 0)
    col = lax.broadcasted_iota(jnp.int32, (c, c), 1)
    diff = (row - col).astype(F32)
    log_f = -jnp.exp(jnp.full((c, c), dec_ref[0, h], F32))
    log_b = -jnp.exp(jnp.full((c, c), dec_ref[1, h], F32))
    d_ref[...] = jnp.where(diff >= 0.0, jnp.exp(jnp.maximum(diff, 0.0) * log_f),
                           jnp.exp(jnp.maximum(-diff, 0.0) * log_b))
    j = lax.broadcasted_iota(jnp.int32, (c, RET_DV), 0).astype(F32)
    log_f2 = -jnp.exp(jnp.full((c, RET_DV), dec_ref[0, h], F32))
    log_b2 = -jnp.exp(jnp.full((c, RET_DV), dec_ref[1, h], F32))
    xz_ref[0] = jnp.exp((j + 1.0) * log_f2)
    xz_ref[1] = jnp.exp((c - 1.0 - j) * log_f2)
    xz_ref[2] = jnp.exp((c - j) * log_b2)
    xz_ref[3] = jnp.exp(j * log_b2)
    gc_ref[0] = jnp.exp(c * -jnp.exp(jnp.full((RET_DK, RET_DV), dec_ref[0, h], F32)))
    gc_ref[1] = jnp.exp(c * -jnp.exp(jnp.full((RET_DK, RET_DV), dec_ref[1, h], F32)))


def _ret_body(dec_ref, q_ref, k_ref, v_ref, gate_ref, cos_ref, sin_ref, o_ref,
              qr_ref, kr_ref, y_ref, d_ref, xz_ref, gc_ref, sf_ref, sb_ref):
    c = RET_BLK
    nblk = SEQ // c
    _ret_tables(dec_ref, pl.program_id(0), d_ref, xz_ref, gc_ref)
    sf_ref[...] = jnp.zeros_like(sf_ref)
    sb_ref[...] = jnp.zeros_like(sb_ref)

    def rotate(x, cs, sn):
        return x * cs + pltpu.roll(x, RET_DK // 2, 1) * sn

    def up(i, carry):
        sl = pl.ds(pl.multiple_of(i * c, c), c)
        cs, sn = cos_ref[sl, :], sin_ref[sl, :]
        q = rotate(q_ref[sl, :].astype(F32), cs, sn).astype(BF16)
        k = (rotate(k_ref[sl, :].astype(F32), cs, sn) * (RET_DK ** -0.5)).astype(BF16)
        qr_ref[sl, :] = q
        kr_ref[sl, :] = k
        v = v_ref[sl, :]
        inner = (_dot_nt(q, k) * d_ref[...]).astype(BF16)
        state = sf_ref[...]
        y_ref[sl, :] = _dot(inner, v) + _dot(q, state.astype(BF16)) * xz_ref[0]
        sf_ref[...] = state * gc_ref[0] + _dot_tn(k, (v.astype(F32) * xz_ref[1]).astype(BF16))
        return carry

    lax.fori_loop(0, nblk, up, 0, unroll=4)

    def down(ii, carry):
        sl = pl.ds(pl.multiple_of((nblk - 1 - ii) * c, c), c)
        q, k, v = qr_ref[sl, :], kr_ref[sl, :], v_ref[sl, :]
        state = sb_ref[...]
        y = y_ref[sl, :] + _dot(q, state.astype(BF16)) * xz_ref[2]
        sb_ref[...] = state * gc_ref[1] + _dot_tn(k, (v.astype(F32) * xz_ref[3]).astype(BF16))
        mu = jnp.mean(y, axis=-1, keepdims=True)
        yc = y - mu
        var = jnp.mean(yc * yc, axis=-1, keepdims=True)
        gate = gate_ref[sl, :].astype(F32)
        o_ref[sl, :] = (gate * _sigmoid(gate) * (yc * lax.rsqrt(var + EPS))).astype(o_ref.dtype)
        return carry

    lax.fori_loop(0, nblk, down, 0, unroll=4)


def _retention(u, decay, cos2, sin2):
    qb, kb = OFF_QB // RET_DK, OFF_KB // RET_DK
    vb, gb = OFF_VB // RET_DV, OFF_GB // RET_DV
    qk_blk, v_blk = (SEQ, RET_DK), (SEQ, RET_DV)
    scratch = [(qk_blk, BF16), (qk_blk, BF16), (v_blk, F32), ((RET_BLK, RET_BLK), F32),
               ((4, RET_BLK, RET_DV), F32), ((2, RET_DK, RET_DV), F32),
               ((RET_DK, RET_DV), F32), ((RET_DK, RET_DV), F32)]
    return pl.pallas_call(
        _ret_body,
        grid=(RET_HEADS, BATCH),
        in_specs=[pl.BlockSpec(memory_space=pltpu.SMEM),
                  pl.BlockSpec(qk_blk, lambda h, b: (b, qb + h)),
                  pl.BlockSpec(qk_blk, lambda h, b: (b, kb + h)),
                  pl.BlockSpec(v_blk, lambda h, b: (b, vb + h)),
                  pl.BlockSpec(v_blk, lambda h, b: (b, gb + h)),
                  pl.BlockSpec(qk_blk, lambda h, b: (0, 0)),
                  pl.BlockSpec(qk_blk, lambda h, b: (0, 0))],
        out_specs=pl.BlockSpec(v_blk, lambda h, b: (b, h)),
        out_shape=jax.ShapeDtypeStruct((M_TOK, RET_V_W), BF16),
        scratch_shapes=[pltpu.VMEM(s, d) for s, d in scratch],
        compiler_params=_params(
            ("parallel", "parallel"),
            [(qk_blk, BF16)] * 2 + [(v_blk, BF16)] * 3 + [(qk_blk, F32)] * 2,
            scratch + [((RET_BLK, RET_DV), F32)] * 4),
        name="retention",
    )(decay, u, u, u, u, cos2, sin2)


def _postmix_body(oa_ref, ob_ref, oc_ref, wb_ref, sa_ref, sb_ref, sc_ref, wo_ref, x_ref,
                  g_ref, wq_ref, kv_ref, wco_ref, o_ref):
    j = pl.program_id(1)

    @pl.when(j == 0)
    def _():
        o_ref[...] = x_ref[...]

    def branch(o, row0, s):
        return _sigmoid(s[...].astype(F32)) * _dot(o[...], wb_ref[row0:row0 + o.shape[1], :])

    merged = branch(oa_ref, 0, sa_ref) + branch(ob_ref, NA_W, sb_ref) + branch(oc_ref, NA_W + RET_V_W, sc_ref)
    o_ref[...] += _dot(merged.astype(BF16), wo_ref[...])

    @pl.when(j == pl.num_programs(1) - 1)
    def _():
        x1 = o_ref[...]
        q = _dot(_rms(x1, g_ref[...]).astype(BF16), wq_ref[...]).astype(BF16)
        heads = []
        for h in range(CROSS_HEADS):
            lo = h * HEAD_DIM
            k = kv_ref[:, lo:lo + HEAD_DIM]
            v = kv_ref[:, CROSS_W + lo:CROSS_W + lo + HEAD_DIM]
            s = _dot_nt(q[:, lo:lo + HEAD_DIM], k) * QK_SCALE2
            m = jnp.max(s, axis=-1, keepdims=True)
            p = jnp.exp2(s - m)
            l = jnp.sum(p, axis=-1, keepdims=True)
            heads.append((_dot(p.astype(BF16), v) * (1.0 / l)).astype(BF16))
        o_ref[...] = x1 + _dot(jnp.concatenate(heads, axis=-1), wco_ref[...])


def _postmix(o_a, o_b, o_c, w_branch, u, w_out, x, g, w_cq, kv, w_co, layer, *, tm, tn):
    sa, sb, sc = OFF_SA // tn, OFF_SB // tn, OFF_SC // tn
    mix_w = NA_W + RET_V_W + DIL_W
    per_batch = SEQ // tm
    once = pl.Buffered(1)

    def ospec(k):
        return pl.BlockSpec((tm, k), lambda i, j: (i, 0))

    return pl.pallas_call(
        _postmix_body,
        grid=(M_TOK // tm, D_MODEL // tn),
        in_specs=[ospec(NA_W), ospec(RET_V_W), ospec(DIL_W),
                  pl.BlockSpec((None, mix_w, tn), lambda i, j: (layer, 0, j)),
                  pl.BlockSpec((tm, tn), lambda i, j: (i, sa + j)),
                  pl.BlockSpec((tm, tn), lambda i, j: (i, sb + j)),
                  pl.BlockSpec((tm, tn), lambda i, j: (i, sc + j)),
                  pl.BlockSpec((None, tn, D_MODEL), lambda i, j: (layer, j, 0)),
                  ospec(D_MODEL),
                  pl.BlockSpec((1, D_MODEL), lambda i, j: (0, 0), pipeline_mode=once),
                  pl.BlockSpec((None, D_MODEL, CROSS_W), lambda i, j: (layer, 0, 0), pipeline_mode=once),
                  pl.BlockSpec((MEM_LEN, 2 * CROSS_W), lambda i, j: (i // per_batch, 0)),
                  pl.BlockSpec((None, CROSS_W, D_MODEL), lambda i, j: (layer, 0, 0), pipeline_mode=once)],
        out_specs=ospec(D_MODEL),
        out_shape=jax.ShapeDtypeStruct((M_TOK, D_MODEL), F32),
        compiler_params=_params(
            ("parallel", "arbitrary"),
            [((tm, mix_w), BF16), ((mix_w, tn), BF16)] + [((tm, tn), BF16)] * 3
            + [((tn, D_MODEL), BF16), ((tm, D_MODEL), F32), ((MEM_LEN, 2 * CROSS_W), BF16), ((tm, D_MODEL), F32)],
            [((D_MODEL, CROSS_W), BF16)] * 2 + [((tm, tn), F32)] * 4 + [((tm, D_MODEL), F32)] * 3
            + [((tm, CROSS_W), F32)] * 2 + [((tm, MEM_LEN), F32)] * 2),
        name="post_mix",
    )(o_a, o_b, o_c, w_branch, u, u, u, w_out, x, g[layer].reshape(1, D_MODEL), w_cq, kv, w_co)


def _mlp_body(x_ref, g_ref, w1_ref, w2_ref, o_ref, xn_ref):
    @pl.when(pl.program_id(1) == 0)
    def _():
        x = x_ref[...]
        xn_ref[...] = _rms(x, g_ref[...]).astype(BF16)
        o_ref[...] = x

    hid = jnp.square(jnp.maximum(_dot(xn_ref[...], w1_ref[...]), 0.0)).astype(BF16)
    o_ref[...] += _dot(hid, w2_ref[...])


def _mlp(x, g, w1, w2, layer, *, tm, tf):
    m, k = x.shape
    return pl.pallas_call(
        _mlp_body,
        grid=(m // tm, D_FF // tf),
        in_specs=[pl.BlockSpec((tm, k), lambda i, f: (i, 0)),
                  pl.BlockSpec((1, k), lambda i, f: (0, 0)),
                  pl.BlockSpec((None, k, tf), lambda i, f: (layer, 0, f)),
                  pl.BlockSpec((None, tf, k), lambda i, f: (layer, f, 0))],
        out_specs=pl.BlockSpec((tm, k), lambda i, f: (i, 0)),
        out_shape=jax.ShapeDtypeStruct((m, k), F32),
        scratch_shapes=[pltpu.VMEM((tm, k), BF16)],
        compiler_params=_params(
            ("parallel", "arbitrary"),
            [((tm, k), F32), ((1, k), F32), ((k, tf), BF16), ((tf, k), BF16), ((tm, k), F32)],
            [((tm, k), BF16), ((tm, tf), F32), ((tm, tf), BF16), ((tm, k), F32)]),
        name="mlp",
    )(x, g[layer].reshape(1, k), w1, w2)


def _final_norm_body(x_ref, g_ref, o_ref):
    o_ref[...] = _rms(x_ref[...], g_ref[...])


def _final_norm(x, g, *, tm):
    return pl.pallas_call(
        _final_norm_body,
        grid=(M_TOK // tm,),
        in_specs=[pl.BlockSpec((tm, D_MODEL), lambda i: (i, 0)),
                  pl.BlockSpec((1, D_MODEL), lambda i: (0, 0))],
        out_specs=pl.BlockSpec((tm, D_MODEL), lambda i: (i, 0)),
        out_shape=jax.ShapeDtypeStruct((M_TOK, D_MODEL), F32),
        compiler_params=_params(("parallel",), [((tm, D_MODEL), F32)] * 2, [((tm, D_MODEL), F32)]),
        name="final_norm",
    )(x, g.reshape(1, D_MODEL))


def kernel(x, mem, t5_bias, norm_mix_g, w_in, na_rpb, ret_decay, w_branch, w_out, norm_cross_g, norm_mem_g,
           w_cq, w_ckv, w_co, norm_mlp_g, w_mlp1, w_mlp2, final_norm_g):
    xs = x.reshape(M_TOK, D_MODEL)
    mem2 = mem.reshape(BATCH * MEM_LEN, D_MODEL)
    w_in_h, w_branch_h, w_out_h = w_in.astype(BF16), w_branch.astype(BF16), w_out.astype(BF16)
    w_cq_h, w_ckv_h, w_co_h = w_cq.astype(BF16), w_ckv.astype(BF16), w_co.astype(BF16)
    w_mlp1_h, w_mlp2_h = w_mlp1.astype(BF16), w_mlp2.astype(BF16)
    cos2, sin2 = _rotary_tables()
    dil_tiles = _dil_bias_tiles(t5_bias)

    for layer in range(DEPTH):
        u = _normmm(xs, norm_mix_g, w_in_h, layer, tm=1024, tn=1536, name="in_proj")
        o_a = _na_attention(u, _na_bias_tiles(na_rpb[layer]))
        o_b = _retention(u, ret_decay[layer], cos2, sin2)
        o_c = _dil_attention(u, dil_tiles)
        kv = _normmm(mem2, norm_mem_g, w_ckv_h, layer, tm=BATCH * MEM_LEN, tn=512, name="mem_kv_proj")
        xs = _postmix(o_a, o_b, o_c, w_branch_h, u, w_out_h, xs, norm_cross_g, w_cq_h, kv, w_co_h, layer,
                      tm=512, tn=512)
        xs = _mlp(xs, norm_mlp_g, w_mlp1_h, w_mlp2_h, layer, tm=512, tf=1024)
    return _final_norm(xs, final_norm_g, tm=512).reshape(BATCH, SEQ, D_MODEL)
```

```python
import numpy as np
import jax
import jax.numpy as jnp
from jax import lax
from jax.experimental import pallas as pl
from jax.experimental.pallas import tpu as pltpu

D_MODEL = 2048
BATCH = 2
SEQ = 4096
DEPTH = 4
MEM_LEN = 256
HEAD_DIM = 128
GRID_W = 64
NA_HEADS = 6
NA_ROWS = 8
NA_COLS = 16
RET_HEADS = 4
RET_DK = 128
RET_DV = 256
RET_BLK = 256
ROPE_BASE = 10000.0
DIL_HEADS = 6
DIL_PAIRS = ((128, 1), (512, 4), (2048, 16))
T5_BUCKETS = 32
T5_MAX_DIST = 1024
CROSS_HEADS = 4
D_FF = 4 * D_MODEL
EPS = 1e-6
NEG = -1e30

NA_W = NA_HEADS * HEAD_DIM
RET_QK_W = RET_HEADS * RET_DK
RET_V_W = RET_HEADS * RET_DV
DIL_W = DIL_HEADS * HEAD_DIM
CROSS_W = CROSS_HEADS * HEAD_DIM
IN_SPLITS = (NA_W, NA_W, NA_W, RET_QK_W, RET_QK_W, RET_V_W, RET_V_W,
             DIL_W, DIL_W, DIL_W, D_MODEL, D_MODEL, D_MODEL)
IN_W = sum(IN_SPLITS)
(OFF_QA, OFF_KA, OFF_VA, OFF_QB, OFF_KB, OFF_VB, OFF_GB,
 OFF_QC, OFF_KC, OFF_VC, OFF_SA, OFF_SB, OFF_SC) = [int(o) for o in np.cumsum((0,) + IN_SPLITS[:-1])]

M_TOK = BATCH * SEQ
ATT_SCALE = HEAD_DIM ** -0.5
LOG2E = float(np.log2(np.e))
QK_SCALE2 = ATT_SCALE * LOG2E

V7X_VMEM_BYTES = 64 * 1024 * 1024
LANES = 128
SUBLANES = 8

NA_QROWS = 4
NA_KROWS = NA_QROWS + NA_ROWS
NA_QBLK = NA_QROWS * GRID_W
NA_KBLK = NA_KROWS * GRID_W
GRID_H = SEQ // GRID_W
NA_NBLK = GRID_H // NA_QROWS

DIL_BLK = 256
DIL_REACH = (max(w for w, _ in DIL_PAIRS) // 2 + DIL_BLK - 1) // DIL_BLK
DIL_NDELTA = 2 * DIL_REACH + 1
DIL_NBLK = SEQ // DIL_BLK

F32 = jnp.float32
BF16 = jnp.bfloat16


def _nbytes(shape, dtype):
    return int(np.prod(shape)) * jnp.dtype(dtype).itemsize


def _params(semantics, pipelined, resident):
    need = 2 * sum(_nbytes(s, d) for s, d in pipelined) + sum(_nbytes(s, d) for s, d in resident)
    assert need < V7X_VMEM_BYTES, need
    return pltpu.CompilerParams(dimension_semantics=semantics,
                                vmem_limit_bytes=min(V7X_VMEM_BYTES, need + need // 4))


def _rms(x, g):
    ms = jnp.mean(x * x, axis=-1, keepdims=True)
    return x * lax.rsqrt(ms + EPS) * g


def _sigmoid(x):
    return 1.0 / (1.0 + jnp.exp(-x))


def _dot(a, b):
    return jnp.dot(a, b, preferred_element_type=F32)


def _dot_nt(a, b):
    return lax.dot_general(a, b, (((1,), (1,)), ((), ())), preferred_element_type=F32)


def _dot_tn(a, b):
    return lax.dot_general(a, b, (((0,), (0,)), ((), ())), preferred_element_type=F32)


def _normmm_body(x_ref, g_ref, w_ref, o_ref, xn_ref):
    @pl.when(pl.program_id(1) == 0)
    def _():
        xn_ref[...] = _rms(x_ref[...], g_ref[...]).astype(BF16)

    o_ref[...] = _dot(xn_ref[...], w_ref[...]).astype(o_ref.dtype)


def _normmm(x, g, w, layer, *, tm, tn, name):
    m, k = x.shape
    n = w.shape[-1]
    g2 = g[layer].reshape(1, k)
    return pl.pallas_call(
        _normmm_body,
        grid=(m // tm, n // tn),
        in_specs=[pl.BlockSpec((tm, k), lambda i, j: (i, 0)),
                  pl.BlockSpec((1, k), lambda i, j: (0, 0)),
                  pl.BlockSpec((None, k, tn), lambda i, j: (layer, 0, j))],
        out_specs=pl.BlockSpec((tm, tn), lambda i, j: (i, j)),
        out_shape=jax.ShapeDtypeStruct((m, n), BF16),
        scratch_shapes=[pltpu.VMEM((tm, k), BF16)],
        compiler_params=_params(
            ("parallel", "arbitrary"),
            [((tm, k), F32), ((1, k), F32), ((k, tn), BF16), ((tm, tn), BF16)],
            [((tm, k), BF16), ((tm, k), F32), ((tm, tn), F32)]),
        name=name,
    )(x, g2, w)


NA_RPB_ROWS = 2 * NA_ROWS - 1
NA_RPB_COLS = 2 * NA_COLS - 1
NA_TILE_ROWS = ((0, 0), (NA_QROWS, 0), (GRID_H - NA_QROWS, GRID_H - NA_KROWS))


def _na_tiles_body(rpb_ref, o_ref):
    shape = (GRID_W, LANES)
    lane = lax.broadcasted_iota(jnp.int32, shape, 1)
    cq = lax.broadcasted_iota(jnp.int32, shape, 0)
    ck = lane & (GRID_W - 1)
    c0 = jnp.clip(cq - NA_COLS // 2, 0, GRID_W - NA_COLS)
    col_ok = (ck >= c0) & (ck < c0 + NA_COLS)
    left = lane < GRID_W
    neg = jnp.full(shape, NEG, F32)
    lo, hi = [], []
    for a in range(NA_RPB_ROWS):
        row = jnp.broadcast_to(rpb_ref[a:a + 1, :], shape)
        lo.append(pltpu.roll(row, LANES - (NA_COLS - 1), 1, stride=1, stride_axis=0))
        hi.append(pltpu.roll(row, GRID_W - (NA_COLS - 1), 1, stride=1, stride_axis=0))
    for t, (qrow0, krow0) in enumerate(NA_TILE_ROWS):
        for rq in range(NA_QROWS):
            r = qrow0 + rq
            r0 = min(max(r - NA_ROWS // 2, 0), GRID_H - NA_ROWS)
            for pair in range(NA_KROWS // 2):
                halves = []
                for side, table in ((0, lo), (1, hi)):
                    rk = krow0 + 2 * pair + side
                    halves.append(table[rk - r + NA_ROWS - 1] if r0 <= rk < r0 + NA_ROWS else neg)
                blk = jnp.where(col_ok, jnp.where(left, halves[0], halves[1]) * LOG2E, NEG)
                o_ref[t, rq * GRID_W:(rq + 1) * GRID_W, pair * LANES:(pair + 1) * LANES] = blk


def _na_bias_tiles(rpb):
    padded = jnp.pad(rpb.astype(F32), ((0, 0), (0, 2 * SUBLANES - NA_RPB_ROWS), (0, LANES - NA_RPB_COLS)))
    return pl.pallas_call(
        _na_tiles_body,
        grid=(NA_HEADS,),
        in_specs=[pl.BlockSpec((None, 2 * SUBLANES, LANES), lambda h: (h, 0, 0))],
        out_specs=pl.BlockSpec((None, 3, NA_QBLK, NA_KBLK), lambda h: (h, 0, 0, 0)),
        out_shape=jax.ShapeDtypeStruct((NA_HEADS, 3, NA_QBLK, NA_KBLK), F32),
        compiler_params=_params(("parallel",), [((3, NA_QBLK, NA_KBLK), F32)], [((NA_QBLK, NA_KBLK), F32)]),
        name="na_bias_tiles",
    )(padded)


def _na_body(q_ref, k_ref, v_ref, bias_ref, o_ref, s_ref):
    def starts(i):
        qrow = i * NA_QROWS
        krow = jnp.clip(qrow - NA_ROWS // 2, 0, GRID_H - NA_KROWS)
        return pl.multiple_of(qrow * GRID_W, NA_QBLK), pl.multiple_of(krow * GRID_W, NA_QBLK)

    def logits(i, buf):
        qs, ks = starts(i)
        tile = jnp.where(i == 0, 0, jnp.where(i == NA_NBLK - 1, 2, 1))
        s_ref[buf] = (_dot_nt(q_ref[pl.ds(qs, NA_QBLK), :], k_ref[pl.ds(ks, NA_KBLK), :]) * QK_SCALE2
                      + bias_ref[tile])

    def finish(i, buf):
        qs, ks = starts(i)
        s = s_ref[buf]
        m = jnp.max(s, axis=-1, keepdims=True)
        p = jnp.exp2(s - m)
        l = jnp.sum(p, axis=-1, keepdims=True)
        o = _dot(p.astype(BF16), v_ref[pl.ds(ks, NA_KBLK), :]) * (1.0 / l)
        o_ref[pl.ds(qs, NA_QBLK), :] = o.astype(o_ref.dtype)

    logits(0, 0)

    def pair(ii, carry):
        i = 2 * ii
        logits(i + 1, 1)
        finish(i, 0)
        logits(jnp.minimum(i + 2, NA_NBLK - 1), 0)
        finish(i + 1, 1)
        return carry

    lax.fori_loop(0, NA_NBLK // 2, pair, 0)


def _na_attention(u, bias):
    qb, kb, vb = OFF_QA // HEAD_DIM, OFF_KA // HEAD_DIM, OFF_VA // HEAD_DIM
    blk = (SEQ, HEAD_DIM)
    return pl.pallas_call(
        _na_body,
        grid=(NA_HEADS, BATCH),
        in_specs=[pl.BlockSpec(blk, lambda h, b: (b, qb + h)),
                  pl.BlockSpec(blk, lambda h, b: (b, kb + h)),
                  pl.BlockSpec(blk, lambda h, b: (b, vb + h)),
                  pl.BlockSpec((None, 3, NA_QBLK, NA_KBLK), lambda h, b: (h, 0, 0, 0))],
        out_specs=pl.BlockSpec(blk, lambda h, b: (b, h)),
        out_shape=jax.ShapeDtypeStruct((M_TOK, NA_W), BF16),
        scratch_shapes=[pltpu.VMEM((2, NA_QBLK, NA_KBLK), F32)],
        compiler_params=_params(
            ("parallel", "parallel"),
            [(blk, BF16)] * 4 + [((3, NA_QBLK, NA_KBLK), F32)],
            [((NA_QBLK, NA_KBLK), F32)] * 3),
        name="na_attention",
    )(u, u, u, bias)


def _t5_bucket(rel):
    nb = T5_BUCKETS // 2
    ret = (rel > 0).astype(np.int32) * nb
    n = np.abs(rel)
    max_exact = nb // 2
    large = max_exact + (np.log(np.maximum(n, 1) / max_exact) / np.log(T5_MAX_DIST / max_exact)
                         * (nb - max_exact)).astype(np.int32)
    large = np.minimum(large, nb - 1)
    return (ret + np.where(n < max_exact, n, large)).astype(np.int32)


def _dil_tiles_body(f_ref, o_ref):
    for d in range(DIL_NDELTA):
        row = jnp.broadcast_to(f_ref[d:d + 1, :], (DIL_BLK, 2 * DIL_BLK))
        o_ref[d] = pltpu.roll(row, DIL_BLK, 1, stride=1, stride_axis=0)[:, :DIL_BLK]
    o_ref[DIL_NDELTA] = jnp.full((DIL_BLK, DIL_BLK), NEG, F32)


def _dil_bias_tiles(t5_bias):
    delta = np.arange(-DIL_REACH, DIL_REACH + 1)[:, None]
    off = delta * DIL_BLK + np.arange(-DIL_BLK, DIL_BLK)[None, :]
    count = np.zeros(off.shape, np.int32)
    for w, d in DIL_PAIRS:
        count += ((off % d == 0) & (np.abs(off) <= w // 2)).astype(np.int32)
    bucket = _t5_bucket(np.clip(off, -T5_MAX_DIST, T5_MAX_DIST))
    bias = jnp.take(t5_bias.T.astype(F32), jnp.asarray(bucket.reshape(-1)), axis=1).reshape((DIL_HEADS,) + off.shape)
    logc = jnp.log(jnp.asarray(np.maximum(count, 1), F32))
    f = jnp.where(jnp.asarray(count > 0)[None], (bias + logc[None]) * LOG2E, NEG)
    f = jnp.pad(f, ((0, 0), (0, 2 * SUBLANES - DIL_NDELTA), (0, 0)))
    return pl.pallas_call(
        _dil_tiles_body,
        grid=(DIL_HEADS,),
        in_specs=[pl.BlockSpec((None, 2 * SUBLANES, 2 * DIL_BLK), lambda h: (h, 0, 0))],
        out_specs=pl.BlockSpec((None, DIL_NDELTA + 1, DIL_BLK, DIL_BLK), lambda h: (h, 0, 0, 0)),
        out_shape=jax.ShapeDtypeStruct((DIL_HEADS, DIL_NDELTA + 1, DIL_BLK, DIL_BLK), F32),
        compiler_params=_params(("parallel",), [((DIL_NDELTA + 1, DIL_BLK, DIL_BLK), F32)],
                                [((DIL_BLK, 2 * DIL_BLK), F32)] * 2),
        name="dil_bias_tiles",
    )(f)


def _dil_body(q_ref, k_ref, v_ref, t_ref, o_ref, s_ref):
    def key_start(i, d):
        return pl.multiple_of(jnp.clip(i + (d - DIL_REACH), 0, DIL_NBLK - 1) * DIL_BLK, DIL_BLK)

    def logits(i, buf):
        q = q_ref[pl.ds(pl.multiple_of(i * DIL_BLK, DIL_BLK), DIL_BLK), :]
        for d in range(DIL_NDELTA):
            j = i + (d - DIL_REACH)
            tile = jnp.where((j >= 0) & (j < DIL_NBLK), d, DIL_NDELTA)
            s_ref[buf, :, d * DIL_BLK:(d + 1) * DIL_BLK] = (
                _dot_nt(q, k_ref[pl.ds(key_start(i, d), DIL_BLK), :]) * QK_SCALE2 + t_ref[tile])

    def finish(i, buf):
        m = jnp.max(s_ref[buf], axis=-1, keepdims=True)
        l = jnp.zeros((DIL_BLK, 1), F32)
        acc = jnp.zeros((DIL_BLK, HEAD_DIM), F32)
        for d in range(DIL_NDELTA):
            p = jnp.exp2(s_ref[buf, :, d * DIL_BLK:(d + 1) * DIL_BLK] - m)
            l = l + jnp.sum(p, axis=-1, keepdims=True)
            acc = acc + _dot(p.astype(BF16), v_ref[pl.ds(key_start(i, d), DIL_BLK), :])
        o_ref[pl.ds(pl.multiple_of(i * DIL_BLK, DIL_BLK), DIL_BLK), :] = (acc * (1.0 / l)).astype(o_ref.dtype)

    logits(0, 0)

    def pair(ii, carry):
        i = 2 * ii
        logits(i + 1, 1)
        finish(i, 0)
        logits(jnp.minimum(i + 2, DIL_NBLK - 1), 0)
        finish(i + 1, 1)
        return carry

    lax.fori_loop(0, DIL_NBLK // 2, pair, 0)


def _dil_attention(u, tiles):
    qb, kb, vb = OFF_QC // HEAD_DIM, OFF_KC // HEAD_DIM, OFF_VC // HEAD_DIM
    blk = (SEQ, HEAD_DIM)
    return pl.pallas_call(
        _dil_body,
        grid=(DIL_HEADS, BATCH),
        in_specs=[pl.BlockSpec(blk, lambda h, b: (b, qb + h)),
                  pl.BlockSpec(blk, lambda h, b: (b, kb + h)),
                  pl.BlockSpec(blk, lambda h, b: (b, vb + h)),
                  pl.BlockSpec((None, DIL_NDELTA + 1, DIL_BLK, DIL_BLK), lambda h, b: (h, 0, 0, 0))],
        out_specs=pl.BlockSpec(blk, lambda h, b: (b, h)),
        out_shape=jax.ShapeDtypeStruct((M_TOK, DIL_W), BF16),
        scratch_shapes=[pltpu.VMEM((2, DIL_BLK, DIL_NDELTA * DIL_BLK), F32)],
        compiler_params=_params(
            ("parallel", "parallel"),
            [(blk, BF16)] * 4 + [((DIL_NDELTA + 1, DIL_BLK, DIL_BLK), F32)],
            [((DIL_BLK, DIL_NDELTA * DIL_BLK), F32)] * 3),
        name="dil_attention",
    )(u, u, u, tiles)


def _rotary_tables():
    inv_freq = jnp.asarray((ROPE_BASE ** (-np.arange(0, RET_DK, 2, dtype=np.float32) / RET_DK)).astype(np.float32))
    ang = jnp.arange(SEQ, dtype=F32)[:, None] * inv_freq[None, :]
    cos, sin = jnp.cos(ang), jnp.sin(ang)
    return jnp.concatenate([cos, cos], axis=-1), jnp.concatenate([-sin, sin], axis=-1)


def _ret_tables(dec_ref, h, d_ref, xz_ref, gc_ref):
    c = RET_BLK
    row = lax.broadcasted_iota(jnp.int32, (c, c), 0)
    col = lax.broadcasted_iota(jnp.int32, (c, c), 1)
    diff = (row - col).astype(F32)
    log_f = -jnp.exp(jnp.full((c, c), dec_ref[0, h], F32))
    log_b = -jnp.exp(jnp.full((c, c), dec_ref[1, h], F32))
    d_ref[...] = jnp.where(diff >= 0.0, jnp.exp(jnp.maximum(diff, 0.0) * log_f),
                           jnp.exp(jnp.maximum(-diff, 0.0) * log_b))
    j = lax.broadcasted_iota(jnp.int32, (c, RET_DV), 0).astype(F32)
    log_f2 = -jnp.exp(jnp.full((c, RET_DV), dec_ref[0, h], F32))
    log_b2 = -jnp.exp(jnp.full((c, RET_DV), dec_ref[1, h], F32))
    xz_ref[0] = jnp.exp((j + 1.0) * log_f2)
    xz_ref[1] = jnp.exp((c - 1.0 - j) * log_f2)
    xz_ref[2] = jnp.exp((c - j) * log_b2)
    xz_ref[3] = jnp.exp(j * log_b2)
    gc_ref[0] = jnp.exp(c * -jnp.exp(jnp.full((RET_DK, RET_DV), dec_ref[0, h], F32)))
    gc_ref[1] = jnp.exp(c * -jnp.exp(jnp.full((RET_DK, RET_DV), dec_ref[1, h], F32)))


def _ret_body(dec_ref, q_ref, k_ref, v_ref, gate_ref, cos_ref, sin_ref, o_ref,
              qr_ref, kr_ref, y_ref, d_ref, xz_ref, gc_ref, sf_ref, sb_ref):
    c = RET_BLK
    nblk = SEQ // c
    _ret_tables(dec_ref, pl.program_id(0), d_ref, xz_ref, gc_ref)
    sf_ref[...] = jnp.zeros_like(sf_ref)
    sb_ref[...] = jnp.zeros_like(sb_ref)

    def rotate(x, cs, sn):
        return x * cs + pltpu.roll(x, RET_DK // 2, 1) * sn

    def up(i, carry):
        sl = pl.ds(pl.multiple_of(i * c, c), c)
        cs, sn = cos_ref[sl, :], sin_ref[sl, :]
        q = rotate(q_ref[sl, :].astype(F32), cs, sn).astype(BF16)
        k = (rotate(k_ref[sl, :].astype(F32), cs, sn) * (RET_DK ** -0.5)).astype(BF16)
        qr_ref[sl, :] = q
        kr_ref[sl, :] = k
        v = v_ref[sl, :]
        inner = (_dot_nt(q, k) * d_ref[...]).astype(BF16)
        state = sf_ref[...]
        y_ref[sl, :] = _dot(inner, v) + _dot(q, state.astype(BF16)) * xz_ref[0]
        sf_ref[...] = state * gc_ref[0] + _dot_tn(k, (v.astype(F32) * xz_ref[1]).astype(BF16))
        return carry

    lax.fori_loop(0, nblk, up, 0, unroll=4)

    def down(ii, carry):
        sl = pl.ds(pl.multiple_of((nblk - 1 - ii) * c, c), c)
        q, k, v = qr_ref[sl, :], kr_ref[sl, :], v_ref[sl, :]
        state = sb_ref[...]
        y = y_ref[sl, :] + _dot(q, state.astype(BF16)) * xz_ref[2]
        sb_ref[...] = state * gc_ref[1] + _dot_tn(k, (v.astype(F32) * xz_ref[3]).astype(BF16))
        mu = jnp.mean(y, axis=-1, keepdims=True)
        yc = y - mu
        var = jnp.mean(yc * yc, axis=-1, keepdims=True)
        gate = gate_ref[sl, :].astype(F32)
        o_ref[sl, :] = (gate * _sigmoid(gate) * (yc * lax.rsqrt(var + EPS))).astype(o_ref.dtype)
        return carry

    lax.fori_loop(0, nblk, down, 0, unroll=4)


def _retention(u, decay, cos2, sin2):
    qb, kb = OFF_QB // RET_DK, OFF_KB // RET_DK
    vb, gb = OFF_VB // RET_DV, OFF_GB // RET_DV
    qk_blk, v_blk = (SEQ, RET_DK), (SEQ, RET_DV)
    scratch = [(qk_blk, BF16), (qk_blk, BF16), (v_blk, F32), ((RET_BLK, RET_BLK), F32),
               ((4, RET_BLK, RET_DV), F32), ((2, RET_DK, RET_DV), F32),
               ((RET_DK, RET_DV), F32), ((RET_DK, RET_DV), F32)]
    return pl.pallas_call(
        _ret_body,
        grid=(RET_HEADS, BATCH),
        in_specs=[pl.BlockSpec(memory_space=pltpu.SMEM),
                  pl.BlockSpec(qk_blk, lambda h, b: (b, qb + h)),
                  pl.BlockSpec(qk_blk, lambda h, b: (b, kb + h)),
                  pl.BlockSpec(v_blk, lambda h, b: (b, vb + h)),
                  pl.BlockSpec(v_blk, lambda h, b: (b, gb + h)),
                  pl.BlockSpec(qk_blk, lambda h, b: (0, 0)),
                  pl.BlockSpec(qk_blk, lambda h, b: (0, 0))],
        out_specs=pl.BlockSpec(v_blk, lambda h, b: (b, h)),
        out_shape=jax.ShapeDtypeStruct((M_TOK, RET_V_W), BF16),
        scratch_shapes=[pltpu.VMEM(s, d) for s, d in scratch],
        compiler_params=_params(
            ("parallel", "parallel"),
            [(qk_blk, BF16)] * 2 + [(v_blk, BF16)] * 3 + [(qk_blk, F32)] * 2,
            scratch + [((RET_BLK, RET_DV), F32)] * 4),
        name="retention",
    )(decay, u, u, u, u, cos2, sin2)


def _postmix_body(oa_ref, ob_ref, oc_ref, wb_ref, sa_ref, sb_ref, sc_ref, wo_ref, x_ref,
                  g_ref, wq_ref, kv_ref, wco_ref, o_ref):
    def branch(o, row0, s):
        return _sigmoid(s[...].astype(F32)) * _dot(o[...], wb_ref[row0:row0 + o.shape[1], :])

    merged = branch(oa_ref, 0, sa_ref) + branch(ob_ref, NA_W, sb_ref) + branch(oc_ref, NA_W + RET_V_W, sc_ref)
    x1 = x_ref[...] + _dot(merged.astype(BF16), wo_ref[...])
    q = _dot(_rms(x1, g_ref[...]).astype(BF16), wq_ref[...]).astype(BF16)
    heads = []
    for h in range(CROSS_HEADS):
        lo = h * HEAD_DIM
        k = kv_ref[:, lo:lo + HEAD_DIM]
        v = kv_ref[:, CROSS_W + lo:CROSS_W + lo + HEAD_DIM]
        s = _dot_nt(q[:, lo:lo + HEAD_DIM], k) * QK_SCALE2
        m = jnp.max(s, axis=-1, keepdims=True)
        p = jnp.exp2(s - m)
        l = jnp.sum(p, axis=-1, keepdims=True)
        heads.append((_dot(p.astype(BF16), v) * (1.0 / l)).astype(BF16))
    o_ref[...] = x1 + _dot(jnp.concatenate(heads, axis=-1), wco_ref[...])


def _postmix(o_a, o_b, o_c, w_branch, u, w_out, x, g, w_cq, kv, w_co, layer, *, tm):
    mix_w = NA_W + RET_V_W + DIL_W
    per_batch = SEQ // tm
    once = pl.Buffered(1)

    def rows(k):
        return pl.BlockSpec((tm, k), lambda i: (i, 0))

    def gate(off):
        return pl.BlockSpec((pl.Element(tm), pl.Element(D_MODEL)), lambda i: (i * tm, off))

    def weight(r, c):
        return pl.BlockSpec((None, r, c), lambda i: (layer, 0, 0), pipeline_mode=once)

    return pl.pallas_call(
        _postmix_body,
        grid=(M_TOK // tm,),
        in_specs=[rows(NA_W), rows(RET_V_W), rows(DIL_W), weight(mix_w, D_MODEL),
                  gate(OFF_SA), gate(OFF_SB), gate(OFF_SC), weight(D_MODEL, D_MODEL), rows(D_MODEL),
                  pl.BlockSpec((1, D_MODEL), lambda i: (0, 0), pipeline_mode=once),
                  weight(D_MODEL, CROSS_W),
                  pl.BlockSpec((MEM_LEN, 2 * CROSS_W), lambda i: (i // per_batch, 0)),
                  weight(CROSS_W, D_MODEL)],
        out_specs=rows(D_MODEL),
        out_shape=jax.ShapeDtypeStruct((M_TOK, D_MODEL), F32),
        compiler_params=_params(
            ("parallel",),
            [((tm, mix_w), BF16)] + [((tm, D_MODEL), BF16)] * 3 + [((tm, D_MODEL), F32)] * 2
            + [((MEM_LEN, 2 * CROSS_W), BF16)],
            [((mix_w + D_MODEL + 2 * CROSS_W, D_MODEL), BF16)] + [((tm, D_MODEL), F32)] * 6),
        name="post_mix",
    )(o_a, o_b, o_c, w_branch, u, u, u, w_out, x, g[layer].reshape(1, D_MODEL), w_cq, kv, w_co)


def _mlp_body(x_ref, g_ref, w1_ref, w2_ref, o_ref, xn_ref):
    @pl.when(pl.program_id(1) == 0)
    def _():
        x = x_ref[...]
        xn_ref[...] = _rms(x, g_ref[...]).astype(BF16)
        o_ref[...] = x

    hid = jnp.square(jnp.maximum(_dot(xn_ref[...], w1_ref[...]), 0.0)).astype(BF16)
    o_ref[...] += _dot(hid, w2_ref[...])


def _mlp(x, g, w1, w2, layer, *, tm, tf):
    m, k = x.shape
    return pl.pallas_call(
        _mlp_body,
        grid=(m // tm, D_FF // tf),
        in_specs=[pl.BlockSpec((tm, k), lambda i, f: (i, 0)),
                  pl.BlockSpec((1, k), lambda i, f: (0, 0)),
                  pl.BlockSpec((None, k, tf), lambda i, f: (layer, 0, f)),
                  pl.BlockSpec((None, tf, k), lambda i, f: (layer, f, 0))],
        out_specs=pl.BlockSpec((tm, k), lambda i, f: (i, 0)),
        out_shape=jax.ShapeDtypeStruct((m, k), F32),
        scratch_shapes=[pltpu.VMEM((tm, k), BF16)],
        compiler_params=_params(
            ("parallel", "arbitrary"),
            [((tm, k), F32), ((1, k), F32), ((k, tf), BF16), ((tf, k), BF16), ((tm, k), F32)],
            [((tm, k), BF16), ((tm, tf), F32), ((tm, tf), BF16), ((tm, k), F32)]),
        name="mlp",
    )(x, g[layer].reshape(1, k), w1, w2)


def _final_norm_body(x_ref, g_ref, o_ref):
    o_ref[...] = _rms(x_ref[...], g_ref[...])


def _final_norm(x, g, *, tm):
    return pl.pallas_call(
        _final_norm_body,
        grid=(M_TOK // tm,),
        in_specs=[pl.BlockSpec((tm, D_MODEL), lambda i: (i, 0)),
                  pl.BlockSpec((1, D_MODEL), lambda i: (0, 0))],
        out_specs=pl.BlockSpec((tm, D_MODEL), lambda i: (i, 0)),
        out_shape=jax.ShapeDtypeStruct((M_TOK, D_MODEL), F32),
        compiler_params=_params(("parallel",), [((tm, D_MODEL), F32)] * 2, [((tm, D_MODEL), F32)]),
        name="final_norm",
    )(x, g.reshape(1, D_MODEL))


def kernel(x, mem, t5_bias, norm_mix_g, w_in, na_rpb, ret_decay, w_branch, w_out, norm_cross_g, norm_mem_g,
           w_cq, w_ckv, w_co, norm_mlp_g, w_mlp1, w_mlp2, final_norm_g):
    xs = x.reshape(M_TOK, D_MODEL)
    mem2 = mem.reshape(BATCH * MEM_LEN, D_MODEL)
    w_in_h, w_branch_h, w_out_h = w_in.astype(BF16), w_branch.astype(BF16), w_out.astype(BF16)
    w_cq_h, w_ckv_h, w_co_h = w_cq.astype(BF16), w_ckv.astype(BF16), w_co.astype(BF16)
    w_mlp1_h, w_mlp2_h = w_mlp1.astype(BF16), w_mlp2.astype(BF16)
    cos2, sin2 = _rotary_tables()
    dil_tiles = _dil_bias_tiles(t5_bias)

    for layer in range(DEPTH):
        u = _normmm(xs, norm_mix_g, w_in_h, layer, tm=1024, tn=1536, name="in_proj")
        o_a = _na_attention(u, _na_bias_tiles(na_rpb[layer]))
        o_b = _retention(u, ret_decay[layer], cos2, sin2)
        o_c = _dil_attention(u, dil_tiles)
        kv = _normmm(mem2, norm_mem_g, w_ckv_h, layer, tm=BATCH * MEM_LEN, tn=512, name="mem_kv_proj")
        xs = _postmix(o_a, o_b, o_c, w_branch_h, u, w_out_h, xs, norm_cross_g, w_cq_h, kv, w_co_h, layer, tm=256)
        xs = _mlp(xs, norm_mlp_g, w_mlp1_h, w_mlp2_h, layer, tm=512, tf=1024)
    return _final_norm(xs, final_norm_g, tm=512).reshape(BATCH, SEQ, D_MODEL)
```

```python
import functools

import numpy as np
import jax
import jax.numpy as jnp
from jax import lax
from jax.experimental import pallas as pl
from jax.experimental.pallas import tpu as pltpu

D_MODEL = 2048
BATCH = 2
SEQ = 4096
DEPTH = 4
MEM_LEN = 256
HEAD_DIM = 128
GRID_W = 64
NA_HEADS = 6
NA_ROWS = 8
NA_COLS = 16
RET_HEADS = 4
RET_DK = 128
RET_DV = 256
RET_BLK = 256
ROPE_BASE = 10000.0
DIL_HEADS = 6
DIL_PAIRS = ((128, 1), (512, 4), (2048, 16))
T5_BUCKETS = 32
T5_MAX_DIST = 1024
CROSS_HEADS = 4
D_FF = 4 * D_MODEL
EPS = 1e-6
NEG = -1e30

NA_W = NA_HEADS * HEAD_DIM
RET_QK_W = RET_HEADS * RET_DK
RET_V_W = RET_HEADS * RET_DV
DIL_W = DIL_HEADS * HEAD_DIM
CROSS_W = CROSS_HEADS * HEAD_DIM
IN_SPLITS = (NA_W, NA_W, NA_W, RET_QK_W, RET_QK_W, RET_V_W, RET_V_W,
             DIL_W, DIL_W, DIL_W, D_MODEL, D_MODEL, D_MODEL)
IN_W = sum(IN_SPLITS)
(OFF_QA, OFF_KA, OFF_VA, OFF_QB, OFF_KB, OFF_VB, OFF_GB,
 OFF_QC, OFF_KC, OFF_VC, OFF_SA, OFF_SB, OFF_SC) = [int(o) for o in np.cumsum((0,) + IN_SPLITS[:-1])]

M_TOK = BATCH * SEQ
ATT_SCALE = HEAD_DIM ** -0.5
LOG2E = float(np.log2(np.e))
QK_SCALE2 = ATT_SCALE * LOG2E

V7X_VMEM_BYTES = 64 * 1024 * 1024
LANES = 128
SUBLANES = 8

NA_QROWS = 4
NA_KROWS = NA_QROWS + NA_ROWS
NA_QBLK = NA_QROWS * GRID_W
NA_KBLK = NA_KROWS * GRID_W
GRID_H = SEQ // GRID_W
NA_NBLK = GRID_H // NA_QROWS

DIL_BLK = 256
DIL_REACH = (max(w for w, _ in DIL_PAIRS) // 2 + DIL_BLK - 1) // DIL_BLK
DIL_NDELTA = 2 * DIL_REACH + 1
DIL_NBLK = SEQ // DIL_BLK

F32 = jnp.float32
BF16 = jnp.bfloat16


def _nbytes(shape, dtype):
    return int(np.prod(shape)) * jnp.dtype(dtype).itemsize


def _params(semantics, pipelined, resident):
    need = 2 * sum(_nbytes(s, d) for s, d in pipelined) + sum(_nbytes(s, d) for s, d in resident)
    assert need < V7X_VMEM_BYTES, need
    return pltpu.CompilerParams(dimension_semantics=semantics,
                                vmem_limit_bytes=min(V7X_VMEM_BYTES, need + need // 4))


def _rms(x, g):
    ms = jnp.mean(x * x, axis=-1, keepdims=True)
    return x * lax.rsqrt(ms + EPS) * g


def _sigmoid(x):
    return 1.0 / (1.0 + jnp.exp(-x))


def _dot(a, b):
    return jnp.dot(a, b, preferred_element_type=F32)


def _dot_nt(a, b):
    return lax.dot_general(a, b, (((1,), (1,)), ((), ())), preferred_element_type=F32)


def _dot_tn(a, b):
    return lax.dot_general(a, b, (((0,), (0,)), ((), ())), preferred_element_type=F32)


def _normmm_body(x_ref, g_ref, w_ref, o_ref, xn_ref):
    @pl.when(pl.program_id(1) == 0)
    def _():
        xn_ref[...] = _rms(x_ref[...], g_ref[...]).astype(BF16)

    o_ref[...] = _dot(xn_ref[...], w_ref[...]).astype(o_ref.dtype)


def _normmm(x, g, w, *, tm, tn, name):
    m, k = x.shape
    n = w.shape[-1]
    g2 = g.reshape(1, k)
    return pl.pallas_call(
        _normmm_body,
        grid=(m // tm, n // tn),
        in_specs=[pl.BlockSpec((tm, k), lambda i, j: (i, 0)),
                  pl.BlockSpec((1, k), lambda i, j: (0, 0)),
                  pl.BlockSpec((k, tn), lambda i, j: (0, j))],
        out_specs=pl.BlockSpec((tm, tn), lambda i, j: (i, j)),
        out_shape=jax.ShapeDtypeStruct((m, n), BF16),
        scratch_shapes=[pltpu.VMEM((tm, k), BF16)],
        compiler_params=_params(
            ("parallel", "arbitrary"),
            [((tm, k), F32), ((1, k), F32), ((k, tn), BF16), ((tm, tn), BF16)],
            [((tm, k), BF16), ((tm, k), F32), ((tm, tn), F32)]),
        name=name,
    )(x, g2, w)


NA_RPB_ROWS = 2 * NA_ROWS - 1
NA_RPB_COLS = 2 * NA_COLS - 1
NA_TILE_ROWS = ((0, 0), (NA_QROWS, 0), (GRID_H - NA_QROWS, GRID_H - NA_KROWS))


def _na_tiles_body(rpb_ref, o_ref):
    shape = (GRID_W, LANES)
    lane = lax.broadcasted_iota(jnp.int32, shape, 1)
    cq = lax.broadcasted_iota(jnp.int32, shape, 0)
    ck = lane & (GRID_W - 1)
    c0 = jnp.clip(cq - NA_COLS // 2, 0, GRID_W - NA_COLS)
    col_ok = (ck >= c0) & (ck < c0 + NA_COLS)
    left = lane < GRID_W
    neg = jnp.full(shape, NEG, F32)
    lo, hi = [], []
    for a in range(NA_RPB_ROWS):
        row = jnp.broadcast_to(rpb_ref[a:a + 1, :], shape)
        lo.append(pltpu.roll(row, LANES - (NA_COLS - 1), 1, stride=1, stride_axis=0))
        hi.append(pltpu.roll(row, GRID_W - (NA_COLS - 1), 1, stride=1, stride_axis=0))
    for t, (qrow0, krow0) in enumerate(NA_TILE_ROWS):
        for rq in range(NA_QROWS):
            r = qrow0 + rq
            r0 = min(max(r - NA_ROWS // 2, 0), GRID_H - NA_ROWS)
            for pair in range(NA_KROWS // 2):
                halves = []
                for side, table in ((0, lo), (1, hi)):
                    rk = krow0 + 2 * pair + side
                    halves.append(table[rk - r + NA_ROWS - 1] if r0 <= rk < r0 + NA_ROWS else neg)
                blk = jnp.where(col_ok, jnp.where(left, halves[0], halves[1]) * LOG2E, NEG)
                o_ref[t, rq * GRID_W:(rq + 1) * GRID_W, pair * LANES:(pair + 1) * LANES] = blk


def _na_bias_tiles(rpb):
    padded = jnp.pad(rpb.astype(F32), ((0, 0), (0, 2 * SUBLANES - NA_RPB_ROWS), (0, LANES - NA_RPB_COLS)))
    return pl.pallas_call(
        _na_tiles_body,
        grid=(NA_HEADS,),
        in_specs=[pl.BlockSpec((None, 2 * SUBLANES, LANES), lambda h: (h, 0, 0))],
        out_specs=pl.BlockSpec((None, 3, NA_QBLK, NA_KBLK), lambda h: (h, 0, 0, 0)),
        out_shape=jax.ShapeDtypeStruct((NA_HEADS, 3, NA_QBLK, NA_KBLK), F32),
        compiler_params=_params(("parallel",), [((3, NA_QBLK, NA_KBLK), F32)], [((NA_QBLK, NA_KBLK), F32)]),
        name="na_bias_tiles",
    )(padded)


def _na_body(q_ref, k_ref, v_ref, bias_ref, o_ref, s_ref):
    def starts(i):
        qrow = i * NA_QROWS
        krow = jnp.clip(qrow - NA_ROWS // 2, 0, GRID_H - NA_KROWS)
        return pl.multiple_of(qrow * GRID_W, NA_QBLK), pl.multiple_of(krow * GRID_W, NA_QBLK)

    def logits(i, buf):
        qs, ks = starts(i)
        tile = jnp.where(i == 0, 0, jnp.where(i == NA_NBLK - 1, 2, 1))
        s_ref[buf] = (_dot_nt(q_ref[pl.ds(qs, NA_QBLK), :], k_ref[pl.ds(ks, NA_KBLK), :]) * QK_SCALE2
                      + bias_ref[tile])

    def finish(i, buf):
        qs, ks = starts(i)
        s = s_ref[buf]
        m = jnp.max(s, axis=-1, keepdims=True)
        p = jnp.exp2(s - m)
        l = jnp.sum(p, axis=-1, keepdims=True)
        o = _dot(p.astype(BF16), v_ref[pl.ds(ks, NA_KBLK), :]) * (1.0 / l)
        o_ref[pl.ds(qs, NA_QBLK), :] = o.astype(o_ref.dtype)

    logits(0, 0)

    def pair(ii, carry):
        i = 2 * ii
        logits(i + 1, 1)
        finish(i, 0)
        logits(jnp.minimum(i + 2, NA_NBLK - 1), 0)
        finish(i + 1, 1)
        return carry

    lax.fori_loop(0, NA_NBLK // 2, pair, 0)


def _na_attention(u, bias):
    qb, kb, vb = OFF_QA // HEAD_DIM, OFF_KA // HEAD_DIM, OFF_VA // HEAD_DIM
    blk = (SEQ, HEAD_DIM)
    return pl.pallas_call(
        _na_body,
        grid=(NA_HEADS, BATCH),
        in_specs=[pl.BlockSpec(blk, lambda h, b: (b, qb + h)),
                  pl.BlockSpec(blk, lambda h, b: (b, kb + h)),
                  pl.BlockSpec(blk, lambda h, b: (b, vb + h)),
                  pl.BlockSpec((None, 3, NA_QBLK, NA_KBLK), lambda h, b: (h, 0, 0, 0))],
        out_specs=pl.BlockSpec(blk, lambda h, b: (b, h)),
        out_shape=jax.ShapeDtypeStruct((M_TOK, NA_W), BF16),
        scratch_shapes=[pltpu.VMEM((2, NA_QBLK, NA_KBLK), F32)],
        compiler_params=_params(
            ("parallel", "parallel"),
            [(blk, BF16)] * 4 + [((3, NA_QBLK, NA_KBLK), F32)],
            [((NA_QBLK, NA_KBLK), F32)] * 3),
        name="na_attention",
    )(u, u, u, bias)


def _t5_bucket(rel):
    nb = T5_BUCKETS // 2
    ret = (rel > 0).astype(np.int32) * nb
    n = np.abs(rel)
    max_exact = nb // 2
    large = max_exact + (np.log(np.maximum(n, 1) / max_exact) / np.log(T5_MAX_DIST / max_exact)
                         * (nb - max_exact)).astype(np.int32)
    large = np.minimum(large, nb - 1)
    return (ret + np.where(n < max_exact, n, large)).astype(np.int32)


def _dil_tiles_body(f_ref, o_ref):
    for d in range(DIL_NDELTA):
        row = jnp.broadcast_to(f_ref[d:d + 1, :], (DIL_BLK, 2 * DIL_BLK))
        o_ref[d] = pltpu.roll(row, DIL_BLK, 1, stride=1, stride_axis=0)[:, :DIL_BLK]
    o_ref[DIL_NDELTA] = jnp.full((DIL_BLK, DIL_BLK), NEG, F32)


def _dil_bias_tiles(t5_bias):
    delta = np.arange(-DIL_REACH, DIL_REACH + 1)[:, None]
    off = delta * DIL_BLK + np.arange(-DIL_BLK, DIL_BLK)[None, :]
    count = np.zeros(off.shape, np.int32)
    for w, d in DIL_PAIRS:
        count += ((off % d == 0) & (np.abs(off) <= w // 2)).astype(np.int32)
    bucket = _t5_bucket(np.clip(off, -T5_MAX_DIST, T5_MAX_DIST))
    bias = jnp.take(t5_bias.T.astype(F32), jnp.asarray(bucket.reshape(-1)), axis=1).reshape((DIL_HEADS,) + off.shape)
    logc = jnp.log(jnp.asarray(np.maximum(count, 1), F32))
    f = jnp.where(jnp.asarray(count > 0)[None], (bias + logc[None]) * LOG2E, NEG)
    f = jnp.pad(f, ((0, 0), (0, 2 * SUBLANES - DIL_NDELTA), (0, 0)))
    return pl.pallas_call(
        _dil_tiles_body,
        grid=(DIL_HEADS,),
        in_specs=[pl.BlockSpec((None, 2 * SUBLANES, 2 * DIL_BLK), lambda h: (h, 0, 0))],
        out_specs=pl.BlockSpec((None, DIL_NDELTA + 1, DIL_BLK, DIL_BLK), lambda h: (h, 0, 0, 0)),
        out_shape=jax.ShapeDtypeStruct((DIL_HEADS, DIL_NDELTA + 1, DIL_BLK, DIL_BLK), F32),
        compiler_params=_params(("parallel",), [((DIL_NDELTA + 1, DIL_BLK, DIL_BLK), F32)],
                                [((DIL_BLK, 2 * DIL_BLK), F32)] * 2),
        name="dil_bias_tiles",
    )(f)


def _dil_body(q_ref, k_ref, v_ref, t_ref, o_ref, s_ref):
    def key_start(i, d):
        return pl.multiple_of(jnp.clip(i + (d - DIL_REACH), 0, DIL_NBLK - 1) * DIL_BLK, DIL_BLK)

    def logits(i, buf):
        q = q_ref[pl.ds(pl.multiple_of(i * DIL_BLK, DIL_BLK), DIL_BLK), :]
        for d in range(DIL_NDELTA):
            j = i + (d - DIL_REACH)
            tile = jnp.where((j >= 0) & (j < DIL_NBLK), d, DIL_NDELTA)
            s_ref[buf, :, d * DIL_BLK:(d + 1) * DIL_BLK] = (
                _dot_nt(q, k_ref[pl.ds(key_start(i, d), DIL_BLK), :]) * QK_SCALE2 + t_ref[tile])

    def finish(i, buf):
        m = jnp.max(s_ref[buf], axis=-1, keepdims=True)
        l = jnp.zeros((DIL_BLK, 1), F32)
        acc = jnp.zeros((DIL_BLK, HEAD_DIM), F32)
        for d in range(DIL_NDELTA):
            p = jnp.exp2(s_ref[buf, :, d * DIL_BLK:(d + 1) * DIL_BLK] - m)
            l = l + jnp.sum(p, axis=-1, keepdims=True)
            acc = acc + _dot(p.astype(BF16), v_ref[pl.ds(key_start(i, d), DIL_BLK), :])
        o_ref[pl.ds(pl.multiple_of(i * DIL_BLK, DIL_BLK), DIL_BLK), :] = (acc * (1.0 / l)).astype(o_ref.dtype)

    logits(0, 0)

    def pair(ii, carry):
        i = 2 * ii
        logits(i + 1, 1)
        finish(i, 0)
        logits(jnp.minimum(i + 2, DIL_NBLK - 1), 0)
        finish(i + 1, 1)
        return carry

    lax.fori_loop(0, DIL_NBLK // 2, pair, 0)


def _dil_attention(u, tiles):
    qb, kb, vb = OFF_QC // HEAD_DIM, OFF_KC // HEAD_DIM, OFF_VC // HEAD_DIM
    blk = (SEQ, HEAD_DIM)
    return pl.pallas_call(
        _dil_body,
        grid=(DIL_HEADS, BATCH),
        in_specs=[pl.BlockSpec(blk, lambda h, b: (b, qb + h)),
                  pl.BlockSpec(blk, lambda h, b: (b, kb + h)),
                  pl.BlockSpec(blk, lambda h, b: (b, vb + h)),
                  pl.BlockSpec((None, DIL_NDELTA + 1, DIL_BLK, DIL_BLK), lambda h, b: (h, 0, 0, 0))],
        out_specs=pl.BlockSpec(blk, lambda h, b: (b, h)),
        out_shape=jax.ShapeDtypeStruct((M_TOK, DIL_W), BF16),
        scratch_shapes=[pltpu.VMEM((2, DIL_BLK, DIL_NDELTA * DIL_BLK), F32)],
        compiler_params=_params(
            ("parallel", "parallel"),
            [(blk, BF16)] * 4 + [((DIL_NDELTA + 1, DIL_BLK, DIL_BLK), F32)],
            [((DIL_BLK, DIL_NDELTA * DIL_BLK), F32)] * 3),
        name="dil_attention",
    )(u, u, u, tiles)


def _rotary_tables():
    inv_freq = jnp.asarray((ROPE_BASE ** (-np.arange(0, RET_DK, 2, dtype=np.float32) / RET_DK)).astype(np.float32))
    ang = jnp.arange(SEQ, dtype=F32)[:, None] * inv_freq[None, :]
    cos, sin = jnp.cos(ang), jnp.sin(ang)
    return jnp.concatenate([cos, cos], axis=-1), jnp.concatenate([-sin, sin], axis=-1)


def _ret_tables(dec_ref, h, d_ref, xz_ref, gc_ref):
    c = RET_BLK
    row = lax.broadcasted_iota(jnp.int32, (c, c), 0)
    col = lax.broadcasted_iota(jnp.int32, (c, c), 1)
    diff = (row - col).astype(F32)
    log_f = -jnp.exp(jnp.full((c, c), dec_ref[0, h], F32))
    log_b = -jnp.exp(jnp.full((c, c), dec_ref[1, h], F32))
    d_ref[...] = jnp.where(diff >= 0.0, jnp.exp(jnp.maximum(diff, 0.0) * log_f),
                           jnp.exp(jnp.maximum(-diff, 0.0) * log_b))
    j = lax.broadcasted_iota(jnp.int32, (c, RET_DV), 0).astype(F32)
    log_f2 = -jnp.exp(jnp.full((c, RET_DV), dec_ref[0, h], F32))
    log_b2 = -jnp.exp(jnp.full((c, RET_DV), dec_ref[1, h], F32))
    xz_ref[0] = jnp.exp((j + 1.0) * log_f2)
    xz_ref[1] = jnp.exp((c - 1.0 - j) * log_f2)
    xz_ref[2] = jnp.exp((c - j) * log_b2)
    xz_ref[3] = jnp.exp(j * log_b2)
    gc_ref[0] = jnp.exp(c * -jnp.exp(jnp.full((RET_DK, RET_DV), dec_ref[0, h], F32)))
    gc_ref[1] = jnp.exp(c * -jnp.exp(jnp.full((RET_DK, RET_DV), dec_ref[1, h], F32)))


def _ret_body(dec_ref, q_ref, k_ref, v_ref, gate_ref, cos_ref, sin_ref, o_ref,
              qr_ref, kr_ref, y_ref, d_ref, xz_ref, gc_ref, sf_ref, sb_ref):
    c = RET_BLK
    nblk = SEQ // c
    _ret_tables(dec_ref, pl.program_id(0), d_ref, xz_ref, gc_ref)
    sf_ref[...] = jnp.zeros_like(sf_ref)
    sb_ref[...] = jnp.zeros_like(sb_ref)

    def rotate(x, cs, sn):
        return x * cs + pltpu.roll(x, RET_DK // 2, 1) * sn

    def up(i, carry):
        sl = pl.ds(pl.multiple_of(i * c, c), c)
        cs, sn = cos_ref[sl, :], sin_ref[sl, :]
        q = rotate(q_ref[sl, :].astype(F32), cs, sn).astype(BF16)
        k = (rotate(k_ref[sl, :].astype(F32), cs, sn) * (RET_DK ** -0.5)).astype(BF16)
        qr_ref[sl, :] = q
        kr_ref[sl, :] = k
        v = v_ref[sl, :]
        inner = (_dot_nt(q, k) * d_ref[...]).astype(BF16)
        state = sf_ref[...]
        y_ref[sl, :] = _dot(inner, v) + _dot(q, state.astype(BF16)) * xz_ref[0]
        sf_ref[...] = state * gc_ref[0] + _dot_tn(k, (v.astype(F32) * xz_ref[1]).astype(BF16))
        return carry

    lax.fori_loop(0, nblk, up, 0, unroll=4)

    def down(ii, carry):
        sl = pl.ds(pl.multiple_of((nblk - 1 - ii) * c, c), c)
        q, k, v = qr_ref[sl, :], kr_ref[sl, :], v_ref[sl, :]
        state = sb_ref[...]
        y = y_ref[sl, :] + _dot(q, state.astype(BF16)) * xz_ref[2]
        sb_ref[...] = state * gc_ref[1] + _dot_tn(k, (v.astype(F32) * xz_ref[3]).astype(BF16))
        mu = jnp.mean(y, axis=-1, keepdims=True)
        yc = y - mu
        var = jnp.mean(yc * yc, axis=-1, keepdims=True)
        gate = gate_ref[sl, :].astype(F32)
        o_ref[sl, :] = (gate * _sigmoid(gate) * (yc * lax.rsqrt(var + EPS))).astype(o_ref.dtype)
        return carry

    lax.fori_loop(0, nblk, down, 0, unroll=4)


def _retention(u, decay, cos2, sin2):
    qb, kb = OFF_QB // RET_DK, OFF_KB // RET_DK
    vb, gb = OFF_VB // RET_DV, OFF_GB // RET_DV
    qk_blk, v_blk = (SEQ, RET_DK), (SEQ, RET_DV)
    scratch = [(qk_blk, BF16), (qk_blk, BF16), (v_blk, F32), ((RET_BLK, RET_BLK), F32),
               ((4, RET_BLK, RET_DV), F32), ((2, RET_DK, RET_DV), F32),
               ((RET_DK, RET_DV), F32), ((RET_DK, RET_DV), F32)]
    return pl.pallas_call(
        _ret_body,
        grid=(RET_HEADS, BATCH),
        in_specs=[pl.BlockSpec(memory_space=pltpu.SMEM),
                  pl.BlockSpec(qk_blk, lambda h, b: (b, qb + h)),
                  pl.BlockSpec(qk_blk, lambda h, b: (b, kb + h)),
                  pl.BlockSpec(v_blk, lambda h, b: (b, vb + h)),
                  pl.BlockSpec(v_blk, lambda h, b: (b, gb + h)),
                  pl.BlockSpec(qk_blk, lambda h, b: (0, 0)),
                  pl.BlockSpec(qk_blk, lambda h, b: (0, 0))],
        out_specs=pl.BlockSpec(v_blk, lambda h, b: (b, h)),
        out_shape=jax.ShapeDtypeStruct((M_TOK, RET_V_W), BF16),
        scratch_shapes=[pltpu.VMEM(s, d) for s, d in scratch],
        compiler_params=_params(
            ("parallel", "parallel"),
            [(qk_blk, BF16)] * 2 + [(v_blk, BF16)] * 3 + [(qk_blk, F32)] * 2,
            scratch + [((RET_BLK, RET_DV), F32)] * 4),
        name="retention",
    )(decay, u, u, u, u, cos2, sin2)


def _postmix_body(oa_ref, ob_ref, oc_ref, wb_ref, sa_ref, sb_ref, sc_ref, wo_ref, x_ref,
                  g_ref, wq_ref, kv_ref, wco_ref, o_ref):
    def branch(o, row0, s):
        return _sigmoid(s[...].astype(F32)) * _dot(o[...], wb_ref[row0:row0 + o.shape[1], :])

    merged = branch(oa_ref, 0, sa_ref) + branch(ob_ref, NA_W, sb_ref) + branch(oc_ref, NA_W + RET_V_W, sc_ref)
    x1 = x_ref[...] + _dot(merged.astype(BF16), wo_ref[...])
    q = _dot(_rms(x1, g_ref[...]).astype(BF16), wq_ref[...]).astype(BF16)
    heads = []
    for h in range(CROSS_HEADS):
        lo = h * HEAD_DIM
        k = kv_ref[:, lo:lo + HEAD_DIM]
        v = kv_ref[:, CROSS_W + lo:CROSS_W + lo + HEAD_DIM]
        s = _dot_nt(q[:, lo:lo + HEAD_DIM], k) * QK_SCALE2
        m = jnp.max(s, axis=-1, keepdims=True)
        p = jnp.exp2(s - m)
        l = jnp.sum(p, axis=-1, keepdims=True)
        heads.append((_dot(p.astype(BF16), v) * (1.0 / l)).astype(BF16))
    o_ref[...] = x1 + _dot(jnp.concatenate(heads, axis=-1), wco_ref[...])


def _postmix(o_a, o_b, o_c, w_branch, u, w_out, x, g, w_cq, kv, w_co, layer, *, tm):
    mix_w = NA_W + RET_V_W + DIL_W
    per_batch = SEQ // tm
    once = pl.Buffered(1)

    def rows(k):
        return pl.BlockSpec((tm, k), lambda i: (i, 0))

    def gate(off):
        return pl.BlockSpec((pl.Element(tm), pl.Element(D_MODEL)), lambda i: (i * tm, off))

    def weight(r, c):
        return pl.BlockSpec((None, r, c), lambda i: (layer, 0, 0), pipeline_mode=once)

    return pl.pallas_call(
        _postmix_body,
        grid=(M_TOK // tm,),
        in_specs=[rows(NA_W), rows(RET_V_W), rows(DIL_W), weight(mix_w, D_MODEL),
                  gate(OFF_SA), gate(OFF_SB), gate(OFF_SC), weight(D_MODEL, D_MODEL), rows(D_MODEL),
                  pl.BlockSpec((1, D_MODEL), lambda i: (0, 0), pipeline_mode=once),
                  weight(D_MODEL, CROSS_W),
                  pl.BlockSpec((MEM_LEN, 2 * CROSS_W), lambda i: (i // per_batch, 0)),
                  weight(CROSS_W, D_MODEL)],
        out_specs=rows(D_MODEL),
        out_shape=jax.ShapeDtypeStruct((M_TOK, D_MODEL), F32),
        compiler_params=_params(
            ("parallel",),
            [((tm, mix_w), BF16)] + [((tm, D_MODEL), BF16)] * 3 + [((tm, D_MODEL), F32)] * 2
            + [((MEM_LEN, 2 * CROSS_W), BF16)],
            [((mix_w + D_MODEL + 2 * CROSS_W, D_MODEL), BF16)] + [((tm, D_MODEL), F32)] * 6),
        name="post_mix",
    )(o_a, o_b, o_c, w_branch, u, u, u, w_out, x, g[layer].reshape(1, D_MODEL), w_cq, kv, w_co)


def _cast_chunk(step, nsteps, next_layer, srcs, dsts, stages, halves, sem_in, sem_out):
    slot = step % 2

    def in_copy(k, s, sl):
        rows = srcs[k].shape[1] // nsteps
        return pltpu.make_async_copy(srcs[k].at[next_layer, pl.ds(pl.multiple_of(s * rows, rows), rows), :],
                                     stages[k].at[sl], sem_in.at[k, sl])

    def out_copy(k, s, sl):
        rows = dsts[k].shape[0] // nsteps
        return pltpu.make_async_copy(halves[k].at[sl], dsts[k].at[pl.ds(pl.multiple_of(s * rows, rows), rows), :],
                                     sem_out.at[k, sl])

    every = range(len(srcs))

    @pl.when(step == 0)
    def _():
        for k in every:
            in_copy(k, 0, 0).start()

    @pl.when(step + 1 < nsteps)
    def _():
        for k in every:
            in_copy(k, step + 1, 1 - slot).start()

    for k in every:
        in_copy(k, step, slot).wait()

    @pl.when(step >= 2)
    def _():
        for k in every:
            out_copy(k, step - 2, slot).wait()

    for k in every:
        halves[k][slot] = stages[k][slot].astype(BF16)
        out_copy(k, step, slot).start()

    @pl.when(step == nsteps - 1)
    def _():
        for k in every:
            out_copy(k, step - 1, 1 - slot).wait()
            out_copy(k, step, slot).wait()


def _mlp_body(*refs, next_layer):
    x_ref, g_ref, w1_ref, w2_ref = refs[:4]
    if next_layer is None:
        o_ref, xn_ref = refs[4:]
    else:
        srcs, o_ref, dsts, xn_ref = refs[4:7], refs[7], refs[8:11], refs[11]
        stages, halves, sem_in, sem_out = refs[12:15], refs[15:18], refs[18], refs[19]
        nsteps = pl.num_programs(0) * pl.num_programs(1)
        step = pl.program_id(0) * pl.num_programs(1) + pl.program_id(1)
        _cast_chunk(step, nsteps, next_layer, srcs, dsts, stages, halves, sem_in, sem_out)

    @pl.when(pl.program_id(1) == 0)
    def _():
        x = x_ref[...]
        xn_ref[...] = _rms(x, g_ref[...]).astype(BF16)
        o_ref[...] = x

    hid = jnp.square(jnp.maximum(_dot(xn_ref[...], w1_ref[...]), 0.0)).astype(BF16)
    o_ref[...] += _dot(hid, w2_ref[...])


def _mlp(x, g, w1, w2, *, tm, tf, cast_next=None):
    m, k = x.shape
    nsteps = (m // tm) * (D_FF // tf)
    in_specs = [pl.BlockSpec((tm, k), lambda i, f: (i, 0)),
                pl.BlockSpec((1, k), lambda i, f: (0, 0)),
                pl.BlockSpec((k, tf), lambda i, f: (0, f)),
                pl.BlockSpec((tf, k), lambda i, f: (f, 0))]
    out_specs = [pl.BlockSpec((tm, k), lambda i, f: (i, 0))]
    out_shape = [jax.ShapeDtypeStruct((m, k), F32)]
    scratch = [((tm, k), BF16)]
    sems = []
    operands = [x, g.reshape(1, k), w1, w2]
    next_layer = None
    if cast_next is not None:
        next_layer, srcs = cast_next
        hbm = pl.BlockSpec(memory_space=pltpu.HBM)
        in_specs += [hbm] * len(srcs)
        out_specs += [hbm] * len(srcs)
        out_shape += [jax.ShapeDtypeStruct(w.shape[1:], BF16) for w in srcs]
        chunks = [(w.shape[1] // nsteps, w.shape[2]) for w in srcs]
        assert all(w.shape[1] % nsteps == 0 and r % (2 * SUBLANES) == 0 for w, (r, _) in zip(srcs, chunks))
        scratch += [((2,) + c, F32) for c in chunks] + [((2,) + c, BF16) for c in chunks]
        sems = [pltpu.SemaphoreType.DMA((len(srcs), 2))] * 2
        operands += list(srcs)
    out = pl.pallas_call(
        functools.partial(_mlp_body, next_layer=next_layer),
        grid=(m // tm, D_FF // tf),
        in_specs=in_specs,
        out_specs=out_specs,
        out_shape=out_shape,
        scratch_shapes=[pltpu.VMEM(s, d) for s, d in scratch] + sems,
        compiler_params=_params(
            ("arbitrary", "arbitrary"),
            [((tm, k), F32), ((1, k), F32), ((k, tf), BF16), ((tf, k), BF16), ((tm, k), F32)],
            scratch + [((tm, tf), F32), ((tm, tf), BF16), ((tm, k), F32)]),
        name="mlp",
    )(*operands)
    return out[0], out[1:]


def _final_norm_body(x_ref, g_ref, o_ref):
    o_ref[...] = _rms(x_ref[...], g_ref[...])


def _final_norm(x, g, *, tm):
    return pl.pallas_call(
        _final_norm_body,
        grid=(M_TOK // tm,),
        in_specs=[pl.BlockSpec((tm, D_MODEL), lambda i: (i, 0)),
                  pl.BlockSpec((1, D_MODEL), lambda i: (0, 0))],
        out_specs=pl.BlockSpec((tm, D_MODEL), lambda i: (i, 0)),
        out_shape=jax.ShapeDtypeStruct((M_TOK, D_MODEL), F32),
        compiler_params=_params(("parallel",), [((tm, D_MODEL), F32)] * 2, [((tm, D_MODEL), F32)]),
        name="final_norm",
    )(x, g.reshape(1, D_MODEL))


def kernel(x, mem, t5_bias, norm_mix_g, w_in, na_rpb, ret_decay, w_branch, w_out, norm_cross_g, norm_mem_g,
           w_cq, w_ckv, w_co, norm_mlp_g, w_mlp1, w_mlp2, final_norm_g):
    xs = x.reshape(M_TOK, D_MODEL)
    mem2 = mem.reshape(BATCH * MEM_LEN, D_MODEL)
    w_branch_h, w_out_h = w_branch.astype(BF16), w_out.astype(BF16)
    w_cq_h, w_ckv_h, w_co_h = w_cq.astype(BF16), w_ckv.astype(BF16), w_co.astype(BF16)
    big = [w_in[0].astype(BF16), w_mlp1[0].astype(BF16), w_mlp2[0].astype(BF16)]
    cos2, sin2 = _rotary_tables()
    dil_tiles = _dil_bias_tiles(t5_bias)

    for layer in range(DEPTH):
        w_in_h, w_mlp1_h, w_mlp2_h = big
        u = _normmm(xs, norm_mix_g[layer], w_in_h, tm=1024, tn=1536, name="in_proj")
        o_a = _na_attention(u, _na_bias_tiles(na_rpb[layer]))
        o_b = _retention(u, ret_decay[layer], cos2, sin2)
        o_c = _dil_attention(u, dil_tiles)
        kv = _normmm(mem2, norm_mem_g[layer], w_ckv_h[layer], tm=BATCH * MEM_LEN, tn=512, name="mem_kv_proj")
        xs = _postmix(o_a, o_b, o_c, w_branch_h, u, w_out_h, xs, norm_cross_g, w_cq_h, kv, w_co_h, layer, tm=256)
        cast_next = (layer + 1, [w_in, w_mlp1, w_mlp2]) if layer + 1 < DEPTH else None
        xs, big = _mlp(xs, norm_mlp_g[layer], w_mlp1_h, w_mlp2_h, tm=512, tf=1024, cast_next=cast_next)
    return _final_norm(xs, final_norm_g, tm=512).reshape(BATCH, SEQ, D_MODEL)
```

```python
import functools

import numpy as np
import jax
import jax.numpy as jnp
from jax import lax
from jax.experimental import pallas as pl
from jax.experimental.pallas import tpu as pltpu

D_MODEL = 2048
BATCH = 2
SEQ = 4096
DEPTH = 4
MEM_LEN = 256
HEAD_DIM = 128
GRID_W = 64
NA_HEADS = 6
NA_ROWS = 8
NA_COLS = 16
RET_HEADS = 4
RET_DK = 128
RET_DV = 256
RET_BLK = 256
ROPE_BASE = 10000.0
DIL_HEADS = 6
DIL_PAIRS = ((128, 1), (512, 4), (2048, 16))
T5_BUCKETS = 32
T5_MAX_DIST = 1024
CROSS_HEADS = 4
D_FF = 4 * D_MODEL
EPS = 1e-6
NEG = -1e30

NA_W = NA_HEADS * HEAD_DIM
RET_QK_W = RET_HEADS * RET_DK
RET_V_W = RET_HEADS * RET_DV
DIL_W = DIL_HEADS * HEAD_DIM
CROSS_W = CROSS_HEADS * HEAD_DIM
IN_SPLITS = (NA_W, NA_W, NA_W, RET_QK_W, RET_QK_W, RET_V_W, RET_V_W,
             DIL_W, DIL_W, DIL_W, D_MODEL, D_MODEL, D_MODEL)
IN_W = sum(IN_SPLITS)
(OFF_QA, OFF_KA, OFF_VA, OFF_QB, OFF_KB, OFF_VB, OFF_GB,
 OFF_QC, OFF_KC, OFF_VC, OFF_SA, OFF_SB, OFF_SC) = [int(o) for o in np.cumsum((0,) + IN_SPLITS[:-1])]

M_TOK = BATCH * SEQ
ATT_SCALE = HEAD_DIM ** -0.5
LOG2E = float(np.log2(np.e))
QK_SCALE2 = ATT_SCALE * LOG2E

V7X_VMEM_BYTES = 64 * 1024 * 1024
LANES = 128
SUBLANES = 8

NA_QROWS = 4
NA_KROWS = NA_QROWS + NA_ROWS
NA_QBLK = NA_QROWS * GRID_W
NA_KBLK = NA_KROWS * GRID_W
GRID_H = SEQ // GRID_W
NA_NBLK = GRID_H // NA_QROWS

DIL_BLK = 256
DIL_REACH = (max(w for w, _ in DIL_PAIRS) // 2 + DIL_BLK - 1) // DIL_BLK
DIL_NDELTA = 2 * DIL_REACH + 1
DIL_NBLK = SEQ // DIL_BLK

F32 = jnp.float32
BF16 = jnp.bfloat16


def _nbytes(shape, dtype):
    return int(np.prod(shape)) * jnp.dtype(dtype).itemsize


def _params(semantics, pipelined, resident):
    need = 2 * sum(_nbytes(s, d) for s, d in pipelined) + sum(_nbytes(s, d) for s, d in resident)
    assert need < V7X_VMEM_BYTES, need
    return pltpu.CompilerParams(dimension_semantics=semantics,
                                vmem_limit_bytes=min(V7X_VMEM_BYTES, need + need // 4))


def _rms(x, g):
    ms = jnp.mean(x * x, axis=-1, keepdims=True)
    return x * lax.rsqrt(ms + EPS) * g


def _sigmoid(x):
    return 1.0 / (1.0 + jnp.exp(-x))


def _dot(a, b):
    return jnp.dot(a, b, preferred_element_type=F32)


def _dot_nt(a, b):
    return lax.dot_general(a, b, (((1,), (1,)), ((), ())), preferred_element_type=F32)


def _dot_tn(a, b):
    return lax.dot_general(a, b, (((0,), (0,)), ((), ())), preferred_element_type=F32)


def _cast_rows(rows, nsteps):
    r = 2 * SUBLANES
    while rows // r > nsteps or rows % r:
        r += 2 * SUBLANES
    return r


def _cast_chunk(step, nsteps, layer, srcs, dsts, stages, halves, sem_in, sem_out):
    slot = step % 2

    for k, (src, dst) in enumerate(zip(srcs, dsts)):
        rows = stages[k].shape[1]
        nchunk = dst.shape[0] // rows

        def in_copy(s, sl, k=k, src=src, rows=rows):
            return pltpu.make_async_copy(src.at[layer, pl.ds(pl.multiple_of(s * rows, rows), rows), :],
                                         stages[k].at[sl], sem_in.at[k, sl])

        def out_copy(s, sl, k=k, dst=dst, rows=rows):
            return pltpu.make_async_copy(halves[k].at[sl], dst.at[pl.ds(pl.multiple_of(s * rows, rows), rows), :],
                                         sem_out.at[k, sl])

        @pl.when(step == 0)
        def _():
            in_copy(0, 0).start()

        @pl.when(step + 1 < nchunk)
        def _():
            in_copy(step + 1, 1 - slot).start()

        @pl.when((step >= 2) & (step < nchunk + 2))
        def _():
            out_copy(step - 2, slot).wait()

        @pl.when(step < nchunk)
        def _():
            in_copy(step, slot).wait()
            halves[k][slot] = stages[k][slot].astype(BF16)
            out_copy(step, slot).start()

        for late in range(max(nchunk - 2, 0), nchunk):
            if late + 2 > nsteps - 1:
                @pl.when(step == nsteps - 1)
                def _():
                    out_copy(late, late % 2).wait()


def _call_with_cast(body, *, grid, in_specs, out_spec, out_shape, scratch, pipelined, temporaries, operands, name,
                    cast=None):
    layer, srcs = cast if cast is not None else (None, [])
    n, n_in, n_scr = len(srcs), len(in_specs), len(scratch)
    nsteps = grid[0] * grid[1]
    chunks = [(_cast_rows(w.shape[1], nsteps), w.shape[2]) for w in srcs]
    cast_scratch = [((2,) + c, F32) for c in chunks] + [((2,) + c, BF16) for c in chunks]

    def wrapped(*refs):
        ins, srcs_r, out = refs[:n_in], refs[n_in:n_in + n], refs[n_in + n]
        dsts = refs[n_in + n + 1:n_in + 2 * n + 1]
        rest = refs[n_in + 2 * n + 1:]
        if n:
            stages, halves, (sem_in, sem_out) = rest[n_scr:n_scr + n], rest[n_scr + n:n_scr + 2 * n], rest[-2:]
            step = pl.program_id(0) * grid[1] + pl.program_id(1)
            _cast_chunk(step, nsteps, layer, srcs_r, dsts, stages, halves, sem_in, sem_out)
        body(*ins, out, *rest[:n_scr])

    hbm = pl.BlockSpec(memory_space=pltpu.HBM)
    out = pl.pallas_call(
        wrapped,
        grid=grid,
        in_specs=list(in_specs) + [hbm] * n,
        out_specs=[out_spec] + [hbm] * n,
        out_shape=[out_shape] + [jax.ShapeDtypeStruct(w.shape[1:], BF16) for w in srcs],
        scratch_shapes=[pltpu.VMEM(sh, d) for sh, d in scratch + cast_scratch]
        + ([pltpu.SemaphoreType.DMA((n, 2))] * 2 if n else []),
        compiler_params=_params(("arbitrary", "arbitrary") if n else ("parallel", "arbitrary"),
                                pipelined, scratch + cast_scratch + temporaries),
        name=name,
    )(*operands, *srcs)
    return out[0], out[1:]


def _normmm_body(x_ref, g_ref, w_ref, o_ref, xn_ref):
    @pl.when(pl.program_id(1) == 0)
    def _():
        xn_ref[...] = _rms(x_ref[...], g_ref[...]).astype(BF16)

    o_ref[...] = _dot(xn_ref[...], w_ref[...]).astype(o_ref.dtype)


def _normmm(x, g, w, *, tm, tn, name, cast=None):
    m, k = x.shape
    n = w.shape[-1]
    return _call_with_cast(
        _normmm_body,
        grid=(m // tm, n // tn),
        in_specs=[pl.BlockSpec((tm, k), lambda i, j: (i, 0)),
                  pl.BlockSpec((1, k), lambda i, j: (0, 0)),
                  pl.BlockSpec((k, tn), lambda i, j: (0, j))],
        out_spec=pl.BlockSpec((tm, tn), lambda i, j: (i, j)),
        out_shape=jax.ShapeDtypeStruct((m, n), BF16),
        scratch=[((tm, k), BF16)],
        pipelined=[((tm, k), F32), ((1, k), F32), ((k, tn), BF16), ((tm, tn), BF16)],
        temporaries=[((tm, k), F32), ((tm, tn), F32)],
        operands=[x, g.reshape(1, k), w], name=name, cast=cast)


NA_RPB_ROWS = 2 * NA_ROWS - 1
NA_RPB_COLS = 2 * NA_COLS - 1
NA_TILE_ROWS = ((0, 0), (NA_QROWS, 0), (GRID_H - NA_QROWS, GRID_H - NA_KROWS))


def _na_tiles_body(rpb_ref, o_ref):
    shape = (GRID_W, LANES)
    lane = lax.broadcasted_iota(jnp.int32, shape, 1)
    cq = lax.broadcasted_iota(jnp.int32, shape, 0)
    ck = lane & (GRID_W - 1)
    c0 = jnp.clip(cq - NA_COLS // 2, 0, GRID_W - NA_COLS)
    col_ok = (ck >= c0) & (ck < c0 + NA_COLS)
    left = lane < GRID_W
    neg = jnp.full(shape, NEG, F32)
    lo, hi = [], []
    for a in range(NA_RPB_ROWS):
        row = jnp.broadcast_to(rpb_ref[a:a + 1, :], shape)
        lo.append(pltpu.roll(row, LANES - (NA_COLS - 1), 1, stride=1, stride_axis=0))
        hi.append(pltpu.roll(row, GRID_W - (NA_COLS - 1), 1, stride=1, stride_axis=0))
    for t, (qrow0, krow0) in enumerate(NA_TILE_ROWS):
        for rq in range(NA_QROWS):
            r = qrow0 + rq
            r0 = min(max(r - NA_ROWS // 2, 0), GRID_H - NA_ROWS)
            for pair in range(NA_KROWS // 2):
                halves = []
                for side, table in ((0, lo), (1, hi)):
                    rk = krow0 + 2 * pair + side
                    halves.append(table[rk - r + NA_ROWS - 1] if r0 <= rk < r0 + NA_ROWS else neg)
                blk = jnp.where(col_ok, jnp.where(left, halves[0], halves[1]) * LOG2E, NEG)
                o_ref[t, rq * GRID_W:(rq + 1) * GRID_W, pair * LANES:(pair + 1) * LANES] = blk


def _na_bias_tiles(rpb):
    padded = jnp.pad(rpb.astype(F32), ((0, 0), (0, 2 * SUBLANES - NA_RPB_ROWS), (0, LANES - NA_RPB_COLS)))
    return pl.pallas_call(
        _na_tiles_body,
        grid=(NA_HEADS,),
        in_specs=[pl.BlockSpec((None, 2 * SUBLANES, LANES), lambda h: (h, 0, 0))],
        out_specs=pl.BlockSpec((None, 3, NA_QBLK, NA_KBLK), lambda h: (h, 0, 0, 0)),
        out_shape=jax.ShapeDtypeStruct((NA_HEADS, 3, NA_QBLK, NA_KBLK), F32),
        compiler_params=_params(("parallel",), [((3, NA_QBLK, NA_KBLK), F32)], [((NA_QBLK, NA_KBLK), F32)]),
        name="na_bias_tiles",
    )(padded)


def _na_body(q_ref, k_ref, v_ref, bias_ref, o_ref, s_ref):
    def starts(i):
        qrow = i * NA_QROWS
        krow = jnp.clip(qrow - NA_ROWS // 2, 0, GRID_H - NA_KROWS)
        return pl.multiple_of(qrow * GRID_W, NA_QBLK), pl.multiple_of(krow * GRID_W, NA_QBLK)

    def logits(i, buf):
        qs, ks = starts(i)
        tile = jnp.where(i == 0, 0, jnp.where(i == NA_NBLK - 1, 2, 1))
        s_ref[buf] = (_dot_nt(q_ref[pl.ds(qs, NA_QBLK), :], k_ref[pl.ds(ks, NA_KBLK), :]) * QK_SCALE2
                      + bias_ref[tile])

    def finish(i, buf):
        qs, ks = starts(i)
        s = s_ref[buf]
        m = jnp.max(s, axis=-1, keepdims=True)
        p = jnp.exp2(s - m)
        l = jnp.sum(p, axis=-1, keepdims=True)
        o = _dot(p.astype(BF16), v_ref[pl.ds(ks, NA_KBLK), :]) * (1.0 / l)
        o_ref[pl.ds(qs, NA_QBLK), :] = o.astype(o_ref.dtype)

    logits(0, 0)

    def pair(ii, carry):
        i = 2 * ii
        logits(i + 1, 1)
        finish(i, 0)
        logits(jnp.minimum(i + 2, NA_NBLK - 1), 0)
        finish(i + 1, 1)
        return carry

    lax.fori_loop(0, NA_NBLK // 2, pair, 0)


def _na_attention(u, bias):
    qb, kb, vb = OFF_QA // HEAD_DIM, OFF_KA // HEAD_DIM, OFF_VA // HEAD_DIM
    blk = (SEQ, HEAD_DIM)
    return pl.pallas_call(
        _na_body,
        grid=(NA_HEADS, BATCH),
        in_specs=[pl.BlockSpec(blk, lambda h, b: (b, qb + h)),
                  pl.BlockSpec(blk, lambda h, b: (b, kb + h)),
                  pl.BlockSpec(blk, lambda h, b: (b, vb + h)),
                  pl.BlockSpec((None, 3, NA_QBLK, NA_KBLK), lambda h, b: (h, 0, 0, 0))],
        out_specs=pl.BlockSpec(blk, lambda h, b: (b, h)),
        out_shape=jax.ShapeDtypeStruct((M_TOK, NA_W), BF16),
        scratch_shapes=[pltpu.VMEM((2, NA_QBLK, NA_KBLK), F32)],
        compiler_params=_params(
            ("parallel", "parallel"),
            [(blk, BF16)] * 4 + [((3, NA_QBLK, NA_KBLK), F32)],
            [((NA_QBLK, NA_KBLK), F32)] * 3),
        name="na_attention",
    )(u, u, u, bias)


def _t5_bucket(rel):
    nb = T5_BUCKETS // 2
    ret = (rel > 0).astype(np.int32) * nb
    n = np.abs(rel)
    max_exact = nb // 2
    large = max_exact + (np.log(np.maximum(n, 1) / max_exact) / np.log(T5_MAX_DIST / max_exact)
                         * (nb - max_exact)).astype(np.int32)
    large = np.minimum(large, nb - 1)
    return (ret + np.where(n < max_exact, n, large)).astype(np.int32)


def _dil_tiles_body(f_ref, o_ref):
    for d in range(DIL_NDELTA):
        row = jnp.broadcast_to(f_ref[d:d + 1, :], (DIL_BLK, 2 * DIL_BLK))
        o_ref[d] = pltpu.roll(row, DIL_BLK, 1, stride=1, stride_axis=0)[:, :DIL_BLK]
    o_ref[DIL_NDELTA] = jnp.full((DIL_BLK, DIL_BLK), NEG, F32)


def _dil_bias_tiles(t5_bias):
    delta = np.arange(-DIL_REACH, DIL_REACH + 1)[:, None]
    off = delta * DIL_BLK + np.arange(-DIL_BLK, DIL_BLK)[None, :]
    count = np.zeros(off.shape, np.int32)
    for w, d in DIL_PAIRS:
        count += ((off % d == 0) & (np.abs(off) <= w // 2)).astype(np.int32)
    bucket = _t5_bucket(np.clip(off, -T5_MAX_DIST, T5_MAX_DIST))
    bias = jnp.take(t5_bias.T.astype(F32), jnp.asarray(bucket.reshape(-1)), axis=1).reshape((DIL_HEADS,) + off.shape)
    logc = jnp.log(jnp.asarray(np.maximum(count, 1), F32))
    f = jnp.where(jnp.asarray(count > 0)[None], (bias + logc[None]) * LOG2E, NEG)
    f = jnp.pad(f, ((0, 0), (0, 2 * SUBLANES - DIL_NDELTA), (0, 0)))
    return pl.pallas_call(
        _dil_tiles_body,
        grid=(DIL_HEADS,),
        in_specs=[pl.BlockSpec((None, 2 * SUBLANES, 2 * DIL_BLK), lambda h: (h, 0, 0))],
        out_specs=pl.BlockSpec((None, DIL_NDELTA + 1, DIL_BLK, DIL_BLK), lambda h: (h, 0, 0, 0)),
        out_shape=jax.ShapeDtypeStruct((DIL_HEADS, DIL_NDELTA + 1, DIL_BLK, DIL_BLK), F32),
        compiler_params=_params(("parallel",), [((DIL_NDELTA + 1, DIL_BLK, DIL_BLK), F32)],
                                [((DIL_BLK, 2 * DIL_BLK), F32)] * 2),
        name="dil_bias_tiles",
    )(f)


def _dil_body(q_ref, k_ref, v_ref, t_ref, o_ref, s_ref):
    def key_start(i, d):
        return pl.multiple_of(jnp.clip(i + (d - DIL_REACH), 0, DIL_NBLK - 1) * DIL_BLK, DIL_BLK)

    def logits(i, buf):
        q = q_ref[pl.ds(pl.multiple_of(i * DIL_BLK, DIL_BLK), DIL_BLK), :]
        for d in range(DIL_NDELTA):
            j = i + (d - DIL_REACH)
            tile = jnp.where((j >= 0) & (j < DIL_NBLK), d, DIL_NDELTA)
            s_ref[buf, :, d * DIL_BLK:(d + 1) * DIL_BLK] = (
                _dot_nt(q, k_ref[pl.ds(key_start(i, d), DIL_BLK), :]) * QK_SCALE2 + t_ref[tile])

    def finish(i, buf):
        m = jnp.max(s_ref[buf], axis=-1, keepdims=True)
        l = jnp.zeros((DIL_BLK, 1), F32)
        acc = jnp.zeros((DIL_BLK, HEAD_DIM), F32)
        for d in range(DIL_NDELTA):
            p = jnp.exp2(s_ref[buf, :, d * DIL_BLK:(d + 1) * DIL_BLK] - m)
            l = l + jnp.sum(p, axis=-1, keepdims=True)
            acc = acc + _dot(p.astype(BF16), v_ref[pl.ds(key_start(i, d), DIL_BLK), :])
        o_ref[pl.ds(pl.multiple_of(i * DIL_BLK, DIL_BLK), DIL_BLK), :] = (acc * (1.0 / l)).astype(o_ref.dtype)

    logits(0, 0)

    def pair(ii, carry):
        i = 2 * ii
        logits(i + 1, 1)
        finish(i, 0)
        logits(jnp.minimum(i + 2, DIL_NBLK - 1), 0)
        finish(i + 1, 1)
        return carry

    lax.fori_loop(0, DIL_NBLK // 2, pair, 0)


def _dil_attention(u, tiles):
    qb, kb, vb = OFF_QC // HEAD_DIM, OFF_KC // HEAD_DIM, OFF_VC // HEAD_DIM
    blk = (SEQ, HEAD_DIM)
    return pl.pallas_call(
        _dil_body,
        grid=(DIL_HEADS, BATCH),
        in_specs=[pl.BlockSpec(blk, lambda h, b: (b, qb + h)),
                  pl.BlockSpec(blk, lambda h, b: (b, kb + h)),
                  pl.BlockSpec(blk, lambda h, b: (b, vb + h)),
                  pl.BlockSpec((None, DIL_NDELTA + 1, DIL_BLK, DIL_BLK), lambda h, b: (h, 0, 0, 0))],
        out_specs=pl.BlockSpec(blk, lambda h, b: (b, h)),
        out_shape=jax.ShapeDtypeStruct((M_TOK, DIL_W), BF16),
        scratch_shapes=[pltpu.VMEM((2, DIL_BLK, DIL_NDELTA * DIL_BLK), F32)],
        compiler_params=_params(
            ("parallel", "parallel"),
            [(blk, BF16)] * 4 + [((DIL_NDELTA + 1, DIL_BLK, DIL_BLK), F32)],
            [((DIL_BLK, DIL_NDELTA * DIL_BLK), F32)] * 3),
        name="dil_attention",
    )(u, u, u, tiles)


def _rotary_tables():
    inv_freq = jnp.asarray((ROPE_BASE ** (-np.arange(0, RET_DK, 2, dtype=np.float32) / RET_DK)).astype(np.float32))
    ang = jnp.arange(SEQ, dtype=F32)[:, None] * inv_freq[None, :]
    cos, sin = jnp.cos(ang), jnp.sin(ang)
    return jnp.concatenate([cos, cos], axis=-1), jnp.concatenate([-sin, sin], axis=-1)


def _ret_tables(dec_ref, h, d_ref, xz_ref, gc_ref):
    c = RET_BLK
    row = lax.broadcasted_iota(jnp.int32, (c, c), 0)
    col = lax.broadcasted_iota(jnp.int32, (c, c), 1)
    diff = (row - col).astype(F32)
    log_f = -jnp.exp(jnp.full((c, c), dec_ref[0, h], F32))
    log_b = -jnp.exp(jnp.full((c, c), dec_ref[1, h], F32))
    d_ref[...] = jnp.where(diff >= 0.0, jnp.exp(jnp.maximum(diff, 0.0) * log_f),
                           jnp.exp(jnp.maximum(-diff, 0.0) * log_b))
    j = lax.broadcasted_iota(jnp.int32, (c, RET_DV), 0).astype(F32)
    log_f2 = -jnp.exp(jnp.full((c, RET_DV), dec_ref[0, h], F32))
    log_b2 = -jnp.exp(jnp.full((c, RET_DV), dec_ref[1, h], F32))
    xz_ref[0] = jnp.exp((j + 1.0) * log_f2)
    xz_ref[1] = jnp.exp((c - 1.0 - j) * log_f2)
    xz_ref[2] = jnp.exp((c - j) * log_b2)
    xz_ref[3] = jnp.exp(j * log_b2)
    gc_ref[0] = jnp.exp(c * -jnp.exp(jnp.full((RET_DK, RET_DV), dec_ref[0, h], F32)))
    gc_ref[1] = jnp.exp(c * -jnp.exp(jnp.full((RET_DK, RET_DV), dec_ref[1, h], F32)))


def _ret_body(dec_ref, q_ref, k_ref, v_ref, gate_ref, cos_ref, sin_ref, o_ref,
              qr_ref, kr_ref, y_ref, d_ref, xz_ref, gc_ref, sf_ref, sb_ref):
    c = RET_BLK
    nblk = SEQ // c
    _ret_tables(dec_ref, pl.program_id(0), d_ref, xz_ref, gc_ref)
    sf_ref[...] = jnp.zeros_like(sf_ref)
    sb_ref[...] = jnp.zeros_like(sb_ref)

    def rotate(x, cs, sn):
        return x * cs + pltpu.roll(x, RET_DK // 2, 1) * sn

    def up(i, carry):
        sl = pl.ds(pl.multiple_of(i * c, c), c)
        cs, sn = cos_ref[sl, :], sin_ref[sl, :]
        q = rotate(q_ref[sl, :].astype(F32), cs, sn).astype(BF16)
        k = (rotate(k_ref[sl, :].astype(F32), cs, sn) * (RET_DK ** -0.5)).astype(BF16)
        qr_ref[sl, :] = q
        kr_ref[sl, :] = k
        v = v_ref[sl, :]
        inner = (_dot_nt(q, k) * d_ref[...]).astype(BF16)
        state = sf_ref[...]
        y_ref[sl, :] = _dot(inner, v) + _dot(q, state.astype(BF16)) * xz_ref[0]
        sf_ref[...] = state * gc_ref[0] + _dot_tn(k, (v.astype(F32) * xz_ref[1]).astype(BF16))
        return carry

    lax.fori_loop(0, nblk, up, 0, unroll=4)

    def down(ii, carry):
        sl = pl.ds(pl.multiple_of((nblk - 1 - ii) * c, c), c)
        q, k, v = qr_ref[sl, :], kr_ref[sl, :], v_ref[sl, :]
        state = sb_ref[...]
        y = y_ref[sl, :] + _dot(q, state.astype(BF16)) * xz_ref[2]
        sb_ref[...] = state * gc_ref[1] + _dot_tn(k, (v.astype(F32) * xz_ref[3]).astype(BF16))
        mu = jnp.mean(y, axis=-1, keepdims=True)
        yc = y - mu
        var = jnp.mean(yc * yc, axis=-1, keepdims=True)
        gate = gate_ref[sl, :].astype(F32)
        o_ref[sl, :] = (gate * _sigmoid(gate) * (yc * lax.rsqrt(var + EPS))).astype(o_ref.dtype)
        return carry

    lax.fori_loop(0, nblk, down, 0, unroll=4)


def _retention(u, decay, cos2, sin2):
    qb, kb = OFF_QB // RET_DK, OFF_KB // RET_DK
    vb, gb = OFF_VB // RET_DV, OFF_GB // RET_DV
    qk_blk, v_blk = (SEQ, RET_DK), (SEQ, RET_DV)
    scratch = [(qk_blk, BF16), (qk_blk, BF16), (v_blk, F32), ((RET_BLK, RET_BLK), F32),
               ((4, RET_BLK, RET_DV), F32), ((2, RET_DK, RET_DV), F32),
               ((RET_DK, RET_DV), F32), ((RET_DK, RET_DV), F32)]
    return pl.pallas_call(
        _ret_body,
        grid=(RET_HEADS, BATCH),
        in_specs=[pl.BlockSpec(memory_space=pltpu.SMEM),
                  pl.BlockSpec(qk_blk, lambda h, b: (b, qb + h)),
                  pl.BlockSpec(qk_blk, lambda h, b: (b, kb + h)),
                  pl.BlockSpec(v_blk, lambda h, b: (b, vb + h)),
                  pl.BlockSpec(v_blk, lambda h, b: (b, gb + h)),
                  pl.BlockSpec(qk_blk, lambda h, b: (0, 0)),
                  pl.BlockSpec(qk_blk, lambda h, b: (0, 0))],
        out_specs=pl.BlockSpec(v_blk, lambda h, b: (b, h)),
        out_shape=jax.ShapeDtypeStruct((M_TOK, RET_V_W), BF16),
        scratch_shapes=[pltpu.VMEM(s, d) for s, d in scratch],
        compiler_params=_params(
            ("parallel", "parallel"),
            [(qk_blk, BF16)] * 2 + [(v_blk, BF16)] * 3 + [(qk_blk, F32)] * 2,
            scratch + [((RET_BLK, RET_DV), F32)] * 4),
        name="retention",
    )(decay, u, u, u, u, cos2, sin2)


def _postmix_body(oa_ref, ob_ref, oc_ref, wb_ref, sa_ref, sb_ref, sc_ref, wo_ref, x_ref,
                  g_ref, wq_ref, kv_ref, wco_ref, o_ref):
    def branch(o, row0, s):
        return _sigmoid(s[...].astype(F32)) * _dot(o[...], wb_ref[row0:row0 + o.shape[1], :])

    merged = branch(oa_ref, 0, sa_ref) + branch(ob_ref, NA_W, sb_ref) + branch(oc_ref, NA_W + RET_V_W, sc_ref)
    x1 = x_ref[...] + _dot(merged.astype(BF16), wo_ref[...])
    q = _dot(_rms(x1, g_ref[...]).astype(BF16), wq_ref[...]).astype(BF16)
    heads = []
    for h in range(CROSS_HEADS):
        lo = h * HEAD_DIM
        k = kv_ref[:, lo:lo + HEAD_DIM]
        v = kv_ref[:, CROSS_W + lo:CROSS_W + lo + HEAD_DIM]
        s = _dot_nt(q[:, lo:lo + HEAD_DIM], k) * QK_SCALE2
        m = jnp.max(s, axis=-1, keepdims=True)
        p = jnp.exp2(s - m)
        l = jnp.sum(p, axis=-1, keepdims=True)
        heads.append((_dot(p.astype(BF16), v) * (1.0 / l)).astype(BF16))
    o_ref[...] = x1 + _dot(jnp.concatenate(heads, axis=-1), wco_ref[...])


def _postmix(o_a, o_b, o_c, w_branch, u, w_out, x, g, w_cq, kv, w_co, *, tm):
    mix_w = NA_W + RET_V_W + DIL_W
    per_batch = SEQ // tm
    once = pl.Buffered(1)

    def rows(k):
        return pl.BlockSpec((tm, k), lambda i: (i, 0))

    def gate(off):
        return pl.BlockSpec((pl.Element(tm), pl.Element(D_MODEL)), lambda i: (i * tm, off))

    def weight(r, c):
        return pl.BlockSpec((r, c), lambda i: (0, 0), pipeline_mode=once)

    return pl.pallas_call(
        _postmix_body,
        grid=(M_TOK // tm,),
        in_specs=[rows(NA_W), rows(RET_V_W), rows(DIL_W), weight(mix_w, D_MODEL),
                  gate(OFF_SA), gate(OFF_SB), gate(OFF_SC), weight(D_MODEL, D_MODEL), rows(D_MODEL),
                  pl.BlockSpec((1, D_MODEL), lambda i: (0, 0), pipeline_mode=once),
                  weight(D_MODEL, CROSS_W),
                  pl.BlockSpec((MEM_LEN, 2 * CROSS_W), lambda i: (i // per_batch, 0)),
                  weight(CROSS_W, D_MODEL)],
        out_specs=rows(D_MODEL),
        out_shape=jax.ShapeDtypeStruct((M_TOK, D_MODEL), F32),
        compiler_params=_params(
            ("parallel",),
            [((tm, mix_w), BF16)] + [((tm, D_MODEL), BF16)] * 3 + [((tm, D_MODEL), F32)] * 2
            + [((MEM_LEN, 2 * CROSS_W), BF16)],
            [((mix_w + D_MODEL + 2 * CROSS_W, D_MODEL), BF16)] + [((tm, D_MODEL), F32)] * 6),
        name="post_mix",
    )(o_a, o_b, o_c, w_branch, u, u, u, w_out, x, g.reshape(1, D_MODEL), w_cq, kv, w_co)


def _mlp_body(*refs, final):
    if final:
        x_ref, g_ref, w1_ref, w2_ref, gf_ref, o_ref, xn_ref = refs
    else:
        x_ref, g_ref, w1_ref, w2_ref, o_ref, xn_ref = refs

    @pl.when(pl.program_id(1) == 0)
    def _():
        x = x_ref[...]
        xn_ref[...] = _rms(x, g_ref[...]).astype(BF16)
        o_ref[...] = x

    hid = jnp.square(jnp.maximum(_dot(xn_ref[...], w1_ref[...]), 0.0)).astype(BF16)
    o_ref[...] += _dot(hid, w2_ref[...])

    if final:
        @pl.when(pl.program_id(1) == pl.num_programs(1) - 1)
        def _():
            o_ref[...] = _rms(o_ref[...], gf_ref[...])


def _mlp(x, g, w1, w2, *, tm, tf, cast=None, final_g=None):
    m, k = x.shape
    vec = pl.BlockSpec((1, k), lambda i, f: (0, 0))
    final = final_g is not None
    return _call_with_cast(
        functools.partial(_mlp_body, final=final),
        grid=(m // tm, D_FF // tf),
        in_specs=[pl.BlockSpec((tm, k), lambda i, f: (i, 0)), vec,
                  pl.BlockSpec((k, tf), lambda i, f: (0, f)),
                  pl.BlockSpec((tf, k), lambda i, f: (f, 0))] + ([vec] if final else []),
        out_spec=pl.BlockSpec((tm, k), lambda i, f: (i, 0)),
        out_shape=jax.ShapeDtypeStruct((m, k), F32),
        scratch=[((tm, k), BF16)],
        pipelined=[((tm, k), F32), ((1, k), F32), ((k, tf), BF16), ((tf, k), BF16), ((tm, k), F32)],
        temporaries=[((tm, tf), F32), ((tm, tf), BF16), ((tm, k), F32)],
        operands=[x, g.reshape(1, k), w1, w2] + ([final_g.reshape(1, k)] if final else []), name="mlp", cast=cast)


def kernel(x, mem, t5_bias, norm_mix_g, w_in, na_rpb, ret_decay, w_branch, w_out, norm_cross_g, norm_mem_g,
           w_cq, w_ckv, w_co, norm_mlp_g, w_mlp1, w_mlp2, final_norm_g):
    xs = x.reshape(M_TOK, D_MODEL)
    mem2 = mem.reshape(BATCH * MEM_LEN, D_MODEL)
    stacked = [w_in, w_mlp1, w_mlp2, w_branch, w_out, w_cq, w_ckv, w_co]
    w_in_h = w_in[0].astype(BF16)
    cos2, sin2 = _rotary_tables()
    dil_tiles = _dil_bias_tiles(t5_bias)

    for layer in range(DEPTH):
        u, rest = _normmm(xs, norm_mix_g[layer], w_in_h, tm=1024, tn=1536, name="in_proj",
                          cast=(0, stacked[1:]) if layer == 0 else None)
        if layer == 0:
            w_mlp1_h, w_mlp2_h, w_branch_h, w_out_h, w_cq_h, w_ckv_h, w_co_h = rest
        o_a = _na_attention(u, _na_bias_tiles(na_rpb[layer]))
        o_b = _retention(u, ret_decay[layer], cos2, sin2)
        o_c = _dil_attention(u, dil_tiles)
        kv, _ = _normmm(mem2, norm_mem_g[layer], w_ckv_h, tm=BATCH * MEM_LEN, tn=512, name="mem_kv_proj")
        xs = _postmix(o_a, o_b, o_c, w_branch_h, u, w_out_h, xs, norm_cross_g[layer], w_cq_h, kv, w_co_h, tm=256)
        last = layer + 1 == DEPTH
        xs, half = _mlp(xs, norm_mlp_g[layer], w_mlp1_h, w_mlp2_h, tm=512, tf=1024,
                        cast=None if last else (layer + 1, stacked), final_g=final_norm_g if last else None)
        if not last:
            w_in_h, w_mlp1_h, w_mlp2_h, w_branch_h, w_out_h, w_cq_h, w_ckv_h, w_co_h = half
    return xs.reshape(BATCH, SEQ, D_MODEL)
```

```python
import functools

import numpy as np
import jax
import jax.numpy as jnp
from jax import lax
from jax.experimental import pallas as pl
from jax.experimental.pallas import tpu as pltpu

D_MODEL = 2048
BATCH = 2
SEQ = 4096
DEPTH = 4
MEM_LEN = 256
HEAD_DIM = 128
GRID_W = 64
NA_HEADS = 6
NA_ROWS = 8
NA_COLS = 16
RET_HEADS = 4
RET_DK = 128
RET_DV = 256
RET_BLK = 256
ROPE_BASE = 10000.0
DIL_HEADS = 6
DIL_PAIRS = ((128, 1), (512, 4), (2048, 16))
T5_BUCKETS = 32
T5_MAX_DIST = 1024
CROSS_HEADS = 4
D_FF = 4 * D_MODEL
EPS = 1e-6
NEG = -1e30

NA_W = NA_HEADS * HEAD_DIM
RET_QK_W = RET_HEADS * RET_DK
RET_V_W = RET_HEADS * RET_DV
DIL_W = DIL_HEADS * HEAD_DIM
CROSS_W = CROSS_HEADS * HEAD_DIM
IN_SPLITS = (NA_W, NA_W, NA_W, RET_QK_W, RET_QK_W, RET_V_W, RET_V_W,
             DIL_W, DIL_W, DIL_W, D_MODEL, D_MODEL, D_MODEL)
IN_W = sum(IN_SPLITS)
(OFF_QA, OFF_KA, OFF_VA, OFF_QB, OFF_KB, OFF_VB, OFF_GB,
 OFF_QC, OFF_KC, OFF_VC, OFF_SA, OFF_SB, OFF_SC) = [int(o) for o in np.cumsum((0,) + IN_SPLITS[:-1])]

M_TOK = BATCH * SEQ
ATT_SCALE = HEAD_DIM ** -0.5
LOG2E = float(np.log2(np.e))
QK_SCALE2 = ATT_SCALE * LOG2E

V7X_VMEM_BYTES = 64 * 1024 * 1024
LANES = 128
SUBLANES = 8

NA_QROWS = 4
NA_KROWS = NA_QROWS + NA_ROWS
NA_QBLK = NA_QROWS * GRID_W
NA_KBLK = NA_KROWS * GRID_W
GRID_H = SEQ // GRID_W
NA_NBLK = GRID_H // NA_QROWS

DIL_BLK = 256
DIL_REACH = (max(w for w, _ in DIL_PAIRS) // 2 + DIL_BLK - 1) // DIL_BLK
DIL_NDELTA = 2 * DIL_REACH + 1
DIL_NBLK = SEQ // DIL_BLK

F32 = jnp.float32
BF16 = jnp.bfloat16


def _nbytes(shape, dtype):
    return int(np.prod(shape)) * jnp.dtype(dtype).itemsize


def _params(semantics, pipelined, resident):
    need = 2 * sum(_nbytes(s, d) for s, d in pipelined) + sum(_nbytes(s, d) for s, d in resident)
    assert need < V7X_VMEM_BYTES, need
    return pltpu.CompilerParams(dimension_semantics=semantics,
                                vmem_limit_bytes=min(V7X_VMEM_BYTES, need + need // 4))


def _rms(x, g):
    ms = jnp.mean(x * x, axis=-1, keepdims=True)
    return x * lax.rsqrt(ms + EPS) * g


def _sigmoid(x):
    return 1.0 / (1.0 + jnp.exp(-x))


def _dot(a, b):
    return jnp.dot(a, b, preferred_element_type=F32)


def _dot_nt(a, b):
    return lax.dot_general(a, b, (((1,), (1,)), ((), ())), preferred_element_type=F32)


def _dot_tn(a, b):
    return lax.dot_general(a, b, (((0,), (0,)), ((), ())), preferred_element_type=F32)


CAST_CHUNK_BYTES = 1 << 19


def _cast_rows(rows, cols, nsteps):
    r = 2 * SUBLANES
    while rows % r or rows // r > nsteps or (r * cols * 4 < CAST_CHUNK_BYTES and r < rows):
        r += 2 * SUBLANES
    return r


def _cast_chunk(step, nsteps, layer, srcs, dsts, stages, halves, sem_in, sem_out):
    slot = step % 2
    every = range(len(srcs))
    rows = [st.shape[1] for st in stages]
    nchunk = [dsts[k].shape[0] // rows[k] for k in every]

    def in_copy(k, s, sl):
        return pltpu.make_async_copy(srcs[k].at[layer, pl.ds(pl.multiple_of(s * rows[k], rows[k]), rows[k]), :],
                                     stages[k].at[sl], sem_in.at[k, sl])

    def out_copy(k, s, sl):
        return pltpu.make_async_copy(halves[k].at[sl], dsts[k].at[pl.ds(pl.multiple_of(s * rows[k], rows[k]), rows[k]), :],
                                     sem_out.at[k, sl])

    def cast(k):
        in_copy(k, step, slot).wait()
        halves[k][slot] = stages[k][slot].astype(BF16)

    @pl.when(step == 0)
    def _():
        for k in every:
            in_copy(k, 0, 0).start()

    full = [k for k in every if nchunk[k] == nsteps]
    for nc in sorted(set(nchunk)):
        group = [k for k in every if nchunk[k] == nc]

        @pl.when(step + 1 < nc)
        def _():
            for k in group:
                in_copy(k, step + 1, 1 - slot).start()

        @pl.when((step >= 2) & (step < nc + 2))
        def _():
            for k in group:
                out_copy(k, step - 2, slot).wait()

        if nc < nsteps:
            @pl.when(step < nc)
            def _():
                for k in group:
                    cast(k)
                    out_copy(k, step, slot).start()

    for k in full:
        cast(k)

    def epilogue():
        for k in full:
            out_copy(k, step, slot).start()

        late = [(k, c) for k in every for c in range(max(nchunk[k] - 2, 0), nchunk[k]) if c + 2 > nsteps - 1]
        if late:
            @pl.when(step == nsteps - 1)
            def _():
                for k, c in late:
                    out_copy(k, c, c % 2).wait()

    return epilogue


def _call_with_cast(body, *, grid, in_specs, out_spec, out_shape, scratch, pipelined, temporaries, operands, name,
                    cast=None):
    layer, srcs = cast if cast is not None else (None, [])
    n, n_in, n_scr = len(srcs), len(in_specs), len(scratch)
    nsteps = grid[0] * grid[1]
    chunks = [(_cast_rows(w.shape[1], w.shape[2], nsteps), w.shape[2]) for w in srcs]
    cast_scratch = [((2,) + c, F32) for c in chunks] + [((2,) + c, BF16) for c in chunks]

    def wrapped(*refs):
        ins, srcs_r, out = refs[:n_in], refs[n_in:n_in + n], refs[n_in + n]
        dsts = refs[n_in + n + 1:n_in + 2 * n + 1]
        rest = refs[n_in + 2 * n + 1:]
        after = None
        if n:
            stages, halves, (sem_in, sem_out) = rest[n_scr:n_scr + n], rest[n_scr + n:n_scr + 2 * n], rest[-2:]
            step = pl.program_id(0) * grid[1] + pl.program_id(1)
            after = _cast_chunk(step, nsteps, layer, srcs_r, dsts, stages, halves, sem_in, sem_out)
        body(*ins, out, *rest[:n_scr])
        if after is not None:
            after()

    hbm = pl.BlockSpec(memory_space=pltpu.HBM)
    out = pl.pallas_call(
        wrapped,
        grid=grid,
        in_specs=list(in_specs) + [hbm] * n,
        out_specs=[out_spec] + [hbm] * n,
        out_shape=[out_shape] + [jax.ShapeDtypeStruct(w.shape[1:], BF16) for w in srcs],
        scratch_shapes=[pltpu.VMEM(sh, d) for sh, d in scratch + cast_scratch]
        + ([pltpu.SemaphoreType.DMA((n, 2))] * 2 if n else []),
        compiler_params=_params(("arbitrary", "arbitrary") if n else ("parallel", "arbitrary"),
                                pipelined, scratch + cast_scratch + temporaries),
        name=name,
    )(*operands, *srcs)
    return out[0], out[1:]


def _normmm_body(x_ref, g_ref, w_ref, o_ref, xn_ref):
    @pl.when(pl.program_id(1) == 0)
    def _():
        xn_ref[...] = _rms(x_ref[...], g_ref[...]).astype(BF16)

    o_ref[...] = _dot(xn_ref[...], w_ref[...]).astype(o_ref.dtype)


def _normmm(x, g, w, *, tm, tn, name, cast=None):
    m, k = x.shape
    n = w.shape[-1]
    return _call_with_cast(
        _normmm_body,
        grid=(m // tm, n // tn),
        in_specs=[pl.BlockSpec((tm, k), lambda i, j: (i, 0)),
                  pl.BlockSpec((1, k), lambda i, j: (0, 0)),
                  pl.BlockSpec((k, tn), lambda i, j: (0, j))],
        out_spec=pl.BlockSpec((tm, tn), lambda i, j: (i, j)),
        out_shape=jax.ShapeDtypeStruct((m, n), BF16),
        scratch=[((tm, k), BF16)],
        pipelined=[((tm, k), F32), ((1, k), F32), ((k, tn), BF16), ((tm, tn), BF16)],
        temporaries=[((tm, tn), F32)],
        operands=[x, g.reshape(1, k), w], name=name, cast=cast)


NA_RPB_ROWS = 2 * NA_ROWS - 1
NA_RPB_COLS = 2 * NA_COLS - 1
NA_TILE_ROWS = ((0, 0), (NA_QROWS, 0), (GRID_H - NA_QROWS, GRID_H - NA_KROWS))


def _na_tiles_body(rpb_ref, o_ref):
    shape = (GRID_W, LANES)
    lane = lax.broadcasted_iota(jnp.int32, shape, 1)
    cq = lax.broadcasted_iota(jnp.int32, shape, 0)
    ck = lane & (GRID_W - 1)
    c0 = jnp.clip(cq - NA_COLS // 2, 0, GRID_W - NA_COLS)
    col_ok = (ck >= c0) & (ck < c0 + NA_COLS)
    left = lane < GRID_W
    neg = jnp.full(shape, NEG, F32)
    lo, hi = [], []
    for a in range(NA_RPB_ROWS):
        row = jnp.broadcast_to(rpb_ref[a:a + 1, :], shape)
        lo.append(pltpu.roll(row, LANES - (NA_COLS - 1), 1, stride=1, stride_axis=0))
        hi.append(pltpu.roll(row, GRID_W - (NA_COLS - 1), 1, stride=1, stride_axis=0))
    for t, (qrow0, krow0) in enumerate(NA_TILE_ROWS):
        for rq in range(NA_QROWS):
            r = qrow0 + rq
            r0 = min(max(r - NA_ROWS // 2, 0), GRID_H - NA_ROWS)
            for pair in range(NA_KROWS // 2):
                halves = []
                for side, table in ((0, lo), (1, hi)):
                    rk = krow0 + 2 * pair + side
                    halves.append(table[rk - r + NA_ROWS - 1] if r0 <= rk < r0 + NA_ROWS else neg)
                blk = jnp.where(col_ok, jnp.where(left, halves[0], halves[1]) * LOG2E, NEG)
                o_ref[t, rq * GRID_W:(rq + 1) * GRID_W, pair * LANES:(pair + 1) * LANES] = blk


def _na_bias_tiles(rpb):
    padded = jnp.pad(rpb.astype(F32), ((0, 0), (0, 2 * SUBLANES - NA_RPB_ROWS), (0, LANES - NA_RPB_COLS)))
    return pl.pallas_call(
        _na_tiles_body,
        grid=(NA_HEADS,),
        in_specs=[pl.BlockSpec((None, 2 * SUBLANES, LANES), lambda h: (h, 0, 0))],
        out_specs=pl.BlockSpec((None, 3, NA_QBLK, NA_KBLK), lambda h: (h, 0, 0, 0)),
        out_shape=jax.ShapeDtypeStruct((NA_HEADS, 3, NA_QBLK, NA_KBLK), F32),
        compiler_params=_params(("parallel",), [((3, NA_QBLK, NA_KBLK), F32)], [((NA_QBLK, NA_KBLK), F32)]),
        name="na_bias_tiles",
    )(padded)


def _na_body(q_ref, k_ref, v_ref, bias_ref, o_ref, s_ref):
    def starts(i):
        qrow = i * NA_QROWS
        krow = jnp.clip(qrow - NA_ROWS // 2, 0, GRID_H - NA_KROWS)
        return pl.multiple_of(qrow * GRID_W, NA_QBLK), pl.multiple_of(krow * GRID_W, NA_QBLK)

    def logits(i, buf):
        qs, ks = starts(i)
        tile = jnp.where(i == 0, 0, jnp.where(i == NA_NBLK - 1, 2, 1))
        s_ref[buf] = (_dot_nt(q_ref[pl.ds(qs, NA_QBLK), :], k_ref[pl.ds(ks, NA_KBLK), :]) * QK_SCALE2
                      + bias_ref[tile])

    def finish(i, buf):
        qs, ks = starts(i)
        s = s_ref[buf]
        m = jnp.max(s, axis=-1, keepdims=True)
        p = jnp.exp2(s - m)
        l = jnp.sum(p, axis=-1, keepdims=True)
        o = _dot(p.astype(BF16), v_ref[pl.ds(ks, NA_KBLK), :]) * (1.0 / l)
        o_ref[pl.ds(qs, NA_QBLK), :] = o.astype(o_ref.dtype)

    logits(0, 0)

    def pair(ii, carry):
        i = 2 * ii
        logits(i + 1, 1)
        finish(i, 0)
        logits(jnp.minimum(i + 2, NA_NBLK - 1), 0)
        finish(i + 1, 1)
        return carry

    lax.fori_loop(0, NA_NBLK // 2, pair, 0)


def _na_attention(u, bias):
    qb, kb, vb = OFF_QA // HEAD_DIM, OFF_KA // HEAD_DIM, OFF_VA // HEAD_DIM
    blk = (SEQ, HEAD_DIM)
    return pl.pallas_call(
        _na_body,
        grid=(NA_HEADS, BATCH),
        in_specs=[pl.BlockSpec(blk, lambda h, b: (b, qb + h)),
                  pl.BlockSpec(blk, lambda h, b: (b, kb + h)),
                  pl.BlockSpec(blk, lambda h, b: (b, vb + h)),
                  pl.BlockSpec((None, 3, NA_QBLK, NA_KBLK), lambda h, b: (h, 0, 0, 0))],
        out_specs=pl.BlockSpec(blk, lambda h, b: (b, h)),
        out_shape=jax.ShapeDtypeStruct((M_TOK, NA_W), BF16),
        scratch_shapes=[pltpu.VMEM((2, NA_QBLK, NA_KBLK), F32)],
        compiler_params=_params(
            ("parallel", "parallel"),
            [(blk, BF16)] * 4 + [((3, NA_QBLK, NA_KBLK), F32)],
            [((NA_QBLK, NA_KBLK), F32)] * 3),
        name="na_attention",
    )(u, u, u, bias)


def _t5_bucket(rel):
    nb = T5_BUCKETS // 2
    ret = (rel > 0).astype(np.int32) * nb
    n = np.abs(rel)
    max_exact = nb // 2
    large = max_exact + (np.log(np.maximum(n, 1) / max_exact) / np.log(T5_MAX_DIST / max_exact)
                         * (nb - max_exact)).astype(np.int32)
    large = np.minimum(large, nb - 1)
    return (ret + np.where(n < max_exact, n, large)).astype(np.int32)


def _dil_tiles_body(f_ref, o_ref):
    for d in range(DIL_NDELTA):
        row = jnp.broadcast_to(f_ref[d:d + 1, :], (DIL_BLK, 2 * DIL_BLK))
        o_ref[d] = pltpu.roll(row, DIL_BLK, 1, stride=1, stride_axis=0)[:, :DIL_BLK]
    o_ref[DIL_NDELTA] = jnp.full((DIL_BLK, DIL_BLK), NEG, F32)


def _dil_bias_tiles(t5_bias):
    delta = np.arange(-DIL_REACH, DIL_REACH + 1)[:, None]
    off = delta * DIL_BLK + np.arange(-DIL_BLK, DIL_BLK)[None, :]
    count = np.zeros(off.shape, np.int32)
    for w, d in DIL_PAIRS:
        count += ((off % d == 0) & (np.abs(off) <= w // 2)).astype(np.int32)
    bucket = _t5_bucket(np.clip(off, -T5_MAX_DIST, T5_MAX_DIST))
    bias = jnp.take(t5_bias.T.astype(F32), jnp.asarray(bucket.reshape(-1)), axis=1).reshape((DIL_HEADS,) + off.shape)
    logc = jnp.log(jnp.asarray(np.maximum(count, 1), F32))
    f = jnp.where(jnp.asarray(count > 0)[None], (bias + logc[None]) * LOG2E, NEG)
    f = jnp.pad(f, ((0, 0), (0, 2 * SUBLANES - DIL_NDELTA), (0, 0)))
    return pl.pallas_call(
        _dil_tiles_body,
        grid=(DIL_HEADS,),
        in_specs=[pl.BlockSpec((None, 2 * SUBLANES, 2 * DIL_BLK), lambda h: (h, 0, 0))],
        out_specs=pl.BlockSpec((None, DIL_NDELTA + 1, DIL_BLK, DIL_BLK), lambda h: (h, 0, 0, 0)),
        out_shape=jax.ShapeDtypeStruct((DIL_HEADS, DIL_NDELTA + 1, DIL_BLK, DIL_BLK), F32),
        compiler_params=_params(("parallel",), [((DIL_NDELTA + 1, DIL_BLK, DIL_BLK), F32)],
                                [((DIL_BLK, 2 * DIL_BLK), F32)] * 2),
        name="dil_bias_tiles",
    )(f)


def _dil_body(q_ref, k_ref, v_ref, t_ref, o_ref, s_ref):
    def key_start(i, d):
        return pl.multiple_of(jnp.clip(i + (d - DIL_REACH), 0, DIL_NBLK - 1) * DIL_BLK, DIL_BLK)

    def logits(i, buf):
        q = q_ref[pl.ds(pl.multiple_of(i * DIL_BLK, DIL_BLK), DIL_BLK), :]
        for d in range(DIL_NDELTA):
            j = i + (d - DIL_REACH)
            tile = jnp.where((j >= 0) & (j < DIL_NBLK), d, DIL_NDELTA)
            s_ref[buf, :, d * DIL_BLK:(d + 1) * DIL_BLK] = (
                _dot_nt(q, k_ref[pl.ds(key_start(i, d), DIL_BLK), :]) * QK_SCALE2 + t_ref[tile])

    def finish(i, buf):
        m = jnp.max(s_ref[buf], axis=-1, keepdims=True)
        l = jnp.zeros((DIL_BLK, 1), F32)
        acc = jnp.zeros((DIL_BLK, HEAD_DIM), F32)
        for d in range(DIL_NDELTA):
            p = jnp.exp2(s_ref[buf, :, d * DIL_BLK:(d + 1) * DIL_BLK] - m)
            l = l + jnp.sum(p, axis=-1, keepdims=True)
            acc = acc + _dot(p.astype(BF16), v_ref[pl.ds(key_start(i, d), DIL_BLK), :])
        o_ref[pl.ds(pl.multiple_of(i * DIL_BLK, DIL_BLK), DIL_BLK), :] = (acc * (1.0 / l)).astype(o_ref.dtype)

    logits(0, 0)

    def pair(ii, carry):
        i = 2 * ii
        logits(i + 1, 1)
        finish(i, 0)
        logits(jnp.minimum(i + 2, DIL_NBLK - 1), 0)
        finish(i + 1, 1)
        return carry

    lax.fori_loop(0, DIL_NBLK // 2, pair, 0)


def _dil_attention(u, tiles):
    qb, kb, vb = OFF_QC // HEAD_DIM, OFF_KC // HEAD_DIM, OFF_VC // HEAD_DIM
    blk = (SEQ, HEAD_DIM)
    return pl.pallas_call(
        _dil_body,
        grid=(DIL_HEADS, BATCH),
        in_specs=[pl.BlockSpec(blk, lambda h, b: (b, qb + h)),
                  pl.BlockSpec(blk, lambda h, b: (b, kb + h)),
                  pl.BlockSpec(blk, lambda h, b: (b, vb + h)),
                  pl.BlockSpec((None, DIL_NDELTA + 1, DIL_BLK, DIL_BLK), lambda h, b: (h, 0, 0, 0))],
        out_specs=pl.BlockSpec(blk, lambda h, b: (b, h)),
        out_shape=jax.ShapeDtypeStruct((M_TOK, DIL_W), BF16),
        scratch_shapes=[pltpu.VMEM((2, DIL_BLK, DIL_NDELTA * DIL_BLK), F32)],
        compiler_params=_params(
            ("parallel", "parallel"),
            [(blk, BF16)] * 4 + [((DIL_NDELTA + 1, DIL_BLK, DIL_BLK), F32)],
            [((DIL_BLK, DIL_NDELTA * DIL_BLK), F32)] * 3),
        name="dil_attention",
    )(u, u, u, tiles)


def _rotary_tables():
    inv_freq = jnp.asarray((ROPE_BASE ** (-np.arange(0, RET_DK, 2, dtype=np.float32) / RET_DK)).astype(np.float32))
    ang = jnp.arange(SEQ, dtype=F32)[:, None] * inv_freq[None, :]
    cos, sin = jnp.cos(ang), jnp.sin(ang)
    return jnp.concatenate([cos, cos], axis=-1), jnp.concatenate([-sin, sin], axis=-1)


def _ret_tables(dec_ref, h, d_ref, xz_ref, gc_ref):
    c = RET_BLK
    row = lax.broadcasted_iota(jnp.int32, (c, c), 0)
    col = lax.broadcasted_iota(jnp.int32, (c, c), 1)
    diff = (row - col).astype(F32)
    log_f = -jnp.exp(jnp.full((c, c), dec_ref[0, h], F32))
    log_b = -jnp.exp(jnp.full((c, c), dec_ref[1, h], F32))
    d_ref[...] = jnp.where(diff >= 0.0, jnp.exp(jnp.maximum(diff, 0.0) * log_f),
                           jnp.exp(jnp.maximum(-diff, 0.0) * log_b))
    j = lax.broadcasted_iota(jnp.int32, (c, RET_DV), 0).astype(F32)
    log_f2 = -jnp.exp(jnp.full((c, RET_DV), dec_ref[0, h], F32))
    log_b2 = -jnp.exp(jnp.full((c, RET_DV), dec_ref[1, h], F32))
    xz_ref[0] = jnp.exp((j + 1.0) * log_f2)
    xz_ref[1] = jnp.exp((c - 1.0 - j) * log_f2)
    xz_ref[2] = jnp.exp((c - j) * log_b2)
    xz_ref[3] = jnp.exp(j * log_b2)
    gc_ref[0] = jnp.exp(c * -jnp.exp(jnp.full((RET_DK, RET_DV), dec_ref[0, h], F32)))
    gc_ref[1] = jnp.exp(c * -jnp.exp(jnp.full((RET_DK, RET_DV), dec_ref[1, h], F32)))


def _ret_body(dec_ref, q_ref, k_ref, v_ref, gate_ref, cos_ref, sin_ref, o_ref,
              qr_ref, kr_ref, y_ref, d_ref, xz_ref, gc_ref, sf_ref, sb_ref):
    c = RET_BLK
    nblk = SEQ // c
    _ret_tables(dec_ref, pl.program_id(0), d_ref, xz_ref, gc_ref)
    sf_ref[...] = jnp.zeros_like(sf_ref)
    sb_ref[...] = jnp.zeros_like(sb_ref)

    def rotate(x, cs, sn):
        return x * cs + pltpu.roll(x, RET_DK // 2, 1) * sn

    def up(i, carry):
        sl = pl.ds(pl.multiple_of(i * c, c), c)
        cs, sn = cos_ref[sl, :], sin_ref[sl, :]
        q = rotate(q_ref[sl, :].astype(F32), cs, sn).astype(BF16)
        k = (rotate(k_ref[sl, :].astype(F32), cs, sn) * (RET_DK ** -0.5)).astype(BF16)
        qr_ref[sl, :] = q
        kr_ref[sl, :] = k
        v = v_ref[sl, :]
        inner = (_dot_nt(q, k) * d_ref[...]).astype(BF16)
        state = sf_ref[...]
        y_ref[sl, :] = _dot(inner, v) + _dot(q, state.astype(BF16)) * xz_ref[0]
        sf_ref[...] = state * gc_ref[0] + _dot_tn(k, (v.astype(F32) * xz_ref[1]).astype(BF16))
        return carry

    lax.fori_loop(0, nblk, up, 0, unroll=4)

    def down(ii, carry):
        sl = pl.ds(pl.multiple_of((nblk - 1 - ii) * c, c), c)
        q, k, v = qr_ref[sl, :], kr_ref[sl, :], v_ref[sl, :]
        state = sb_ref[...]
        y = y_ref[sl, :] + _dot(q, state.astype(BF16)) * xz_ref[2]
        sb_ref[...] = state * gc_ref[1] + _dot_tn(k, (v.astype(F32) * xz_ref[3]).astype(BF16))
        mu = jnp.mean(y, axis=-1, keepdims=True)
        yc = y - mu
        var = jnp.mean(yc * yc, axis=-1, keepdims=True)
        gate = gate_ref[sl, :].astype(F32)
        o_ref[sl, :] = (gate * _sigmoid(gate) * (yc * lax.rsqrt(var + EPS))).astype(o_ref.dtype)
        return carry

    lax.fori_loop(0, nblk, down, 0, unroll=4)


def _retention(u, decay, cos2, sin2):
    qb, kb = OFF_QB // RET_DK, OFF_KB // RET_DK
    vb, gb = OFF_VB // RET_DV, OFF_GB // RET_DV
    qk_blk, v_blk = (SEQ, RET_DK), (SEQ, RET_DV)
    scratch = [(qk_blk, BF16), (qk_blk, BF16), (v_blk, F32), ((RET_BLK, RET_BLK), F32),
               ((4, RET_BLK, RET_DV), F32), ((2, RET_DK, RET_DV), F32),
               ((RET_DK, RET_DV), F32), ((RET_DK, RET_DV), F32)]
    return pl.pallas_call(
        _ret_body,
        grid=(RET_HEADS, BATCH),
        in_specs=[pl.BlockSpec(memory_space=pltpu.SMEM),
                  pl.BlockSpec(qk_blk, lambda h, b: (b, qb + h)),
                  pl.BlockSpec(qk_blk, lambda h, b: (b, kb + h)),
                  pl.BlockSpec(v_blk, lambda h, b: (b, vb + h)),
                  pl.BlockSpec(v_blk, lambda h, b: (b, gb + h)),
                  pl.BlockSpec(qk_blk, lambda h, b: (0, 0)),
                  pl.BlockSpec(qk_blk, lambda h, b: (0, 0))],
        out_specs=pl.BlockSpec(v_blk, lambda h, b: (b, h)),
        out_shape=jax.ShapeDtypeStruct((M_TOK, RET_V_W), BF16),
        scratch_shapes=[pltpu.VMEM(s, d) for s, d in scratch],
        compiler_params=_params(
            ("parallel", "parallel"),
            [(qk_blk, BF16)] * 2 + [(v_blk, BF16)] * 3 + [(qk_blk, F32)] * 2,
            scratch + [((RET_BLK, RET_DV), F32)] * 4),
        name="retention",
    )(decay, u, u, u, u, cos2, sin2)


def _postmix_body(oa_ref, ob_ref, oc_ref, wb_ref, sa_ref, sb_ref, sc_ref, wo_ref, x_ref,
                  g_ref, wq_ref, kv_ref, wco_ref, o_ref):
    def branch(o, row0, s):
        return _sigmoid(s[...].astype(F32)) * _dot(o[...], wb_ref[row0:row0 + o.shape[1], :])

    merged = branch(oa_ref, 0, sa_ref) + branch(ob_ref, NA_W, sb_ref) + branch(oc_ref, NA_W + RET_V_W, sc_ref)
    x1 = x_ref[...] + _dot(merged.astype(BF16), wo_ref[...])
    q = _dot(_rms(x1, g_ref[...]).astype(BF16), wq_ref[...]).astype(BF16)
    heads = []
    for h in range(CROSS_HEADS):
        lo = h * HEAD_DIM
        k = kv_ref[:, lo:lo + HEAD_DIM]
        v = kv_ref[:, CROSS_W + lo:CROSS_W + lo + HEAD_DIM]
        s = _dot_nt(q[:, lo:lo + HEAD_DIM], k) * QK_SCALE2
        m = jnp.max(s, axis=-1, keepdims=True)
        p = jnp.exp2(s - m)
        l = jnp.sum(p, axis=-1, keepdims=True)
        heads.append((_dot(p.astype(BF16), v) * (1.0 / l)).astype(BF16))
    o_ref[...] = x1 + _dot(jnp.concatenate(heads, axis=-1), wco_ref[...])


def _postmix(o_a, o_b, o_c, w_branch, u, w_out, x, g, w_cq, kv, w_co, *, tm):
    mix_w = NA_W + RET_V_W + DIL_W
    per_batch = SEQ // tm
    once = pl.Buffered(1)

    def rows(k):
        return pl.BlockSpec((tm, k), lambda i: (i, 0))

    def gate(off):
        return pl.BlockSpec((pl.Element(tm), pl.Element(D_MODEL)), lambda i: (i * tm, off))

    def weight(r, c):
        return pl.BlockSpec((r, c), lambda i: (0, 0), pipeline_mode=once)

    return pl.pallas_call(
        _postmix_body,
        grid=(M_TOK // tm,),
        in_specs=[rows(NA_W), rows(RET_V_W), rows(DIL_W), weight(mix_w, D_MODEL),
                  gate(OFF_SA), gate(OFF_SB), gate(OFF_SC), weight(D_MODEL, D_MODEL), rows(D_MODEL),
                  pl.BlockSpec((1, D_MODEL), lambda i: (0, 0), pipeline_mode=once),
                  weight(D_MODEL, CROSS_W),
                  pl.BlockSpec((MEM_LEN, 2 * CROSS_W), lambda i: (i // per_batch, 0)),
                  weight(CROSS_W, D_MODEL)],
        out_specs=rows(D_MODEL),
        out_shape=jax.ShapeDtypeStruct((M_TOK, D_MODEL), F32),
        compiler_params=_params(
            ("parallel",),
            [((tm, mix_w), BF16)] + [((tm, D_MODEL), BF16)] * 3 + [((tm, D_MODEL), F32)] * 2
            + [((MEM_LEN, 2 * CROSS_W), BF16)],
            [((mix_w + D_MODEL + 2 * CROSS_W, D_MODEL), BF16)] + [((tm, D_MODEL), F32)] * 6),
        name="post_mix",
    )(o_a, o_b, o_c, w_branch, u, u, u, w_out, x, g.reshape(1, D_MODEL), w_cq, kv, w_co)


def _mlp_body(*refs, final):
    if final:
        x_ref, g_ref, w1_ref, w2_ref, gf_ref, o_ref, xn_ref = refs
    else:
        x_ref, g_ref, w1_ref, w2_ref, o_ref, xn_ref = refs

    @pl.when(pl.program_id(1) == 0)
    def _():
        x = x_ref[...]
        xn_ref[...] = _rms(x, g_ref[...]).astype(BF16)
        o_ref[...] = x

    hid = jnp.square(jnp.maximum(_dot(xn_ref[...], w1_ref[...]), 0.0)).astype(BF16)
    o_ref[...] += _dot(hid, w2_ref[...])

    if final:
        @pl.when(pl.program_id(1) == pl.num_programs(1) - 1)
        def _():
            o_ref[...] = _rms(o_ref[...], gf_ref[...])


def _mlp(x, g, w1, w2, *, tm, tf, cast=None, final_g=None):
    m, k = x.shape
    vec = pl.BlockSpec((1, k), lambda i, f: (0, 0))
    final = final_g is not None
    return _call_with_cast(
        functools.partial(_mlp_body, final=final),
        grid=(m // tm, D_FF // tf),
        in_specs=[pl.BlockSpec((tm, k), lambda i, f: (i, 0)), vec,
                  pl.BlockSpec((k, tf), lambda i, f: (0, f)),
                  pl.BlockSpec((tf, k), lambda i, f: (f, 0))] + ([vec] if final else []),
        out_spec=pl.BlockSpec((tm, k), lambda i, f: (i, 0)),
        out_shape=jax.ShapeDtypeStruct((m, k), F32),
        scratch=[((tm, k), BF16)],
        pipelined=[((tm, k), F32), ((1, k), F32), ((k, tf), BF16), ((tf, k), BF16), ((tm, k), F32)],
        temporaries=[((tm, tf), F32), ((tm, tf), BF16), ((tm, k), F32)],
        operands=[x, g.reshape(1, k), w1, w2] + ([final_g.reshape(1, k)] if final else []), name="mlp", cast=cast)


def kernel(x, mem, t5_bias, norm_mix_g, w_in, na_rpb, ret_decay, w_branch, w_out, norm_cross_g, norm_mem_g,
           w_cq, w_ckv, w_co, norm_mlp_g, w_mlp1, w_mlp2, final_norm_g):
    xs = x.reshape(M_TOK, D_MODEL)
    mem2 = mem.reshape(BATCH * MEM_LEN, D_MODEL)
    stacked = [w_in, w_mlp1, w_mlp2, w_branch, w_out, w_cq, w_ckv, w_co]
    w_in_h = w_in[0].astype(BF16)
    cos2, sin2 = _rotary_tables()
    dil_tiles = _dil_bias_tiles(t5_bias)

    for layer in range(DEPTH):
        u, rest = _normmm(xs, norm_mix_g[layer], w_in_h, tm=1024, tn=1536, name="in_proj",
                          cast=(layer, stacked[1:]))
        w_mlp1_h, w_mlp2_h, w_branch_h, w_out_h, w_cq_h, w_ckv_h, w_co_h = rest
        o_a = _na_attention(u, _na_bias_tiles(na_rpb[layer]))
        o_b = _retention(u, ret_decay[layer], cos2, sin2)
        o_c = _dil_attention(u, dil_tiles)
        kv, _ = _normmm(mem2, norm_mem_g[layer], w_ckv_h, tm=BATCH * MEM_LEN, tn=512, name="mem_kv_proj")
        xs = _postmix(o_a, o_b, o_c, w_branch_h, u, w_out_h, xs, norm_cross_g[layer], w_cq_h, kv, w_co_h, tm=256)
        last = layer + 1 == DEPTH
        xs, half = _mlp(xs, norm_mlp_g[layer], w_mlp1_h, w_mlp2_h, tm=512, tf=1024,
                        cast=None if last else (layer + 1, stacked[:1]), final_g=final_norm_g if last else None)
        if not last:
            w_in_h, = half
    return xs.reshape(BATCH, SEQ, D_MODEL)
```

```python
import functools

import numpy as np
import jax
import jax.numpy as jnp
from jax import lax
from jax.experimental import pallas as pl
from jax.experimental.pallas import tpu as pltpu

D_MODEL = 2048
BATCH = 2
SEQ = 4096
DEPTH = 4
MEM_LEN = 256
HEAD_DIM = 128
GRID_W = 64
NA_HEADS = 6
NA_ROWS = 8
NA_COLS = 16
RET_HEADS = 4
RET_DK = 128
RET_DV = 256
RET_BLK = 256
ROPE_BASE = 10000.0
DIL_HEADS = 6
DIL_PAIRS = ((128, 1), (512, 4), (2048, 16))
T5_BUCKETS = 32
T5_MAX_DIST = 1024
CROSS_HEADS = 4
D_FF = 4 * D_MODEL
EPS = 1e-6
NEG = -1e30

NA_W = NA_HEADS * HEAD_DIM
RET_QK_W = RET_HEADS * RET_DK
RET_V_W = RET_HEADS * RET_DV
DIL_W = DIL_HEADS * HEAD_DIM
CROSS_W = CROSS_HEADS * HEAD_DIM
IN_SPLITS = (NA_W, NA_W, NA_W, RET_QK_W, RET_QK_W, RET_V_W, RET_V_W,
             DIL_W, DIL_W, DIL_W, D_MODEL, D_MODEL, D_MODEL)
IN_W = sum(IN_SPLITS)
(OFF_QA, OFF_KA, OFF_VA, OFF_QB, OFF_KB, OFF_VB, OFF_GB,
 OFF_QC, OFF_KC, OFF_VC, OFF_SA, OFF_SB, OFF_SC) = [int(o) for o in np.cumsum((0,) + IN_SPLITS[:-1])]

M_TOK = BATCH * SEQ
ATT_SCALE = HEAD_DIM ** -0.5
LOG2E = float(np.log2(np.e))
QK_SCALE2 = ATT_SCALE * LOG2E

V7X_VMEM_BYTES = 64 * 1024 * 1024
LANES = 128
SUBLANES = 8

NA_QROWS = 4
NA_KROWS = NA_QROWS + NA_ROWS
NA_QBLK = NA_QROWS * GRID_W
NA_KBLK = NA_KROWS * GRID_W
GRID_H = SEQ // GRID_W
NA_NBLK = GRID_H // NA_QROWS

DIL_BLK = 256
DIL_REACH = (max(w for w, _ in DIL_PAIRS) // 2 + DIL_BLK - 1) // DIL_BLK
DIL_NDELTA = 2 * DIL_REACH + 1
DIL_NBLK = SEQ // DIL_BLK

F32 = jnp.float32
BF16 = jnp.bfloat16


def _nbytes(shape, dtype):
    return int(np.prod(shape)) * jnp.dtype(dtype).itemsize


def _params(semantics, pipelined, resident):
    need = 2 * sum(_nbytes(s, d) for s, d in pipelined) + sum(_nbytes(s, d) for s, d in resident)
    assert need < V7X_VMEM_BYTES, need
    return pltpu.CompilerParams(dimension_semantics=semantics,
                                vmem_limit_bytes=min(V7X_VMEM_BYTES, need + need // 4))


def _rms(x, g):
    ms = jnp.mean(x * x, axis=-1, keepdims=True)
    return x * lax.rsqrt(ms + EPS) * g


def _sigmoid(x):
    return 1.0 / (1.0 + jnp.exp(-x))


def _dot(a, b):
    return jnp.dot(a, b, preferred_element_type=F32)


def _dot_nt(a, b):
    return lax.dot_general(a, b, (((1,), (1,)), ((), ())), preferred_element_type=F32)


def _dot_tn(a, b):
    return lax.dot_general(a, b, (((0,), (0,)), ((), ())), preferred_element_type=F32)


CAST_CHUNK_BYTES = 1 << 19


def _cast_rows(rows, cols, nsteps):
    r = 2 * SUBLANES
    while rows % r or rows // r > nsteps or (r * cols * 4 < CAST_CHUNK_BYTES and r < rows):
        r += 2 * SUBLANES
    return r


def _cast_chunk(step, nsteps, layer, srcs, dsts, stages, halves, sem_in, sem_out):
    slot = step % 2
    every = range(len(srcs))
    rows = [st.shape[1] for st in stages]
    nchunk = [dsts[k].shape[0] // rows[k] for k in every]

    def in_copy(k, s, sl):
        return pltpu.make_async_copy(srcs[k].at[layer, pl.ds(pl.multiple_of(s * rows[k], rows[k]), rows[k]), :],
                                     stages[k].at[sl], sem_in.at[k, sl])

    def out_copy(k, s, sl):
        return pltpu.make_async_copy(halves[k].at[sl], dsts[k].at[pl.ds(pl.multiple_of(s * rows[k], rows[k]), rows[k]), :],
                                     sem_out.at[k, sl])

    def cast(k):
        in_copy(k, step, slot).wait()
        halves[k][slot] = stages[k][slot].astype(BF16)

    @pl.when(step == 0)
    def _():
        for k in every:
            in_copy(k, 0, 0).start()

    full = [k for k in every if nchunk[k] == nsteps]
    for nc in sorted(set(nchunk)):
        group = [k for k in every if nchunk[k] == nc]

        @pl.when(step + 1 < nc)
        def _():
            for k in group:
                in_copy(k, step + 1, 1 - slot).start()

        @pl.when((step >= 2) & (step < nc + 2))
        def _():
            for k in group:
                out_copy(k, step - 2, slot).wait()

        if nc < nsteps:
            @pl.when(step < nc)
            def _():
                for k in group:
                    cast(k)
                    out_copy(k, step, slot).start()

    for k in full:
        cast(k)

    def epilogue():
        for k in full:
            out_copy(k, step, slot).start()

        late = [(k, c) for k in every for c in range(max(nchunk[k] - 2, 0), nchunk[k]) if c + 2 > nsteps - 1]
        if late:
            @pl.when(step == nsteps - 1)
            def _():
                for k, c in late:
                    out_copy(k, c, c % 2).wait()

    return epilogue


def _call_with_cast(body, *, grid, in_specs, out_spec, out_shape, scratch, pipelined, temporaries, operands, name,
                    cast=None):
    layer, srcs = cast if cast is not None else (None, [])
    n, n_in, n_scr = len(srcs), len(in_specs), len(scratch)
    nsteps = grid[0] * grid[1]
    chunks = [(_cast_rows(w.shape[1], w.shape[2], nsteps), w.shape[2]) for w in srcs]
    cast_scratch = [((2,) + c, F32) for c in chunks] + [((2,) + c, BF16) for c in chunks]

    def wrapped(*refs):
        ins, srcs_r, out = refs[:n_in], refs[n_in:n_in + n], refs[n_in + n]
        dsts = refs[n_in + n + 1:n_in + 2 * n + 1]
        rest = refs[n_in + 2 * n + 1:]
        after = None
        if n:
            stages, halves, (sem_in, sem_out) = rest[n_scr:n_scr + n], rest[n_scr + n:n_scr + 2 * n], rest[-2:]
            step = pl.program_id(0) * grid[1] + pl.program_id(1)
            after = _cast_chunk(step, nsteps, layer, srcs_r, dsts, stages, halves, sem_in, sem_out)
        body(*ins, out, *rest[:n_scr])
        if after is not None:
            after()

    hbm = pl.BlockSpec(memory_space=pltpu.HBM)
    out = pl.pallas_call(
        wrapped,
        grid=grid,
        in_specs=list(in_specs) + [hbm] * n,
        out_specs=[out_spec] + [hbm] * n,
        out_shape=[out_shape] + [jax.ShapeDtypeStruct(w.shape[1:], BF16) for w in srcs],
        scratch_shapes=[pltpu.VMEM(sh, d) for sh, d in scratch + cast_scratch]
        + ([pltpu.SemaphoreType.DMA((n, 2))] * 2 if n else []),
        compiler_params=_params(("arbitrary", "arbitrary") if n else ("parallel", "arbitrary"),
                                pipelined, scratch + cast_scratch + temporaries),
        name=name,
    )(*operands, *srcs)
    return out[0], out[1:]


def _normmm_body(x_ref, g_ref, w_ref, o_ref, xn_ref):
    @pl.when(pl.program_id(1) == 0)
    def _():
        xn_ref[...] = _rms(x_ref[...], g_ref[...]).astype(BF16)

    o_ref[...] = _dot(xn_ref[...], w_ref[...]).astype(o_ref.dtype)


def _normmm(x, g, w, *, tm, tn, name, cast=None):
    m, k = x.shape
    n = w.shape[-1]
    return _call_with_cast(
        _normmm_body,
        grid=(m // tm, n // tn),
        in_specs=[pl.BlockSpec((tm, k), lambda i, j: (i, 0)),
                  pl.BlockSpec((1, k), lambda i, j: (0, 0)),
                  pl.BlockSpec((k, tn), lambda i, j: (0, j))],
        out_spec=pl.BlockSpec((tm, tn), lambda i, j: (i, j)),
        out_shape=jax.ShapeDtypeStruct((m, n), BF16),
        scratch=[((tm, k), BF16)],
        pipelined=[((tm, k), F32), ((1, k), F32), ((k, tn), BF16), ((tm, tn), BF16)],
        temporaries=[((tm, tn), F32)],
        operands=[x, g.reshape(1, k), w], name=name, cast=cast)


NA_RPB_ROWS = 2 * NA_ROWS - 1
NA_RPB_COLS = 2 * NA_COLS - 1
NA_TILE_ROWS = ((0, 0), (NA_QROWS, 0), (GRID_H - NA_QROWS, GRID_H - NA_KROWS))


def _na_tiles_body(rpb_ref, o_ref):
    shape = (GRID_W, LANES)
    lane = lax.broadcasted_iota(jnp.int32, shape, 1)
    cq = lax.broadcasted_iota(jnp.int32, shape, 0)
    ck = lane & (GRID_W - 1)
    c0 = jnp.clip(cq - NA_COLS // 2, 0, GRID_W - NA_COLS)
    col_ok = (ck >= c0) & (ck < c0 + NA_COLS)
    left = lane < GRID_W
    neg = jnp.full(shape, NEG, F32)
    lo, hi = [], []
    for a in range(NA_RPB_ROWS):
        row = jnp.broadcast_to(rpb_ref[a:a + 1, :], shape)
        lo.append(pltpu.roll(row, LANES - (NA_COLS - 1), 1, stride=1, stride_axis=0))
        hi.append(pltpu.roll(row, GRID_W - (NA_COLS - 1), 1, stride=1, stride_axis=0))
    for t, (qrow0, krow0) in enumerate(NA_TILE_ROWS):
        for rq in range(NA_QROWS):
            r = qrow0 + rq
            r0 = min(max(r - NA_ROWS // 2, 0), GRID_H - NA_ROWS)
            for pair in range(NA_KROWS // 2):
                halves = []
                for side, table in ((0, lo), (1, hi)):
                    rk = krow0 + 2 * pair + side
                    halves.append(table[rk - r + NA_ROWS - 1] if r0 <= rk < r0 + NA_ROWS else neg)
                blk = jnp.where(col_ok, jnp.where(left, halves[0], halves[1]) * LOG2E, NEG)
                o_ref[t, rq * GRID_W:(rq + 1) * GRID_W, pair * LANES:(pair + 1) * LANES] = blk


def _na_bias_tiles(rpb):
    padded = jnp.pad(rpb.astype(F32), ((0, 0), (0, 2 * SUBLANES - NA_RPB_ROWS), (0, LANES - NA_RPB_COLS)))
    return pl.pallas_call(
        _na_tiles_body,
        grid=(NA_HEADS,),
        in_specs=[pl.BlockSpec((None, 2 * SUBLANES, LANES), lambda h: (h, 0, 0))],
        out_specs=pl.BlockSpec((None, 3, NA_QBLK, NA_KBLK), lambda h: (h, 0, 0, 0)),
        out_shape=jax.ShapeDtypeStruct((NA_HEADS, 3, NA_QBLK, NA_KBLK), F32),
        compiler_params=_params(("parallel",), [((3, NA_QBLK, NA_KBLK), F32)], [((NA_QBLK, NA_KBLK), F32)]),
        name="na_bias_tiles",
    )(padded)


def _na_body(q_ref, k_ref, v_ref, bias_ref, o_ref, s_ref):
    def starts(i):
        qrow = i * NA_QROWS
        krow = jnp.clip(qrow - NA_ROWS // 2, 0, GRID_H - NA_KROWS)
        return pl.multiple_of(qrow * GRID_W, NA_QBLK), pl.multiple_of(krow * GRID_W, NA_QBLK)

    def logits(i, buf):
        qs, ks = starts(i)
        tile = jnp.where(i == 0, 0, jnp.where(i == NA_NBLK - 1, 2, 1))
        s_ref[buf] = (_dot_nt(q_ref[pl.ds(qs, NA_QBLK), :], k_ref[pl.ds(ks, NA_KBLK), :]) * QK_SCALE2
                      + bias_ref[tile])

    def finish(i, buf):
        qs, ks = starts(i)
        s = s_ref[buf]
        m = jnp.max(s, axis=-1, keepdims=True)
        p = jnp.exp2(s - m)
        l = jnp.sum(p, axis=-1, keepdims=True)
        o = _dot(p.astype(BF16), v_ref[pl.ds(ks, NA_KBLK), :]) * (1.0 / l)
        o_ref[pl.ds(qs, NA_QBLK), :] = o.astype(o_ref.dtype)

    logits(0, 0)

    def pair(ii, carry):
        i = 2 * ii
        logits(i + 1, 1)
        finish(i, 0)
        logits(jnp.minimum(i + 2, NA_NBLK - 1), 0)
        finish(i + 1, 1)
        return carry

    lax.fori_loop(0, NA_NBLK // 2, pair, 0)


def _na_attention(u, bias):
    qb, kb, vb = OFF_QA // HEAD_DIM, OFF_KA // HEAD_DIM, OFF_VA // HEAD_DIM
    blk = (SEQ, HEAD_DIM)
    return pl.pallas_call(
        _na_body,
        grid=(NA_HEADS, BATCH),
        in_specs=[pl.BlockSpec(blk, lambda h, b: (b, qb + h)),
                  pl.BlockSpec(blk, lambda h, b: (b, kb + h)),
                  pl.BlockSpec(blk, lambda h, b: (b, vb + h)),
                  pl.BlockSpec((None, 3, NA_QBLK, NA_KBLK), lambda h, b: (h, 0, 0, 0))],
        out_specs=pl.BlockSpec(blk, lambda h, b: (b, h)),
        out_shape=jax.ShapeDtypeStruct((M_TOK, NA_W), BF16),
        scratch_shapes=[pltpu.VMEM((2, NA_QBLK, NA_KBLK), F32)],
        compiler_params=_params(
            ("parallel", "parallel"),
            [(blk, BF16)] * 4 + [((3, NA_QBLK, NA_KBLK), F32)],
            [((NA_QBLK, NA_KBLK), F32)] * 3),
        name="na_attention",
    )(u, u, u, bias)


def _t5_bucket(rel):
    nb = T5_BUCKETS // 2
    ret = (rel > 0).astype(np.int32) * nb
    n = np.abs(rel)
    max_exact = nb // 2
    large = max_exact + (np.log(np.maximum(n, 1) / max_exact) / np.log(T5_MAX_DIST / max_exact)
                         * (nb - max_exact)).astype(np.int32)
    large = np.minimum(large, nb - 1)
    return (ret + np.where(n < max_exact, n, large)).astype(np.int32)


def _dil_tiles_body(f_ref, o_ref):
    for d in range(DIL_NDELTA):
        row = jnp.broadcast_to(f_ref[d:d + 1, :], (DIL_BLK, 2 * DIL_BLK))
        o_ref[d] = pltpu.roll(row, DIL_BLK, 1, stride=1, stride_axis=0)[:, :DIL_BLK]
    o_ref[DIL_NDELTA] = jnp.full((DIL_BLK, DIL_BLK), NEG, F32)


def _dil_bias_tiles(t5_bias):
    delta = np.arange(-DIL_REACH, DIL_REACH + 1)[:, None]
    off = delta * DIL_BLK + np.arange(-DIL_BLK, DIL_BLK)[None, :]
    count = np.zeros(off.shape, np.int32)
    for w, d in DIL_PAIRS:
        count += ((off % d == 0) & (np.abs(off) <= w // 2)).astype(np.int32)
    bucket = _t5_bucket(np.clip(off, -T5_MAX_DIST, T5_MAX_DIST))
    bias = jnp.take(t5_bias.T.astype(F32), jnp.asarray(bucket.reshape(-1)), axis=1).reshape((DIL_HEADS,) + off.shape)
    logc = jnp.log(jnp.asarray(np.maximum(count, 1), F32))
    f = jnp.where(jnp.asarray(count > 0)[None], (bias + logc[None]) * LOG2E, NEG)
    f = jnp.pad(f, ((0, 0), (0, 2 * SUBLANES - DIL_NDELTA), (0, 0)))
    return pl.pallas_call(
        _dil_tiles_body,
        grid=(DIL_HEADS,),
        in_specs=[pl.BlockSpec((None, 2 * SUBLANES, 2 * DIL_BLK), lambda h: (h, 0, 0))],
        out_specs=pl.BlockSpec((None, DIL_NDELTA + 1, DIL_BLK, DIL_BLK), lambda h: (h, 0, 0, 0)),
        out_shape=jax.ShapeDtypeStruct((DIL_HEADS, DIL_NDELTA + 1, DIL_BLK, DIL_BLK), F32),
        compiler_params=_params(("parallel",), [((DIL_NDELTA + 1, DIL_BLK, DIL_BLK), F32)],
                                [((DIL_BLK, 2 * DIL_BLK), F32)] * 2),
        name="dil_bias_tiles",
    )(f)


def _dil_body(q_ref, k_ref, v_ref, t_ref, o_ref, s_ref, vx_ref):
    vx_ref[:, :HEAD_DIM] = v_ref[...]
    lane = lax.broadcasted_iota(jnp.int32, (SEQ, HEAD_DIM), 1)
    vx_ref[:, HEAD_DIM:] = jnp.where(lane == 0, 1.0, 0.0).astype(BF16)

    def key_start(i, d):
        return pl.multiple_of(jnp.clip(i + (d - DIL_REACH), 0, DIL_NBLK - 1) * DIL_BLK, DIL_BLK)

    def logits(i, buf):
        q = q_ref[pl.ds(pl.multiple_of(i * DIL_BLK, DIL_BLK), DIL_BLK), :]
        for d in range(DIL_NDELTA):
            j = i + (d - DIL_REACH)
            tile = jnp.where((j >= 0) & (j < DIL_NBLK), d, DIL_NDELTA)
            s_ref[buf, :, d * DIL_BLK:(d + 1) * DIL_BLK] = (
                _dot_nt(q, k_ref[pl.ds(key_start(i, d), DIL_BLK), :]) * QK_SCALE2 + t_ref[tile])

    def finish(i, buf):
        half = DIL_BLK // 2
        for r0 in (0, half):
            m = jnp.max(s_ref[buf, r0:r0 + half, :], axis=-1, keepdims=True)
            acc = jnp.zeros((half, 2 * HEAD_DIM), F32)
            for d in range(DIL_NDELTA):
                p = jnp.exp2(s_ref[buf, r0:r0 + half, d * DIL_BLK:(d + 1) * DIL_BLK] - m)
                acc = acc + _dot(p.astype(BF16), vx_ref[pl.ds(key_start(i, d), DIL_BLK), :])
            out = acc[:, :HEAD_DIM] * (1.0 / acc[:, HEAD_DIM:HEAD_DIM + 1])
            o_ref[pl.ds(pl.multiple_of(i * DIL_BLK + r0, half), half), :] = out.astype(o_ref.dtype)

    logits(0, 0)

    def pair(ii, carry):
        i = 2 * ii
        logits(i + 1, 1)
        finish(i, 0)
        logits(jnp.minimum(i + 2, DIL_NBLK - 1), 0)
        finish(i + 1, 1)
        return carry

    lax.fori_loop(0, DIL_NBLK // 2, pair, 0)


def _dil_attention(u, tiles):
    qb, kb, vb = OFF_QC // HEAD_DIM, OFF_KC // HEAD_DIM, OFF_VC // HEAD_DIM
    blk = (SEQ, HEAD_DIM)
    return pl.pallas_call(
        _dil_body,
        grid=(DIL_HEADS, BATCH),
        in_specs=[pl.BlockSpec(blk, lambda h, b: (b, qb + h)),
                  pl.BlockSpec(blk, lambda h, b: (b, kb + h)),
                  pl.BlockSpec(blk, lambda h, b: (b, vb + h)),
                  pl.BlockSpec((None, DIL_NDELTA + 1, DIL_BLK, DIL_BLK), lambda h, b: (h, 0, 0, 0))],
        out_specs=pl.BlockSpec(blk, lambda h, b: (b, h)),
        out_shape=jax.ShapeDtypeStruct((M_TOK, DIL_W), BF16),
        scratch_shapes=[pltpu.VMEM((2, DIL_BLK, DIL_NDELTA * DIL_BLK), F32), pltpu.VMEM((SEQ, 2 * HEAD_DIM), BF16)],
        compiler_params=_params(
            ("parallel", "parallel"),
            [(blk, BF16)] * 4 + [((DIL_NDELTA + 1, DIL_BLK, DIL_BLK), F32)],
            [((DIL_BLK, DIL_NDELTA * DIL_BLK), F32)] * 3 + [((SEQ, 2 * HEAD_DIM), BF16)]),
        name="dil_attention",
    )(u, u, u, tiles)


def _rotary_tables():
    inv_freq = jnp.asarray((ROPE_BASE ** (-np.arange(0, RET_DK, 2, dtype=np.float32) / RET_DK)).astype(np.float32))
    ang = jnp.arange(SEQ, dtype=F32)[:, None] * inv_freq[None, :]
    cos, sin = jnp.cos(ang), jnp.sin(ang)
    return jnp.concatenate([cos, cos], axis=-1), jnp.concatenate([-sin, sin], axis=-1)


def _ret_tables(dec_ref, h, d_ref, xz_ref, gc_ref):
    c = RET_BLK
    row = lax.broadcasted_iota(jnp.int32, (c, c), 0)
    col = lax.broadcasted_iota(jnp.int32, (c, c), 1)
    diff = (row - col).astype(F32)
    log_f = -jnp.exp(jnp.full((c, c), dec_ref[0, h], F32))
    log_b = -jnp.exp(jnp.full((c, c), dec_ref[1, h], F32))
    d_ref[...] = jnp.where(diff >= 0.0, jnp.exp(jnp.maximum(diff, 0.0) * log_f),
                           jnp.exp(jnp.maximum(-diff, 0.0) * log_b))
    j = lax.broadcasted_iota(jnp.int32, (c, RET_DV), 0).astype(F32)
    log_f2 = -jnp.exp(jnp.full((c, RET_DV), dec_ref[0, h], F32))
    log_b2 = -jnp.exp(jnp.full((c, RET_DV), dec_ref[1, h], F32))
    xz_ref[0] = jnp.exp((j + 1.0) * log_f2)
    xz_ref[1] = jnp.exp((c - 1.0 - j) * log_f2)
    xz_ref[2] = jnp.exp((c - j) * log_b2)
    xz_ref[3] = jnp.exp(j * log_b2)
    gc_ref[0] = jnp.exp(c * -jnp.exp(jnp.full((RET_DK, RET_DV), dec_ref[0, h], F32)))
    gc_ref[1] = jnp.exp(c * -jnp.exp(jnp.full((RET_DK, RET_DV), dec_ref[1, h], F32)))


def _ret_body(dec_ref, q_ref, k_ref, v_ref, gate_ref, cos_ref, sin_ref, o_ref,
              qr_ref, kr_ref, y_ref, d_ref, xz_ref, gc_ref, sf_ref, sb_ref):
    c = RET_BLK
    nblk = SEQ // c
    _ret_tables(dec_ref, pl.program_id(0), d_ref, xz_ref, gc_ref)
    sf_ref[...] = jnp.zeros_like(sf_ref)
    sb_ref[...] = jnp.zeros_like(sb_ref)

    def rotate(x, cs, sn):
        return x * cs + pltpu.roll(x, RET_DK // 2, 1) * sn

    def up(i, carry):
        sl = pl.ds(pl.multiple_of(i * c, c), c)
        cs, sn = cos_ref[sl, :], sin_ref[sl, :]
        q = rotate(q_ref[sl, :].astype(F32), cs, sn).astype(BF16)
        k = (rotate(k_ref[sl, :].astype(F32), cs, sn) * (RET_DK ** -0.5)).astype(BF16)
        qr_ref[sl, :] = q
        kr_ref[sl, :] = k
        v = v_ref[sl, :]
        inner = (_dot_nt(q, k) * d_ref[...]).astype(BF16)
        state = sf_ref[...]
        y_ref[sl, :] = _dot(inner, v) + _dot(q, state.astype(BF16)) * xz_ref[0]
        sf_ref[...] = state * gc_ref[0] + _dot_tn(k, (v.astype(F32) * xz_ref[1]).astype(BF16))
        return carry

    lax.fori_loop(0, nblk, up, 0, unroll=4)

    def down(ii, carry):
        sl = pl.ds(pl.multiple_of((nblk - 1 - ii) * c, c), c)
        q, k, v = qr_ref[sl, :], kr_ref[sl, :], v_ref[sl, :]
        state = sb_ref[...]
        y = y_ref[sl, :] + _dot(q, state.astype(BF16)) * xz_ref[2]
        sb_ref[...] = state * gc_ref[1] + _dot_tn(k, (v.astype(F32) * xz_ref[3]).astype(BF16))
        mu = jnp.mean(y, axis=-1, keepdims=True)
        yc = y - mu
        var = jnp.mean(yc * yc, axis=-1, keepdims=True)
        gate = gate_ref[sl, :].astype(F32)
        o_ref[sl, :] = (gate * _sigmoid(gate) * (yc * lax.rsqrt(var + EPS))).astype(o_ref.dtype)
        return carry

    lax.fori_loop(0, nblk, down, 0, unroll=4)


def _retention(u, decay, cos2, sin2):
    qb, kb = OFF_QB // RET_DK, OFF_KB // RET_DK
    vb, gb = OFF_VB // RET_DV, OFF_GB // RET_DV
    qk_blk, v_blk = (SEQ, RET_DK), (SEQ, RET_DV)
    scratch = [(qk_blk, BF16), (qk_blk, BF16), (v_blk, F32), ((RET_BLK, RET_BLK), F32),
               ((4, RET_BLK, RET_DV), F32), ((2, RET_DK, RET_DV), F32),
               ((RET_DK, RET_DV), F32), ((RET_DK, RET_DV), F32)]
    return pl.pallas_call(
        _ret_body,
        grid=(RET_HEADS, BATCH),
        in_specs=[pl.BlockSpec(memory_space=pltpu.SMEM),
                  pl.BlockSpec(qk_blk, lambda h, b: (b, qb + h)),
                  pl.BlockSpec(qk_blk, lambda h, b: (b, kb + h)),
                  pl.BlockSpec(v_blk, lambda h, b: (b, vb + h)),
                  pl.BlockSpec(v_blk, lambda h, b: (b, gb + h)),
                  pl.BlockSpec(qk_blk, lambda h, b: (0, 0)),
                  pl.BlockSpec(qk_blk, lambda h, b: (0, 0))],
        out_specs=pl.BlockSpec(v_blk, lambda h, b: (b, h)),
        out_shape=jax.ShapeDtypeStruct((M_TOK, RET_V_W), BF16),
        scratch_shapes=[pltpu.VMEM(s, d) for s, d in scratch],
        compiler_params=_params(
            ("parallel", "parallel"),
            [(qk_blk, BF16)] * 2 + [(v_blk, BF16)] * 3 + [(qk_blk, F32)] * 2,
            scratch + [((RET_BLK, RET_DV), F32)] * 4),
        name="retention",
    )(decay, u, u, u, u, cos2, sin2)


def _postmix_body(oa_ref, ob_ref, oc_ref, wb_ref, sa_ref, sb_ref, sc_ref, wo_ref, x_ref,
                  g_ref, wq_ref, kv_ref, wco_ref, o_ref):
    def branch(o, row0, s):
        return _sigmoid(s[...].astype(F32)) * _dot(o[...], wb_ref[row0:row0 + o.shape[1], :])

    merged = branch(oa_ref, 0, sa_ref) + branch(ob_ref, NA_W, sb_ref) + branch(oc_ref, NA_W + RET_V_W, sc_ref)
    x1 = x_ref[...] + _dot(merged.astype(BF16), wo_ref[...])
    q = _dot(_rms(x1, g_ref[...]).astype(BF16), wq_ref[...]).astype(BF16)
    heads = []
    for h in range(CROSS_HEADS):
        lo = h * HEAD_DIM
        k = kv_ref[:, lo:lo + HEAD_DIM]
        v = kv_ref[:, CROSS_W + lo:CROSS_W + lo + HEAD_DIM]
        s = _dot_nt(q[:, lo:lo + HEAD_DIM], k) * QK_SCALE2
        m = jnp.max(s, axis=-1, keepdims=True)
        p = jnp.exp2(s - m)
        l = jnp.sum(p, axis=-1, keepdims=True)
        heads.append((_dot(p.astype(BF16), v) * (1.0 / l)).astype(BF16))
    o_ref[...] = x1 + _dot(jnp.concatenate(heads, axis=-1), wco_ref[...])


def _postmix(o_a, o_b, o_c, w_branch, u, w_out, x, g, w_cq, kv, w_co, *, tm):
    mix_w = NA_W + RET_V_W + DIL_W
    per_batch = SEQ // tm
    once = pl.Buffered(1)

    def rows(k):
        return pl.BlockSpec((tm, k), lambda i: (i, 0))

    def gate(off):
        return pl.BlockSpec((pl.Element(tm), pl.Element(D_MODEL)), lambda i: (i * tm, off))

    def weight(r, c):
        return pl.BlockSpec((r, c), lambda i: (0, 0), pipeline_mode=once)

    return pl.pallas_call(
        _postmix_body,
        grid=(M_TOK // tm,),
        in_specs=[rows(NA_W), rows(RET_V_W), rows(DIL_W), weight(mix_w, D_MODEL),
                  gate(OFF_SA), gate(OFF_SB), gate(OFF_SC), weight(D_MODEL, D_MODEL), rows(D_MODEL),
                  pl.BlockSpec((1, D_MODEL), lambda i: (0, 0), pipeline_mode=once),
                  weight(D_MODEL, CROSS_W),
                  pl.BlockSpec((MEM_LEN, 2 * CROSS_W), lambda i: (i // per_batch, 0)),
                  weight(CROSS_W, D_MODEL)],
        out_specs=rows(D_MODEL),
        out_shape=jax.ShapeDtypeStruct((M_TOK, D_MODEL), F32),
        compiler_params=_params(
            ("parallel",),
            [((tm, mix_w), BF16)] + [((tm, D_MODEL), BF16)] * 3 + [((tm, D_MODEL), F32)] * 2
            + [((MEM_LEN, 2 * CROSS_W), BF16)],
            [((mix_w + D_MODEL + 2 * CROSS_W, D_MODEL), BF16)] + [((tm, D_MODEL), F32)] * 6),
        name="post_mix",
    )(o_a, o_b, o_c, w_branch, u, u, u, w_out, x, g.reshape(1, D_MODEL), w_cq, kv, w_co)


def _mlp_body(*refs, final):
    if final:
        x_ref, g_ref, w1_ref, w2_ref, gf_ref, o_ref, xn_ref = refs
    else:
        x_ref, g_ref, w1_ref, w2_ref, o_ref, xn_ref = refs

    @pl.when(pl.program_id(1) == 0)
    def _():
        x = x_ref[...]
        xn_ref[...] = _rms(x, g_ref[...]).astype(BF16)
        o_ref[...] = x

    hid = jnp.square(jnp.maximum(_dot(xn_ref[...], w1_ref[...]), 0.0)).astype(BF16)
    o_ref[...] += _dot(hid, w2_ref[...])

    if final:
        @pl.when(pl.program_id(1) == pl.num_programs(1) - 1)
        def _():
            o_ref[...] = _rms(o_ref[...], gf_ref[...])


def _mlp(x, g, w1, w2, *, tm, tf, cast=None, final_g=None):
    m, k = x.shape
    vec = pl.BlockSpec((1, k), lambda i, f: (0, 0))
    final = final_g is not None
    return _call_with_cast(
        functools.partial(_mlp_body, final=final),
        grid=(m // tm, D_FF // tf),
        in_specs=[pl.BlockSpec((tm, k), lambda i, f: (i, 0)), vec,
                  pl.BlockSpec((k, tf), lambda i, f: (0, f)),
                  pl.BlockSpec((tf, k), lambda i, f: (f, 0))] + ([vec] if final else []),
        out_spec=pl.BlockSpec((tm, k), lambda i, f: (i, 0)),
        out_shape=jax.ShapeDtypeStruct((m, k), F32),
        scratch=[((tm, k), BF16)],
        pipelined=[((tm, k), F32), ((1, k), F32), ((k, tf), BF16), ((tf, k), BF16), ((tm, k), F32)],
        temporaries=[((tm, tf), F32), ((tm, tf), BF16), ((tm, k), F32)],
        operands=[x, g.reshape(1, k), w1, w2] + ([final_g.reshape(1, k)] if final else []), name="mlp", cast=cast)


def kernel(x, mem, t5_bias, norm_mix_g, w_in, na_rpb, ret_decay, w_branch, w_out, norm_cross_g, norm_mem_g,
           w_cq, w_ckv, w_co, norm_mlp_g, w_mlp1, w_mlp2, final_norm_g):
    xs = x.reshape(M_TOK, D_MODEL)
    mem2 = mem.reshape(BATCH * MEM_LEN, D_MODEL)
    stacked = [w_in, w_mlp1, w_mlp2, w_branch, w_out, w_cq, w_ckv, w_co]
    w_in_h = w_in[0].astype(BF16)
    cos2, sin2 = _rotary_tables()
    dil_tiles = _dil_bias_tiles(t5_bias)

    for layer in range(DEPTH):
        u, rest = _normmm(xs, norm_mix_g[layer], w_in_h, tm=1024, tn=1536, name="in_proj",
                          cast=(layer, stacked[1:]))
        w_mlp1_h, w_mlp2_h, w_branch_h, w_out_h, w_cq_h, w_ckv_h, w_co_h = rest
        o_a = _na_attention(u, _na_bias_tiles(na_rpb[layer]))
        o_b = _retention(u, ret_decay[layer], cos2, sin2)
        o_c = _dil_attention(u, dil_tiles)
        kv, _ = _normmm(mem2, norm_mem_g[layer], w_ckv_h, tm=BATCH * MEM_LEN, tn=512, name="mem_kv_proj")
        xs = _postmix(o_a, o_b, o_c, w_branch_h, u, w_out_h, xs, norm_cross_g[layer], w_cq_h, kv, w_co_h, tm=256)
        last = layer + 1 == DEPTH
        xs, half = _mlp(xs, norm_mlp_g[layer], w_mlp1_h, w_mlp2_h, tm=512, tf=1024,
                        cast=None if last else (layer + 1, stacked[:1]), final_g=final_norm_g if last else None)
        if not last:
            w_in_h, = half
    return xs.reshape(BATCH, SEQ, D_MODEL)
```

```python
import functools

import numpy as np
import jax
import jax.numpy as jnp
from jax import lax
from jax.experimental import pallas as pl
from jax.experimental.pallas import tpu as pltpu

D_MODEL = 2048
BATCH = 2
SEQ = 4096
DEPTH = 4
MEM_LEN = 256
HEAD_DIM = 128
GRID_W = 64
NA_HEADS = 6
NA_ROWS = 8
NA_COLS = 16
RET_HEADS = 4
RET_DK = 128
RET_DV = 256
RET_BLK = 256
ROPE_BASE = 10000.0
DIL_HEADS = 6
DIL_PAIRS = ((128, 1), (512, 4), (2048, 16))
T5_BUCKETS = 32
T5_MAX_DIST = 1024
CROSS_HEADS = 4
D_FF = 4 * D_MODEL
EPS = 1e-6
NEG = -1e30

NA_W = NA_HEADS * HEAD_DIM
RET_QK_W = RET_HEADS * RET_DK
RET_V_W = RET_HEADS * RET_DV
DIL_W = DIL_HEADS * HEAD_DIM
CROSS_W = CROSS_HEADS * HEAD_DIM
IN_SPLITS = (NA_W, NA_W, NA_W, RET_QK_W, RET_QK_W, RET_V_W, RET_V_W,
             DIL_W, DIL_W, DIL_W, D_MODEL, D_MODEL, D_MODEL)
IN_W = sum(IN_SPLITS)
(OFF_QA, OFF_KA, OFF_VA, OFF_QB, OFF_KB, OFF_VB, OFF_GB,
 OFF_QC, OFF_KC, OFF_VC, OFF_SA, OFF_SB, OFF_SC) = [int(o) for o in np.cumsum((0,) + IN_SPLITS[:-1])]

M_TOK = BATCH * SEQ
ATT_SCALE = HEAD_DIM ** -0.5
LOG2E = float(np.log2(np.e))
QK_SCALE2 = ATT_SCALE * LOG2E

V7X_VMEM_BYTES = 64 * 1024 * 1024
LANES = 128
SUBLANES = 8

NA_QROWS = 4
NA_KROWS = NA_QROWS + NA_ROWS
NA_QBLK = NA_QROWS * GRID_W
NA_KBLK = NA_KROWS * GRID_W
GRID_H = SEQ // GRID_W
NA_NBLK = GRID_H // NA_QROWS

DIL_BLK = 256
DIL_REACH = (max(w for w, _ in DIL_PAIRS) // 2 + DIL_BLK - 1) // DIL_BLK
DIL_NDELTA = 2 * DIL_REACH + 1
DIL_NBLK = SEQ // DIL_BLK

F32 = jnp.float32
BF16 = jnp.bfloat16


def _nbytes(shape, dtype):
    return int(np.prod(shape)) * jnp.dtype(dtype).itemsize


def _params(semantics, pipelined, resident):
    need = 2 * sum(_nbytes(s, d) for s, d in pipelined) + sum(_nbytes(s, d) for s, d in resident)
    assert need < V7X_VMEM_BYTES, need
    return pltpu.CompilerParams(dimension_semantics=semantics,
                                vmem_limit_bytes=min(V7X_VMEM_BYTES, need + need // 4))


def _rms(x, g):
    ms = jnp.mean(x * x, axis=-1, keepdims=True)
    return x * lax.rsqrt(ms + EPS) * g


def _sigmoid(x):
    return 1.0 / (1.0 + jnp.exp(-x))


def _dot(a, b):
    return jnp.dot(a, b, preferred_element_type=F32)


def _dot_nt(a, b):
    return lax.dot_general(a, b, (((1,), (1,)), ((), ())), preferred_element_type=F32)


def _dot_tn(a, b):
    return lax.dot_general(a, b, (((0,), (0,)), ((), ())), preferred_element_type=F32)


CAST_CHUNK_BYTES = 1 << 19


def _cast_rows(rows, cols, nsteps):
    r = 2 * SUBLANES
    while rows % r or rows // r > nsteps or (r * cols * 4 < CAST_CHUNK_BYTES and r < rows):
        r += 2 * SUBLANES
    return r


def _cast_chunk(step, nsteps, layer, srcs, dsts, stages, halves, sem_in, sem_out):
    slot = step % 2
    every = range(len(srcs))
    rows = [st.shape[1] for st in stages]
    nchunk = [dsts[k].shape[0] // rows[k] for k in every]

    def in_copy(k, s, sl):
        return pltpu.make_async_copy(srcs[k].at[layer, pl.ds(pl.multiple_of(s * rows[k], rows[k]), rows[k]), :],
                                     stages[k].at[sl], sem_in.at[k, sl])

    def out_copy(k, s, sl):
        return pltpu.make_async_copy(halves[k].at[sl], dsts[k].at[pl.ds(pl.multiple_of(s * rows[k], rows[k]), rows[k]), :],
                                     sem_out.at[k, sl])

    def cast(k):
        in_copy(k, step, slot).wait()
        halves[k][slot] = stages[k][slot].astype(BF16)

    @pl.when(step == 0)
    def _():
        for k in every:
            in_copy(k, 0, 0).start()

    full = [k for k in every if nchunk[k] == nsteps]
    for nc in sorted(set(nchunk)):
        group = [k for k in every if nchunk[k] == nc]

        @pl.when(step + 1 < nc)
        def _():
            for k in group:
                in_copy(k, step + 1, 1 - slot).start()

        @pl.when((step >= 2) & (step < nc + 2))
        def _():
            for k in group:
                out_copy(k, step - 2, slot).wait()

        if nc < nsteps:
            @pl.when(step < nc)
            def _():
                for k in group:
                    cast(k)
                    out_copy(k, step, slot).start()

    for k in full:
        cast(k)

    def epilogue():
        for k in full:
            out_copy(k, step, slot).start()

        late = [(k, c) for k in every for c in range(max(nchunk[k] - 2, 0), nchunk[k]) if c + 2 > nsteps - 1]
        if late:
            @pl.when(step == nsteps - 1)
            def _():
                for k, c in late:
                    out_copy(k, c, c % 2).wait()

    return epilogue


def _call_with_cast(body, *, grid, in_specs, out_spec, out_shape, scratch, pipelined, temporaries, operands, name,
                    cast=None):
    layer, srcs = cast if cast is not None else (None, [])
    n, n_in, n_scr = len(srcs), len(in_specs), len(scratch)
    nsteps = grid[0] * grid[1]
    chunks = [(_cast_rows(w.shape[1], w.shape[2], nsteps), w.shape[2]) for w in srcs]
    cast_scratch = [((2,) + c, F32) for c in chunks] + [((2,) + c, BF16) for c in chunks]

    def wrapped(*refs):
        ins, srcs_r, out = refs[:n_in], refs[n_in:n_in + n], refs[n_in + n]
        dsts = refs[n_in + n + 1:n_in + 2 * n + 1]
        rest = refs[n_in + 2 * n + 1:]
        after = None
        if n:
            stages, halves, (sem_in, sem_out) = rest[n_scr:n_scr + n], rest[n_scr + n:n_scr + 2 * n], rest[-2:]
            step = pl.program_id(0) * grid[1] + pl.program_id(1)
            after = _cast_chunk(step, nsteps, layer, srcs_r, dsts, stages, halves, sem_in, sem_out)
        body(*ins, out, *rest[:n_scr])
        if after is not None:
            after()

    hbm = pl.BlockSpec(memory_space=pltpu.HBM)
    out = pl.pallas_call(
        wrapped,
        grid=grid,
        in_specs=list(in_specs) + [hbm] * n,
        out_specs=[out_spec] + [hbm] * n,
        out_shape=[out_shape] + [jax.ShapeDtypeStruct(w.shape[1:], BF16) for w in srcs],
        scratch_shapes=[pltpu.VMEM(sh, d) for sh, d in scratch + cast_scratch]
        + ([pltpu.SemaphoreType.DMA((n, 2))] * 2 if n else []),
        compiler_params=_params(("arbitrary", "arbitrary") if n else ("parallel", "arbitrary"),
                                pipelined, scratch + cast_scratch + temporaries),
        name=name,
    )(*operands, *srcs)
    return out[0], out[1:]


def _normmm_body(x_ref, g_ref, w_ref, o_ref, xn_ref):
    @pl.when(pl.program_id(1) == 0)
    def _():
        xn_ref[...] = _rms(x_ref[...], g_ref[...]).astype(BF16)

    o_ref[...] = _dot(xn_ref[...], w_ref[...]).astype(o_ref.dtype)


def _normmm(x, g, w, *, tm, tn, name, cast=None):
    m, k = x.shape
    n = w.shape[-1]
    return _call_with_cast(
        _normmm_body,
        grid=(m // tm, n // tn),
        in_specs=[pl.BlockSpec((tm, k), lambda i, j: (i, 0)),
                  pl.BlockSpec((1, k), lambda i, j: (0, 0)),
                  pl.BlockSpec((k, tn), lambda i, j: (0, j))],
        out_spec=pl.BlockSpec((tm, tn), lambda i, j: (i, j)),
        out_shape=jax.ShapeDtypeStruct((m, n), BF16),
        scratch=[((tm, k), BF16)],
        pipelined=[((tm, k), F32), ((1, k), F32), ((k, tn), BF16), ((tm, tn), BF16)],
        temporaries=[((tm, tn), F32)],
        operands=[x, g.reshape(1, k), w], name=name, cast=cast)


NA_RPB_ROWS = 2 * NA_ROWS - 1
NA_RPB_COLS = 2 * NA_COLS - 1
NA_TILE_ROWS = ((0, 0), (NA_QROWS, 0), (GRID_H - NA_QROWS, GRID_H - NA_KROWS))


def _na_tiles_body(rpb_ref, o_ref):
    shape = (GRID_W, LANES)
    lane = lax.broadcasted_iota(jnp.int32, shape, 1)
    cq = lax.broadcasted_iota(jnp.int32, shape, 0)
    ck = lane & (GRID_W - 1)
    c0 = jnp.clip(cq - NA_COLS // 2, 0, GRID_W - NA_COLS)
    col_ok = (ck >= c0) & (ck < c0 + NA_COLS)
    left = lane < GRID_W
    neg = jnp.full(shape, NEG, F32)
    lo, hi = [], []
    for a in range(NA_RPB_ROWS):
        row = jnp.broadcast_to(rpb_ref[a:a + 1, :], shape)
        lo.append(pltpu.roll(row, LANES - (NA_COLS - 1), 1, stride=1, stride_axis=0))
        hi.append(pltpu.roll(row, GRID_W - (NA_COLS - 1), 1, stride=1, stride_axis=0))
    for t, (qrow0, krow0) in enumerate(NA_TILE_ROWS):
        for rq in range(NA_QROWS):
            r = qrow0 + rq
            r0 = min(max(r - NA_ROWS // 2, 0), GRID_H - NA_ROWS)
            for pair in range(NA_KROWS // 2):
                halves = []
                for side, table in ((0, lo), (1, hi)):
                    rk = krow0 + 2 * pair + side
                    halves.append(table[rk - r + NA_ROWS - 1] if r0 <= rk < r0 + NA_ROWS else neg)
                blk = jnp.where(col_ok, jnp.where(left, halves[0], halves[1]) * LOG2E, NEG)
                o_ref[t, rq * GRID_W:(rq + 1) * GRID_W, pair * LANES:(pair + 1) * LANES] = blk


def _na_body(q_ref, k_ref, v_ref, rpb_ref, o_ref, s_ref, bias_ref):
    @pl.when(pl.program_id(1) == 0)
    def _():
        _na_tiles_body(rpb_ref, bias_ref)

    def starts(i):
        qrow = i * NA_QROWS
        krow = jnp.clip(qrow - NA_ROWS // 2, 0, GRID_H - NA_KROWS)
        return pl.multiple_of(qrow * GRID_W, NA_QBLK), pl.multiple_of(krow * GRID_W, NA_QBLK)

    def logits(i, buf):
        qs, ks = starts(i)
        tile = jnp.where(i == 0, 0, jnp.where(i == NA_NBLK - 1, 2, 1))
        s_ref[buf] = (_dot_nt(q_ref[pl.ds(qs, NA_QBLK), :], k_ref[pl.ds(ks, NA_KBLK), :]) * QK_SCALE2
                      + bias_ref[tile])

    def finish(i, buf):
        qs, ks = starts(i)
        s = s_ref[buf]
        m = jnp.max(s, axis=-1, keepdims=True)
        p = jnp.exp2(s - m)
        l = jnp.sum(p, axis=-1, keepdims=True)
        o = _dot(p.astype(BF16), v_ref[pl.ds(ks, NA_KBLK), :]) * (1.0 / l)
        o_ref[pl.ds(qs, NA_QBLK), :] = o.astype(o_ref.dtype)

    logits(0, 0)

    def pair(ii, carry):
        i = 2 * ii
        logits(i + 1, 1)
        finish(i, 0)
        logits(jnp.minimum(i + 2, NA_NBLK - 1), 0)
        finish(i + 1, 1)
        return carry

    lax.fori_loop(0, NA_NBLK // 2, pair, 0)


def _na_attention(u, rpb):
    padded = jnp.pad(rpb.astype(F32), ((0, 0), (0, 2 * SUBLANES - NA_RPB_ROWS), (0, LANES - NA_RPB_COLS)))
    qb, kb, vb = OFF_QA // HEAD_DIM, OFF_KA // HEAD_DIM, OFF_VA // HEAD_DIM
    blk = (SEQ, HEAD_DIM)
    return pl.pallas_call(
        _na_body,
        grid=(NA_HEADS, BATCH),
        in_specs=[pl.BlockSpec(blk, lambda h, b: (b, qb + h)),
                  pl.BlockSpec(blk, lambda h, b: (b, kb + h)),
                  pl.BlockSpec(blk, lambda h, b: (b, vb + h)),
                  pl.BlockSpec((None, 2 * SUBLANES, LANES), lambda h, b: (h, 0, 0))],
        out_specs=pl.BlockSpec(blk, lambda h, b: (b, h)),
        out_shape=jax.ShapeDtypeStruct((M_TOK, NA_W), BF16),
        scratch_shapes=[pltpu.VMEM((2, NA_QBLK, NA_KBLK), F32), pltpu.VMEM((3, NA_QBLK, NA_KBLK), F32)],
        compiler_params=_params(
            ("arbitrary", "arbitrary"),
            [(blk, BF16)] * 4,
            [((NA_QBLK, NA_KBLK), F32)] * 8),
        name="na_attention",
    )(u, u, u, padded)


def _t5_bucket(rel):
    nb = T5_BUCKETS // 2
    ret = (rel > 0).astype(np.int32) * nb
    n = np.abs(rel)
    max_exact = nb // 2
    large = max_exact + (np.log(np.maximum(n, 1) / max_exact) / np.log(T5_MAX_DIST / max_exact)
                         * (nb - max_exact)).astype(np.int32)
    large = np.minimum(large, nb - 1)
    return (ret + np.where(n < max_exact, n, large)).astype(np.int32)


def _dil_tiles_body(f_ref, o_ref):
    for d in range(DIL_NDELTA):
        row = jnp.broadcast_to(f_ref[d:d + 1, :], (DIL_BLK, 2 * DIL_BLK))
        o_ref[d] = pltpu.roll(row, DIL_BLK, 1, stride=1, stride_axis=0)[:, :DIL_BLK]
    o_ref[DIL_NDELTA] = jnp.full((DIL_BLK, DIL_BLK), NEG, F32)


def _dil_offset_rows(t5_bias):
    delta = np.arange(-DIL_REACH, DIL_REACH + 1)[:, None]
    off = delta * DIL_BLK + np.arange(-DIL_BLK, DIL_BLK)[None, :]
    count = np.zeros(off.shape, np.int32)
    for w, d in DIL_PAIRS:
        count += ((off % d == 0) & (np.abs(off) <= w // 2)).astype(np.int32)
    bucket = _t5_bucket(np.clip(off, -T5_MAX_DIST, T5_MAX_DIST))
    bias = jnp.take(t5_bias.T.astype(F32), jnp.asarray(bucket.reshape(-1)), axis=1).reshape((DIL_HEADS,) + off.shape)
    logc = jnp.log(jnp.asarray(np.maximum(count, 1), F32))
    f = jnp.where(jnp.asarray(count > 0)[None], (bias + logc[None]) * LOG2E, NEG)
    return jnp.pad(f, ((0, 0), (0, 2 * SUBLANES - DIL_NDELTA), (0, 0)))


def _dil_body(q_ref, k_ref, v_ref, f_ref, o_ref, s_ref, vx_ref, t_ref):
    @pl.when(pl.program_id(1) == 0)
    def _():
        _dil_tiles_body(f_ref, t_ref)

    vx_ref[:, :HEAD_DIM] = v_ref[...]
    lane = lax.broadcasted_iota(jnp.int32, (SEQ, HEAD_DIM), 1)
    vx_ref[:, HEAD_DIM:] = jnp.where(lane == 0, 1.0, 0.0).astype(BF16)

    def key_start(i, d):
        return pl.multiple_of(jnp.clip(i + (d - DIL_REACH), 0, DIL_NBLK - 1) * DIL_BLK, DIL_BLK)

    def logits(i, buf):
        q = q_ref[pl.ds(pl.multiple_of(i * DIL_BLK, DIL_BLK), DIL_BLK), :]
        for d in range(DIL_NDELTA):
            j = i + (d - DIL_REACH)
            tile = jnp.where((j >= 0) & (j < DIL_NBLK), d, DIL_NDELTA)
            s_ref[buf, :, d * DIL_BLK:(d + 1) * DIL_BLK] = (
                _dot_nt(q, k_ref[pl.ds(key_start(i, d), DIL_BLK), :]) * QK_SCALE2 + t_ref[tile])

    def finish(i, buf):
        half = DIL_BLK // 2
        for r0 in (0, half):
            m = jnp.max(s_ref[buf, r0:r0 + half, :], axis=-1, keepdims=True)
            acc = jnp.zeros((half, 2 * HEAD_DIM), F32)
            for d in range(DIL_NDELTA):
                p = jnp.exp2(s_ref[buf, r0:r0 + half, d * DIL_BLK:(d + 1) * DIL_BLK] - m)
                acc = acc + _dot(p.astype(BF16), vx_ref[pl.ds(key_start(i, d), DIL_BLK), :])
            out = acc[:, :HEAD_DIM] * (1.0 / acc[:, HEAD_DIM:HEAD_DIM + 1])
            o_ref[pl.ds(pl.multiple_of(i * DIL_BLK + r0, half), half), :] = out.astype(o_ref.dtype)

    logits(0, 0)

    def pair(ii, carry):
        i = 2 * ii
        logits(i + 1, 1)
        finish(i, 0)
        logits(jnp.minimum(i + 2, DIL_NBLK - 1), 0)
        finish(i + 1, 1)
        return carry

    lax.fori_loop(0, DIL_NBLK // 2, pair, 0)


def _dil_attention(u, offset_rows):
    qb, kb, vb = OFF_QC // HEAD_DIM, OFF_KC // HEAD_DIM, OFF_VC // HEAD_DIM
    blk = (SEQ, HEAD_DIM)
    return pl.pallas_call(
        _dil_body,
        grid=(DIL_HEADS, BATCH),
        in_specs=[pl.BlockSpec(blk, lambda h, b: (b, qb + h)),
                  pl.BlockSpec(blk, lambda h, b: (b, kb + h)),
                  pl.BlockSpec(blk, lambda h, b: (b, vb + h)),
                  pl.BlockSpec((None, 2 * SUBLANES, 2 * DIL_BLK), lambda h, b: (h, 0, 0))],
        out_specs=pl.BlockSpec(blk, lambda h, b: (b, h)),
        out_shape=jax.ShapeDtypeStruct((M_TOK, DIL_W), BF16),
        scratch_shapes=[pltpu.VMEM((2, DIL_BLK, DIL_NDELTA * DIL_BLK), F32), pltpu.VMEM((SEQ, 2 * HEAD_DIM), BF16),
                        pltpu.VMEM((DIL_NDELTA + 1, DIL_BLK, DIL_BLK), F32)],
        compiler_params=_params(
            ("arbitrary", "arbitrary"),
            [(blk, BF16)] * 4,
            [((DIL_BLK, DIL_NDELTA * DIL_BLK), F32)] * 3 + [((SEQ, 2 * HEAD_DIM), BF16),
                                                            ((DIL_NDELTA + 1, DIL_BLK, DIL_BLK), F32),
                                                            ((DIL_BLK, 2 * DIL_BLK), F32)]),
        name="dil_attention",
    )(u, u, u, offset_rows)


def _rotary_tables():
    inv_freq = jnp.asarray((ROPE_BASE ** (-np.arange(0, RET_DK, 2, dtype=np.float32) / RET_DK)).astype(np.float32))
    ang = jnp.arange(SEQ, dtype=F32)[:, None] * inv_freq[None, :]
    cos, sin = jnp.cos(ang), jnp.sin(ang)
    return jnp.concatenate([cos, cos], axis=-1), jnp.concatenate([-sin, sin], axis=-1)


def _ret_tables(dec_ref, h, d_ref, xz_ref, gc_ref):
    c = RET_BLK
    row = lax.broadcasted_iota(jnp.int32, (c, c), 0)
    col = lax.broadcasted_iota(jnp.int32, (c, c), 1)
    diff = (row - col).astype(F32)
    log_f = -jnp.exp(jnp.full((c, c), dec_ref[0, h], F32))
    log_b = -jnp.exp(jnp.full((c, c), dec_ref[1, h], F32))
    d_ref[...] = jnp.where(diff >= 0.0, jnp.exp(jnp.maximum(diff, 0.0) * log_f),
                           jnp.exp(jnp.maximum(-diff, 0.0) * log_b))
    j = lax.broadcasted_iota(jnp.int32, (c, RET_DV), 0).astype(F32)
    log_f2 = -jnp.exp(jnp.full((c, RET_DV), dec_ref[0, h], F32))
    log_b2 = -jnp.exp(jnp.full((c, RET_DV), dec_ref[1, h], F32))
    xz_ref[0] = jnp.exp((j + 1.0) * log_f2)
    xz_ref[1] = jnp.exp((c - 1.0 - j) * log_f2)
    xz_ref[2] = jnp.exp((c - j) * log_b2)
    xz_ref[3] = jnp.exp(j * log_b2)
    gc_ref[0] = jnp.exp(c * -jnp.exp(jnp.full((RET_DK, RET_DV), dec_ref[0, h], F32)))
    gc_ref[1] = jnp.exp(c * -jnp.exp(jnp.full((RET_DK, RET_DV), dec_ref[1, h], F32)))


def _ret_body(dec_ref, q_ref, k_ref, v_ref, gate_ref, cos_ref, sin_ref, o_ref,
              qr_ref, kr_ref, y_ref, d_ref, xz_ref, gc_ref, sf_ref, sb_ref):
    c = RET_BLK
    nblk = SEQ // c
    _ret_tables(dec_ref, pl.program_id(0), d_ref, xz_ref, gc_ref)
    sf_ref[...] = jnp.zeros_like(sf_ref)
    sb_ref[...] = jnp.zeros_like(sb_ref)

    def rotate(x, cs, sn):
        return x * cs + pltpu.roll(x, RET_DK // 2, 1) * sn

    def up(i, carry):
        sl = pl.ds(pl.multiple_of(i * c, c), c)
        cs, sn = cos_ref[sl, :], sin_ref[sl, :]
        q = rotate(q_ref[sl, :].astype(F32), cs, sn).astype(BF16)
        k = (rotate(k_ref[sl, :].astype(F32), cs, sn) * (RET_DK ** -0.5)).astype(BF16)
        qr_ref[sl, :] = q
        kr_ref[sl, :] = k
        v = v_ref[sl, :]
        inner = (_dot_nt(q, k) * d_ref[...]).astype(BF16)
        state = sf_ref[...]
        y_ref[sl, :] = _dot(inner, v) + _dot(q, state.astype(BF16)) * xz_ref[0]
        sf_ref[...] = state * gc_ref[0] + _dot_tn(k, (v.astype(F32) * xz_ref[1]).astype(BF16))
        return carry

    lax.fori_loop(0, nblk, up, 0, unroll=4)

    def down(ii, carry):
        sl = pl.ds(pl.multiple_of((nblk - 1 - ii) * c, c), c)
        q, k, v = qr_ref[sl, :], kr_ref[sl, :], v_ref[sl, :]
        state = sb_ref[...]
        y = y_ref[sl, :] + _dot(q, state.astype(BF16)) * xz_ref[2]
        sb_ref[...] = state * gc_ref[1] + _dot_tn(k, (v.astype(F32) * xz_ref[3]).astype(BF16))
        mu = jnp.mean(y, axis=-1, keepdims=True)
        yc = y - mu
        var = jnp.mean(yc * yc, axis=-1, keepdims=True)
        gate = gate_ref[sl, :].astype(F32)
        o_ref[sl, :] = (gate * _sigmoid(gate) * (yc * lax.rsqrt(var + EPS))).astype(o_ref.dtype)
        return carry

    lax.fori_loop(0, nblk, down, 0, unroll=4)


def _retention(u, decay, cos2, sin2):
    qb, kb = OFF_QB // RET_DK, OFF_KB // RET_DK
    vb, gb = OFF_VB // RET_DV, OFF_GB // RET_DV
    qk_blk, v_blk = (SEQ, RET_DK), (SEQ, RET_DV)
    scratch = [(qk_blk, BF16), (qk_blk, BF16), (v_blk, F32), ((RET_BLK, RET_BLK), F32),
               ((4, RET_BLK, RET_DV), F32), ((2, RET_DK, RET_DV), F32),
               ((RET_DK, RET_DV), F32), ((RET_DK, RET_DV), F32)]
    return pl.pallas_call(
        _ret_body,
        grid=(RET_HEADS, BATCH),
        in_specs=[pl.BlockSpec(memory_space=pltpu.SMEM),
                  pl.BlockSpec(qk_blk, lambda h, b: (b, qb + h)),
                  pl.BlockSpec(qk_blk, lambda h, b: (b, kb + h)),
                  pl.BlockSpec(v_blk, lambda h, b: (b, vb + h)),
                  pl.BlockSpec(v_blk, lambda h, b: (b, gb + h)),
                  pl.BlockSpec(qk_blk, lambda h, b: (0, 0)),
                  pl.BlockSpec(qk_blk, lambda h, b: (0, 0))],
        out_specs=pl.BlockSpec(v_blk, lambda h, b: (b, h)),
        out_shape=jax.ShapeDtypeStruct((M_TOK, RET_V_W), BF16),
        scratch_shapes=[pltpu.VMEM(s, d) for s, d in scratch],
        compiler_params=_params(
            ("parallel", "parallel"),
            [(qk_blk, BF16)] * 2 + [(v_blk, BF16)] * 3 + [(qk_blk, F32)] * 2,
            scratch + [((RET_BLK, RET_DV), F32)] * 4),
        name="retention",
    )(decay, u, u, u, u, cos2, sin2)


def _postmix_body(oa_ref, ob_ref, oc_ref, wb_ref, sa_ref, sb_ref, sc_ref, wo_ref, x_ref,
                  g_ref, wq_ref, kv_ref, wco_ref, o_ref):
    def branch(o, row0, s):
        return _sigmoid(s[...].astype(F32)) * _dot(o[...], wb_ref[row0:row0 + o.shape[1], :])

    merged = branch(oa_ref, 0, sa_ref) + branch(ob_ref, NA_W, sb_ref) + branch(oc_ref, NA_W + RET_V_W, sc_ref)
    x1 = x_ref[...] + _dot(merged.astype(BF16), wo_ref[...])
    q = _dot(_rms(x1, g_ref[...]).astype(BF16), wq_ref[...]).astype(BF16)
    heads = []
    for h in range(CROSS_HEADS):
        lo = h * HEAD_DIM
        k = kv_ref[:, lo:lo + HEAD_DIM]
        v = kv_ref[:, CROSS_W + lo:CROSS_W + lo + HEAD_DIM]
        s = _dot_nt(q[:, lo:lo + HEAD_DIM], k) * QK_SCALE2
        m = jnp.max(s, axis=-1, keepdims=True)
        p = jnp.exp2(s - m)
        l = jnp.sum(p, axis=-1, keepdims=True)
        heads.append((_dot(p.astype(BF16), v) * (1.0 / l)).astype(BF16))
    o_ref[...] = x1 + _dot(jnp.concatenate(heads, axis=-1), wco_ref[...])


def _postmix(o_a, o_b, o_c, w_branch, u, w_out, x, g, w_cq, kv, w_co, *, tm):
    mix_w = NA_W + RET_V_W + DIL_W
    per_batch = SEQ // tm
    once = pl.Buffered(1)

    def rows(k):
        return pl.BlockSpec((tm, k), lambda i: (i, 0))

    def gate(off):
        return pl.BlockSpec((pl.Element(tm), pl.Element(D_MODEL)), lambda i: (i * tm, off))

    def weight(r, c):
        return pl.BlockSpec((r, c), lambda i: (0, 0), pipeline_mode=once)

    return pl.pallas_call(
        _postmix_body,
        grid=(M_TOK // tm,),
        in_specs=[rows(NA_W), rows(RET_V_W), rows(DIL_W), weight(mix_w, D_MODEL),
                  gate(OFF_SA), gate(OFF_SB), gate(OFF_SC), weight(D_MODEL, D_MODEL), rows(D_MODEL),
                  pl.BlockSpec((1, D_MODEL), lambda i: (0, 0), pipeline_mode=once),
                  weight(D_MODEL, CROSS_W),
                  pl.BlockSpec((MEM_LEN, 2 * CROSS_W), lambda i: (i // per_batch, 0)),
                  weight(CROSS_W, D_MODEL)],
        out_specs=rows(D_MODEL),
        out_shape=jax.ShapeDtypeStruct((M_TOK, D_MODEL), F32),
        compiler_params=_params(
            ("parallel",),
            [((tm, mix_w), BF16)] + [((tm, D_MODEL), BF16)] * 3 + [((tm, D_MODEL), F32)] * 2
            + [((MEM_LEN, 2 * CROSS_W), BF16)],
            [((mix_w + D_MODEL + 2 * CROSS_W, D_MODEL), BF16)] + [((tm, D_MODEL), F32)] * 6),
        name="post_mix",
    )(o_a, o_b, o_c, w_branch, u, u, u, w_out, x, g.reshape(1, D_MODEL), w_cq, kv, w_co)


def _mlp_body(*refs, final):
    if final:
        x_ref, g_ref, w1_ref, w2_ref, gf_ref, o_ref, xn_ref = refs
    else:
        x_ref, g_ref, w1_ref, w2_ref, o_ref, xn_ref = refs

    @pl.when(pl.program_id(1) == 0)
    def _():
        x = x_ref[...]
        xn_ref[...] = _rms(x, g_ref[...]).astype(BF16)
        o_ref[...] = x

    hid = jnp.square(jnp.maximum(_dot(xn_ref[...], w1_ref[...]), 0.0)).astype(BF16)
    o_ref[...] += _dot(hid, w2_ref[...])

    if final:
        @pl.when(pl.program_id(1) == pl.num_programs(1) - 1)
        def _():
            o_ref[...] = _rms(o_ref[...], gf_ref[...])


def _mlp(x, g, w1, w2, *, tm, tf, cast=None, final_g=None):
    m, k = x.shape
    vec = pl.BlockSpec((1, k), lambda i, f: (0, 0))
    final = final_g is not None
    return _call_with_cast(
        functools.partial(_mlp_body, final=final),
        grid=(m // tm, D_FF // tf),
        in_specs=[pl.BlockSpec((tm, k), lambda i, f: (i, 0)), vec,
                  pl.BlockSpec((k, tf), lambda i, f: (0, f)),
                  pl.BlockSpec((tf, k), lambda i, f: (f, 0))] + ([vec] if final else []),
        out_spec=pl.BlockSpec((tm, k), lambda i, f: (i, 0)),
        out_shape=jax.ShapeDtypeStruct((m, k), F32),
        scratch=[((tm, k), BF16)],
        pipelined=[((tm, k), F32), ((1, k), F32), ((k, tf), BF16), ((tf, k), BF16), ((tm, k), F32)],
        temporaries=[((tm, tf), F32), ((tm, tf), BF16), ((tm, k), F32)],
        operands=[x, g.reshape(1, k), w1, w2] + ([final_g.reshape(1, k)] if final else []), name="mlp", cast=cast)


def kernel(x, mem, t5_bias, norm_mix_g, w_in, na_rpb, ret_decay, w_branch, w_out, norm_cross_g, norm_mem_g,
           w_cq, w_ckv, w_co, norm_mlp_g, w_mlp1, w_mlp2, final_norm_g):
    xs = x.reshape(M_TOK, D_MODEL)
    mem2 = mem.reshape(BATCH * MEM_LEN, D_MODEL)
    stacked = [w_in, w_mlp1, w_mlp2, w_branch, w_out, w_cq, w_ckv, w_co]
    w_in_h = w_in[0].astype(BF16)
    cos2, sin2 = _rotary_tables()
    dil_rows = _dil_offset_rows(t5_bias)

    for layer in range(DEPTH):
        u, rest = _normmm(xs, norm_mix_g[layer], w_in_h, tm=1024, tn=1536, name="in_proj",
                          cast=(layer, stacked[1:]))
        w_mlp1_h, w_mlp2_h, w_branch_h, w_out_h, w_cq_h, w_ckv_h, w_co_h = rest
        o_a = _na_attention(u, na_rpb[layer])
        o_b = _retention(u, ret_decay[layer], cos2, sin2)
        o_c = _dil_attention(u, dil_rows)
        kv, _ = _normmm(mem2, norm_mem_g[layer], w_ckv_h, tm=BATCH * MEM_LEN, tn=512, name="mem_kv_proj")
        xs = _postmix(o_a, o_b, o_c, w_branch_h, u, w_out_h, xs, norm_cross_g[layer], w_cq_h, kv, w_co_h, tm=256)
        last = layer + 1 == DEPTH
        xs, half = _mlp(xs, norm_mlp_g[layer], w_mlp1_h, w_mlp2_h, tm=512, tf=1024,
                        cast=None if last else (layer + 1, stacked[:1]), final_g=final_norm_g if last else None)
        if not last:
            w_in_h, = half
    return xs.reshape(BATCH, SEQ, D_MODEL)
```

```python
import functools

import numpy as np
import jax
import jax.numpy as jnp
from jax import lax
from jax.experimental import pallas as pl
from jax.experimental.pallas import tpu as pltpu

D_MODEL = 2048
BATCH = 2
SEQ = 4096
DEPTH = 4
MEM_LEN = 256
HEAD_DIM = 128
GRID_W = 64
NA_HEADS = 6
NA_ROWS = 8
NA_COLS = 16
RET_HEADS = 4
RET_DK = 128
RET_DV = 256
RET_BLK = 256
ROPE_BASE = 10000.0
DIL_HEADS = 6
DIL_PAIRS = ((128, 1), (512, 4), (2048, 16))
T5_BUCKETS = 32
T5_MAX_DIST = 1024
CROSS_HEADS = 4
D_FF = 4 * D_MODEL
EPS = 1e-6
NEG = -1e30

NA_W = NA_HEADS * HEAD_DIM
RET_QK_W = RET_HEADS * RET_DK
RET_V_W = RET_HEADS * RET_DV
DIL_W = DIL_HEADS * HEAD_DIM
CROSS_W = CROSS_HEADS * HEAD_DIM
IN_SPLITS = (NA_W, NA_W, NA_W, RET_QK_W, RET_QK_W, RET_V_W, RET_V_W,
             DIL_W, DIL_W, DIL_W, D_MODEL, D_MODEL, D_MODEL)
IN_W = sum(IN_SPLITS)
(OFF_QA, OFF_KA, OFF_VA, OFF_QB, OFF_KB, OFF_VB, OFF_GB,
 OFF_QC, OFF_KC, OFF_VC, OFF_SA, OFF_SB, OFF_SC) = [int(o) for o in np.cumsum((0,) + IN_SPLITS[:-1])]

M_TOK = BATCH * SEQ
ATT_SCALE = HEAD_DIM ** -0.5
LOG2E = float(np.log2(np.e))
QK_SCALE2 = ATT_SCALE * LOG2E

V7X_VMEM_BYTES = 64 * 1024 * 1024
LANES = 128
SUBLANES = 8

NA_QROWS = 4
NA_KROWS = NA_QROWS + NA_ROWS
NA_QBLK = NA_QROWS * GRID_W
NA_KBLK = NA_KROWS * GRID_W
GRID_H = SEQ // GRID_W
NA_NBLK = GRID_H // NA_QROWS

DIL_BLK = 256
DIL_REACH = (max(w for w, _ in DIL_PAIRS) // 2 + DIL_BLK - 1) // DIL_BLK
DIL_NDELTA = 2 * DIL_REACH + 1
DIL_NBLK = SEQ // DIL_BLK

F32 = jnp.float32
BF16 = jnp.bfloat16


def _nbytes(shape, dtype):
    return int(np.prod(shape)) * jnp.dtype(dtype).itemsize


def _params(semantics, pipelined, resident):
    need = 2 * sum(_nbytes(s, d) for s, d in pipelined) + sum(_nbytes(s, d) for s, d in resident)
    assert need < V7X_VMEM_BYTES, need
    return pltpu.CompilerParams(dimension_semantics=semantics,
                                vmem_limit_bytes=min(V7X_VMEM_BYTES, need + need // 4))


def _rms(x, g):
    ms = jnp.mean(x * x, axis=-1, keepdims=True)
    return x * lax.rsqrt(ms + EPS) * g


def _sigmoid(x):
    return 1.0 / (1.0 + jnp.exp(-x))


def _dot(a, b):
    return jnp.dot(a, b, preferred_element_type=F32)


def _dot_nt(a, b):
    return lax.dot_general(a, b, (((1,), (1,)), ((), ())), preferred_element_type=F32)


def _dot_tn(a, b):
    return lax.dot_general(a, b, (((0,), (0,)), ((), ())), preferred_element_type=F32)


CAST_CHUNK_BYTES = 1 << 19


def _cast_rows(rows, cols, nsteps):
    r = 2 * SUBLANES
    while rows % r or rows // r > nsteps or (r * cols * 4 < CAST_CHUNK_BYTES and r < rows):
        r += 2 * SUBLANES
    return r


def _cast_chunk(step, nsteps, layer, srcs, dsts, stages, halves, sem_in, sem_out):
    slot = step % 2
    every = range(len(srcs))
    rows = [st.shape[1] for st in stages]
    nchunk = [dsts[k].shape[0] // rows[k] for k in every]

    def in_copy(k, s, sl):
        return pltpu.make_async_copy(srcs[k].at[layer, pl.ds(pl.multiple_of(s * rows[k], rows[k]), rows[k]), :],
                                     stages[k].at[sl], sem_in.at[k, sl])

    def out_copy(k, s, sl):
        return pltpu.make_async_copy(halves[k].at[sl], dsts[k].at[pl.ds(pl.multiple_of(s * rows[k], rows[k]), rows[k]), :],
                                     sem_out.at[k, sl])

    def cast(k):
        in_copy(k, step, slot).wait()
        halves[k][slot] = stages[k][slot].astype(BF16)

    @pl.when(step == 0)
    def _():
        for k in every:
            in_copy(k, 0, 0).start()

    full = [k for k in every if nchunk[k] == nsteps]
    for nc in sorted(set(nchunk)):
        group = [k for k in every if nchunk[k] == nc]

        @pl.when(step + 1 < nc)
        def _():
            for k in group:
                in_copy(k, step + 1, 1 - slot).start()

        @pl.when((step >= 2) & (step < nc + 2))
        def _():
            for k in group:
                out_copy(k, step - 2, slot).wait()

        if nc < nsteps:
            @pl.when(step < nc)
            def _():
                for k in group:
                    cast(k)
                    out_copy(k, step, slot).start()

    for k in full:
        cast(k)

    def epilogue():
        for k in full:
            out_copy(k, step, slot).start()

        late = [(k, c) for k in every for c in range(max(nchunk[k] - 2, 0), nchunk[k]) if c + 2 > nsteps - 1]
        if late:
            @pl.when(step == nsteps - 1)
            def _():
                for k, c in late:
                    out_copy(k, c, c % 2).wait()

    return epilogue


def _call_with_cast(body, *, grid, in_specs, out_spec, out_shape, scratch, pipelined, temporaries, operands, name,
                    cast=None):
    layer, srcs = cast if cast is not None else (None, [])
    n, n_in, n_scr = len(srcs), len(in_specs), len(scratch)
    nsteps = grid[0] * grid[1]
    chunks = [(_cast_rows(w.shape[1], w.shape[2], nsteps), w.shape[2]) for w in srcs]
    cast_scratch = [((2,) + c, F32) for c in chunks] + [((2,) + c, BF16) for c in chunks]

    def wrapped(*refs):
        ins, srcs_r, out = refs[:n_in], refs[n_in:n_in + n], refs[n_in + n]
        dsts = refs[n_in + n + 1:n_in + 2 * n + 1]
        rest = refs[n_in + 2 * n + 1:]
        after = None
        if n:
            stages, halves, (sem_in, sem_out) = rest[n_scr:n_scr + n], rest[n_scr + n:n_scr + 2 * n], rest[-2:]
            step = pl.program_id(0) * grid[1] + pl.program_id(1)
            after = _cast_chunk(step, nsteps, layer, srcs_r, dsts, stages, halves, sem_in, sem_out)
        body(*ins, out, *rest[:n_scr])
        if after is not None:
            after()

    hbm = pl.BlockSpec(memory_space=pltpu.HBM)
    out = pl.pallas_call(
        wrapped,
        grid=grid,
        in_specs=list(in_specs) + [hbm] * n,
        out_specs=[out_spec] + [hbm] * n,
        out_shape=[out_shape] + [jax.ShapeDtypeStruct(w.shape[1:], BF16) for w in srcs],
        scratch_shapes=[pltpu.VMEM(sh, d) for sh, d in scratch + cast_scratch]
        + ([pltpu.SemaphoreType.DMA((n, 2))] * 2 if n else []),
        compiler_params=_params(("arbitrary", "arbitrary") if n else ("parallel", "arbitrary"),
                                pipelined, scratch + cast_scratch + temporaries),
        name=name,
    )(*operands, *srcs)
    return out[0], out[1:]


def _normmm_body(x_ref, g_ref, w_ref, o_ref, xn_ref, *, layer):
    @pl.when(pl.program_id(1) == 0)
    def _():
        xn_ref[...] = _rms(x_ref[...], g_ref[layer:layer + 1, :]).astype(BF16)

    o_ref[...] = _dot(xn_ref[...], w_ref[...]).astype(o_ref.dtype)


def _normmm(x, g, layer, w, *, tm, tn, name, cast=None):
    m, k = x.shape
    n = w.shape[-1]
    return _call_with_cast(
        functools.partial(_normmm_body, layer=layer),
        grid=(m // tm, n // tn),
        in_specs=[pl.BlockSpec((tm, k), lambda i, j: (i, 0)),
                  pl.BlockSpec(g.shape, lambda i, j: (0, 0)),
                  pl.BlockSpec((k, tn), lambda i, j: (0, j))],
        out_spec=pl.BlockSpec((tm, tn), lambda i, j: (i, j)),
        out_shape=jax.ShapeDtypeStruct((m, n), BF16),
        scratch=[((tm, k), BF16)],
        pipelined=[((tm, k), F32), (g.shape, F32), ((k, tn), BF16), ((tm, tn), BF16)],
        temporaries=[((tm, tn), F32)],
        operands=[x, g, w], name=name, cast=cast)


NA_RPB_ROWS = 2 * NA_ROWS - 1
NA_RPB_COLS = 2 * NA_COLS - 1
NA_TILE_ROWS = ((0, 0), (NA_QROWS, 0), (GRID_H - NA_QROWS, GRID_H - NA_KROWS))


def _na_tiles_body(rpb_ref, o_ref):
    shape = (GRID_W, LANES)
    lane = lax.broadcasted_iota(jnp.int32, shape, 1)
    cq = lax.broadcasted_iota(jnp.int32, shape, 0)
    ck = lane & (GRID_W - 1)
    c0 = jnp.clip(cq - NA_COLS // 2, 0, GRID_W - NA_COLS)
    col_ok = (ck >= c0) & (ck < c0 + NA_COLS)
    left = lane < GRID_W
    neg = jnp.full(shape, NEG, F32)
    lo, hi = [], []
    for a in range(NA_RPB_ROWS):
        row = jnp.broadcast_to(rpb_ref[a:a + 1, :], shape)
        lo.append(pltpu.roll(row, LANES - (NA_COLS - 1), 1, stride=1, stride_axis=0))
        hi.append(pltpu.roll(row, GRID_W - (NA_COLS - 1), 1, stride=1, stride_axis=0))
    for t, (qrow0, krow0) in enumerate(NA_TILE_ROWS):
        for rq in range(NA_QROWS):
            r = qrow0 + rq
            r0 = min(max(r - NA_ROWS // 2, 0), GRID_H - NA_ROWS)
            for pair in range(NA_KROWS // 2):
                halves = []
                for side, table in ((0, lo), (1, hi)):
                    rk = krow0 + 2 * pair + side
                    halves.append(table[rk - r + NA_ROWS - 1] if r0 <= rk < r0 + NA_ROWS else neg)
                blk = jnp.where(col_ok, jnp.where(left, halves[0], halves[1]) * LOG2E, NEG)
                o_ref[t, rq * GRID_W:(rq + 1) * GRID_W, pair * LANES:(pair + 1) * LANES] = blk


def _na_body(q_ref, k_ref, v_ref, rpb_ref, o_ref, s_ref, bias_ref):
    @pl.when(pl.program_id(1) == 0)
    def _():
        _na_tiles_body(rpb_ref, bias_ref)

    def starts(i):
        qrow = i * NA_QROWS
        krow = jnp.clip(qrow - NA_ROWS // 2, 0, GRID_H - NA_KROWS)
        return pl.multiple_of(qrow * GRID_W, NA_QBLK), pl.multiple_of(krow * GRID_W, NA_QBLK)

    def logits(i, buf):
        qs, ks = starts(i)
        tile = jnp.where(i == 0, 0, jnp.where(i == NA_NBLK - 1, 2, 1))
        s_ref[buf] = (_dot_nt(q_ref[pl.ds(qs, NA_QBLK), :], k_ref[pl.ds(ks, NA_KBLK), :]) * QK_SCALE2
                      + bias_ref[tile])

    def finish(i, buf):
        qs, ks = starts(i)
        s = s_ref[buf]
        m = jnp.max(s, axis=-1, keepdims=True)
        p = jnp.exp2(s - m)
        l = jnp.sum(p, axis=-1, keepdims=True)
        o = _dot(p.astype(BF16), v_ref[pl.ds(ks, NA_KBLK), :]) * (1.0 / l)
        o_ref[pl.ds(qs, NA_QBLK), :] = o.astype(o_ref.dtype)

    logits(0, 0)

    def pair(ii, carry):
        i = 2 * ii
        logits(i + 1, 1)
        finish(i, 0)
        logits(jnp.minimum(i + 2, NA_NBLK - 1), 0)
        finish(i + 1, 1)
        return carry

    lax.fori_loop(0, NA_NBLK // 2, pair, 0)


def _na_attention(u, padded_rpb, layer):
    qb, kb, vb = OFF_QA // HEAD_DIM, OFF_KA // HEAD_DIM, OFF_VA // HEAD_DIM
    blk = (SEQ, HEAD_DIM)
    return pl.pallas_call(
        _na_body,
        grid=(NA_HEADS, BATCH),
        in_specs=[pl.BlockSpec(blk, lambda h, b: (b, qb + h)),
                  pl.BlockSpec(blk, lambda h, b: (b, kb + h)),
                  pl.BlockSpec(blk, lambda h, b: (b, vb + h)),
                  pl.BlockSpec((None, None, 2 * SUBLANES, LANES), lambda h, b: (layer, h, 0, 0))],
        out_specs=pl.BlockSpec(blk, lambda h, b: (b, h)),
        out_shape=jax.ShapeDtypeStruct((M_TOK, NA_W), BF16),
        scratch_shapes=[pltpu.VMEM((2, NA_QBLK, NA_KBLK), F32), pltpu.VMEM((3, NA_QBLK, NA_KBLK), F32)],
        compiler_params=_params(
            ("arbitrary", "arbitrary"),
            [(blk, BF16)] * 4,
            [((NA_QBLK, NA_KBLK), F32)] * 8),
        name="na_attention",
    )(u, u, u, padded_rpb)


def _t5_bucket(rel):
    nb = T5_BUCKETS // 2
    ret = (rel > 0).astype(np.int32) * nb
    n = np.abs(rel)
    max_exact = nb // 2
    large = max_exact + (np.log(np.maximum(n, 1) / max_exact) / np.log(T5_MAX_DIST / max_exact)
                         * (nb - max_exact)).astype(np.int32)
    large = np.minimum(large, nb - 1)
    return (ret + np.where(n < max_exact, n, large)).astype(np.int32)


def _dil_tiles_body(f_ref, o_ref):
    for d in range(DIL_NDELTA):
        row = jnp.broadcast_to(f_ref[d:d + 1, :], (DIL_BLK, 2 * DIL_BLK))
        o_ref[d] = pltpu.roll(row, DIL_BLK, 1, stride=1, stride_axis=0)[:, :DIL_BLK]
    o_ref[DIL_NDELTA] = jnp.full((DIL_BLK, DIL_BLK), NEG, F32)


def _dil_offset_rows(t5_bias):
    delta = np.arange(-DIL_REACH, DIL_REACH + 1)[:, None]
    off = delta * DIL_BLK + np.arange(-DIL_BLK, DIL_BLK)[None, :]
    count = np.zeros(off.shape, np.int32)
    for w, d in DIL_PAIRS:
        count += ((off % d == 0) & (np.abs(off) <= w // 2)).astype(np.int32)
    bucket = _t5_bucket(np.clip(off, -T5_MAX_DIST, T5_MAX_DIST))
    bias = jnp.take(t5_bias.T.astype(F32), jnp.asarray(bucket.reshape(-1)), axis=1).reshape((DIL_HEADS,) + off.shape)
    logc = jnp.log(jnp.asarray(np.maximum(count, 1), F32))
    f = jnp.where(jnp.asarray(count > 0)[None], (bias + logc[None]) * LOG2E, NEG)
    return jnp.pad(f, ((0, 0), (0, 2 * SUBLANES - DIL_NDELTA), (0, 0)))


def _dil_body(q_ref, k_ref, v_ref, f_ref, o_ref, s_ref, vx_ref, t_ref):
    @pl.when(pl.program_id(1) == 0)
    def _():
        _dil_tiles_body(f_ref, t_ref)

    vx_ref[:, :HEAD_DIM] = v_ref[...]
    lane = lax.broadcasted_iota(jnp.int32, (SEQ, HEAD_DIM), 1)
    vx_ref[:, HEAD_DIM:] = jnp.where(lane == 0, 1.0, 0.0).astype(BF16)

    def key_start(i, d):
        return pl.multiple_of(jnp.clip(i + (d - DIL_REACH), 0, DIL_NBLK - 1) * DIL_BLK, DIL_BLK)

    def logits(i, buf):
        q = q_ref[pl.ds(pl.multiple_of(i * DIL_BLK, DIL_BLK), DIL_BLK), :]
        for d in range(DIL_NDELTA):
            j = i + (d - DIL_REACH)
            tile = jnp.where((j >= 0) & (j < DIL_NBLK), d, DIL_NDELTA)
            s_ref[buf, :, d * DIL_BLK:(d + 1) * DIL_BLK] = (
                _dot_nt(q, k_ref[pl.ds(key_start(i, d), DIL_BLK), :]) * QK_SCALE2 + t_ref[tile])

    def finish(i, buf):
        half = DIL_BLK // 2
        for r0 in (0, half):
            m = jnp.max(s_ref[buf, r0:r0 + half, :], axis=-1, keepdims=True)
            acc = jnp.zeros((half, 2 * HEAD_DIM), F32)
            for d in range(DIL_NDELTA):
                p = jnp.exp2(s_ref[buf, r0:r0 + half, d * DIL_BLK:(d + 1) * DIL_BLK] - m)
                acc = acc + _dot(p.astype(BF16), vx_ref[pl.ds(key_start(i, d), DIL_BLK), :])
            out = acc[:, :HEAD_DIM] * (1.0 / acc[:, HEAD_DIM:HEAD_DIM + 1])
            o_ref[pl.ds(pl.multiple_of(i * DIL_BLK + r0, half), half), :] = out.astype(o_ref.dtype)

    logits(0, 0)

    def pair(ii, carry):
        i = 2 * ii
        logits(i + 1, 1)
        finish(i, 0)
        logits(jnp.minimum(i + 2, DIL_NBLK - 1), 0)
        finish(i + 1, 1)
        return carry

    lax.fori_loop(0, DIL_NBLK // 2, pair, 0)


def _dil_attention(u, offset_rows):
    qb, kb, vb = OFF_QC // HEAD_DIM, OFF_KC // HEAD_DIM, OFF_VC // HEAD_DIM
    blk = (SEQ, HEAD_DIM)
    return pl.pallas_call(
        _dil_body,
        grid=(DIL_HEADS, BATCH),
        in_specs=[pl.BlockSpec(blk, lambda h, b: (b, qb + h)),
                  pl.BlockSpec(blk, lambda h, b: (b, kb + h)),
                  pl.BlockSpec(blk, lambda h, b: (b, vb + h)),
                  pl.BlockSpec((None, 2 * SUBLANES, 2 * DIL_BLK), lambda h, b: (h, 0, 0))],
        out_specs=pl.BlockSpec(blk, lambda h, b: (b, h)),
        out_shape=jax.ShapeDtypeStruct((M_TOK, DIL_W), BF16),
        scratch_shapes=[pltpu.VMEM((2, DIL_BLK, DIL_NDELTA * DIL_BLK), F32), pltpu.VMEM((SEQ, 2 * HEAD_DIM), BF16),
                        pltpu.VMEM((DIL_NDELTA + 1, DIL_BLK, DIL_BLK), F32)],
        compiler_params=_params(
            ("arbitrary", "arbitrary"),
            [(blk, BF16)] * 4,
            [((DIL_BLK, DIL_NDELTA * DIL_BLK), F32)] * 3 + [((SEQ, 2 * HEAD_DIM), BF16),
                                                            ((DIL_NDELTA + 1, DIL_BLK, DIL_BLK), F32),
                                                            ((DIL_BLK, 2 * DIL_BLK), F32)]),
        name="dil_attention",
    )(u, u, u, offset_rows)


def _rotary_tables():
    inv_freq = jnp.asarray((ROPE_BASE ** (-np.arange(0, RET_DK, 2, dtype=np.float32) / RET_DK)).astype(np.float32))
    ang = jnp.arange(SEQ, dtype=F32)[:, None] * inv_freq[None, :]
    cos, sin = jnp.cos(ang), jnp.sin(ang)
    return jnp.concatenate([cos, cos], axis=-1), jnp.concatenate([-sin, sin], axis=-1)


def _ret_tables(dec_ref, layer, h, d_ref, xz_ref, gc_ref):
    c = RET_BLK
    row = lax.broadcasted_iota(jnp.int32, (c, c), 0)
    col = lax.broadcasted_iota(jnp.int32, (c, c), 1)
    diff = (row - col).astype(F32)
    log_f = -jnp.exp(jnp.full((c, c), dec_ref[layer, 0, h], F32))
    log_b = -jnp.exp(jnp.full((c, c), dec_ref[layer, 1, h], F32))
    d_ref[...] = jnp.where(diff >= 0.0, jnp.exp(jnp.maximum(diff, 0.0) * log_f),
                           jnp.exp(jnp.maximum(-diff, 0.0) * log_b))
    j = lax.broadcasted_iota(jnp.int32, (c, RET_DV), 0).astype(F32)
    log_f2 = -jnp.exp(jnp.full((c, RET_DV), dec_ref[layer, 0, h], F32))
    log_b2 = -jnp.exp(jnp.full((c, RET_DV), dec_ref[layer, 1, h], F32))
    xz_ref[0] = jnp.exp((j + 1.0) * log_f2)
    xz_ref[1] = jnp.exp((c - 1.0 - j) * log_f2)
    xz_ref[2] = jnp.exp((c - j) * log_b2)
    xz_ref[3] = jnp.exp(j * log_b2)
    gc_ref[0] = jnp.exp(c * -jnp.exp(jnp.full((RET_DK, RET_DV), dec_ref[layer, 0, h], F32)))
    gc_ref[1] = jnp.exp(c * -jnp.exp(jnp.full((RET_DK, RET_DV), dec_ref[layer, 1, h], F32)))


def _ret_body(dec_ref, q_ref, k_ref, v_ref, gate_ref, cos_ref, sin_ref, o_ref,
              qr_ref, kr_ref, y_ref, d_ref, xz_ref, gc_ref, sf_ref, sb_ref, *, layer):
    c = RET_BLK
    nblk = SEQ // c
    _ret_tables(dec_ref, layer, pl.program_id(0), d_ref, xz_ref, gc_ref)
    sf_ref[...] = jnp.zeros_like(sf_ref)
    sb_ref[...] = jnp.zeros_like(sb_ref)

    def rotate(x, cs, sn):
        return x * cs + pltpu.roll(x, RET_DK // 2, 1) * sn

    def up(i, carry):
        sl = pl.ds(pl.multiple_of(i * c, c), c)
        cs, sn = cos_ref[sl, :], sin_ref[sl, :]
        q = rotate(q_ref[sl, :].astype(F32), cs, sn).astype(BF16)
        k = (rotate(k_ref[sl, :].astype(F32), cs, sn) * (RET_DK ** -0.5)).astype(BF16)
        qr_ref[sl, :] = q
        kr_ref[sl, :] = k
        v = v_ref[sl, :]
        inner = (_dot_nt(q, k) * d_ref[...]).astype(BF16)
        state = sf_ref[...]
        y_ref[sl, :] = _dot(inner, v) + _dot(q, state.astype(BF16)) * xz_ref[0]
        sf_ref[...] = state * gc_ref[0] + _dot_tn(k, (v.astype(F32) * xz_ref[1]).astype(BF16))
        return carry

    lax.fori_loop(0, nblk, up, 0, unroll=4)

    def down(ii, carry):
        sl = pl.ds(pl.multiple_of((nblk - 1 - ii) * c, c), c)
        q, k, v = qr_ref[sl, :], kr_ref[sl, :], v_ref[sl, :]
        state = sb_ref[...]
        y = y_ref[sl, :] + _dot(q, state.astype(BF16)) * xz_ref[2]
        sb_ref[...] = state * gc_ref[1] + _dot_tn(k, (v.astype(F32) * xz_ref[3]).astype(BF16))
        mu = jnp.mean(y, axis=-1, keepdims=True)
        yc = y - mu
        var = jnp.mean(yc * yc, axis=-1, keepdims=True)
        gate = gate_ref[sl, :].astype(F32)
        o_ref[sl, :] = (gate * _sigmoid(gate) * (yc * lax.rsqrt(var + EPS))).astype(o_ref.dtype)
        return carry

    lax.fori_loop(0, nblk, down, 0, unroll=4)


def _retention(u, decay, layer, cos2, sin2):
    qb, kb = OFF_QB // RET_DK, OFF_KB // RET_DK
    vb, gb = OFF_VB // RET_DV, OFF_GB // RET_DV
    qk_blk, v_blk = (SEQ, RET_DK), (SEQ, RET_DV)
    scratch = [(qk_blk, BF16), (qk_blk, BF16), (v_blk, F32), ((RET_BLK, RET_BLK), F32),
               ((4, RET_BLK, RET_DV), F32), ((2, RET_DK, RET_DV), F32),
               ((RET_DK, RET_DV), F32), ((RET_DK, RET_DV), F32)]
    return pl.pallas_call(
        functools.partial(_ret_body, layer=layer),
        grid=(RET_HEADS, BATCH),
        in_specs=[pl.BlockSpec(memory_space=pltpu.SMEM),
                  pl.BlockSpec(qk_blk, lambda h, b: (b, qb + h)),
                  pl.BlockSpec(qk_blk, lambda h, b: (b, kb + h)),
                  pl.BlockSpec(v_blk, lambda h, b: (b, vb + h)),
                  pl.BlockSpec(v_blk, lambda h, b: (b, gb + h)),
                  pl.BlockSpec(qk_blk, lambda h, b: (0, 0)),
                  pl.BlockSpec(qk_blk, lambda h, b: (0, 0))],
        out_specs=pl.BlockSpec(v_blk, lambda h, b: (b, h)),
        out_shape=jax.ShapeDtypeStruct((M_TOK, RET_V_W), BF16),
        scratch_shapes=[pltpu.VMEM(s, d) for s, d in scratch],
        compiler_params=_params(
            ("parallel", "parallel"),
            [(qk_blk, BF16)] * 2 + [(v_blk, BF16)] * 3 + [(qk_blk, F32)] * 2,
            scratch + [((RET_BLK, RET_DV), F32)] * 4),
        name="retention",
    )(decay, u, u, u, u, cos2, sin2)


def _postmix_body(oa_ref, ob_ref, oc_ref, wb_ref, sa_ref, sb_ref, sc_ref, wo_ref, x_ref,
                  g_ref, wq_ref, kv_ref, wco_ref, o_ref, *, layer):
    def branch(o, row0, s):
        return _sigmoid(s[...].astype(F32)) * _dot(o[...], wb_ref[row0:row0 + o.shape[1], :])

    merged = branch(oa_ref, 0, sa_ref) + branch(ob_ref, NA_W, sb_ref) + branch(oc_ref, NA_W + RET_V_W, sc_ref)
    x1 = x_ref[...] + _dot(merged.astype(BF16), wo_ref[...])
    q = _dot(_rms(x1, g_ref[layer:layer + 1, :]).astype(BF16), wq_ref[...]).astype(BF16)
    heads = []
    for h in range(CROSS_HEADS):
        lo = h * HEAD_DIM
        k = kv_ref[:, lo:lo + HEAD_DIM]
        v = kv_ref[:, CROSS_W + lo:CROSS_W + lo + HEAD_DIM]
        s = _dot_nt(q[:, lo:lo + HEAD_DIM], k) * QK_SCALE2
        m = jnp.max(s, axis=-1, keepdims=True)
        p = jnp.exp2(s - m)
        l = jnp.sum(p, axis=-1, keepdims=True)
        heads.append((_dot(p.astype(BF16), v) * (1.0 / l)).astype(BF16))
    o_ref[...] = x1 + _dot(jnp.concatenate(heads, axis=-1), wco_ref[...])


def _postmix(o_a, o_b, o_c, w_branch, u, w_out, x, g, layer, w_cq, kv, w_co, *, tm):
    mix_w = NA_W + RET_V_W + DIL_W
    per_batch = SEQ // tm
    once = pl.Buffered(1)

    def rows(k):
        return pl.BlockSpec((tm, k), lambda i: (i, 0))

    def gate(off):
        return pl.BlockSpec((pl.Element(tm), pl.Element(D_MODEL)), lambda i: (i * tm, off))

    def weight(r, c):
        return pl.BlockSpec((r, c), lambda i: (0, 0), pipeline_mode=once)

    return pl.pallas_call(
        functools.partial(_postmix_body, layer=layer),
        grid=(M_TOK // tm,),
        in_specs=[rows(NA_W), rows(RET_V_W), rows(DIL_W), weight(mix_w, D_MODEL),
                  gate(OFF_SA), gate(OFF_SB), gate(OFF_SC), weight(D_MODEL, D_MODEL), rows(D_MODEL),
                  pl.BlockSpec(g.shape, lambda i: (0, 0), pipeline_mode=once),
                  weight(D_MODEL, CROSS_W),
                  pl.BlockSpec((MEM_LEN, 2 * CROSS_W), lambda i: (i // per_batch, 0)),
                  weight(CROSS_W, D_MODEL)],
        out_specs=rows(D_MODEL),
        out_shape=jax.ShapeDtypeStruct((M_TOK, D_MODEL), F32),
        compiler_params=_params(
            ("parallel",),
            [((tm, mix_w), BF16)] + [((tm, D_MODEL), BF16)] * 3 + [((tm, D_MODEL), F32)] * 2
            + [((MEM_LEN, 2 * CROSS_W), BF16)],
            [((mix_w + D_MODEL + 2 * CROSS_W, D_MODEL), BF16)] + [((tm, D_MODEL), F32)] * 6),
        name="post_mix",
    )(o_a, o_b, o_c, w_branch, u, u, u, w_out, x, g, w_cq, kv, w_co)


def _mlp_body(*refs, layer, final):
    if final:
        x_ref, g_ref, w1_ref, w2_ref, gf_ref, o_ref, xn_ref = refs
    else:
        x_ref, g_ref, w1_ref, w2_ref, o_ref, xn_ref = refs

    @pl.when(pl.program_id(1) == 0)
    def _():
        x = x_ref[...]
        xn_ref[...] = _rms(x, g_ref[layer:layer + 1, :]).astype(BF16)
        o_ref[...] = x

    hid = jnp.square(jnp.maximum(_dot(xn_ref[...], w1_ref[...]), 0.0)).astype(BF16)
    o_ref[...] += _dot(hid, w2_ref[...])

    if final:
        @pl.when(pl.program_id(1) == pl.num_programs(1) - 1)
        def _():
            o_ref[...] = _rms(o_ref[...], gf_ref[...])


def _mlp(x, g, layer, w1, w2, *, tm, tf, cast=None, final_g=None):
    m, k = x.shape
    vec = pl.BlockSpec((1, k), lambda i, f: (0, 0))
    final = final_g is not None
    return _call_with_cast(
        functools.partial(_mlp_body, layer=layer, final=final),
        grid=(m // tm, D_FF // tf),
        in_specs=[pl.BlockSpec((tm, k), lambda i, f: (i, 0)), pl.BlockSpec(g.shape, lambda i, f: (0, 0)),
                  pl.BlockSpec((k, tf), lambda i, f: (0, f)),
                  pl.BlockSpec((tf, k), lambda i, f: (f, 0))] + ([vec] if final else []),
        out_spec=pl.BlockSpec((tm, k), lambda i, f: (i, 0)),
        out_shape=jax.ShapeDtypeStruct((m, k), F32),
        scratch=[((tm, k), BF16)],
        pipelined=[((tm, k), F32), (g.shape, F32), ((k, tf), BF16), ((tf, k), BF16), ((tm, k), F32)],
        temporaries=[((tm, tf), F32), ((tm, tf), BF16), ((tm, k), F32)],
        operands=[x, g, w1, w2] + ([final_g.reshape(1, k)] if final else []), name="mlp", cast=cast)


def kernel(x, mem, t5_bias, norm_mix_g, w_in, na_rpb, ret_decay, w_branch, w_out, norm_cross_g, norm_mem_g,
           w_cq, w_ckv, w_co, norm_mlp_g, w_mlp1, w_mlp2, final_norm_g):
    xs = x.reshape(M_TOK, D_MODEL)
    mem2 = mem.reshape(BATCH * MEM_LEN, D_MODEL)
    stacked = [w_in, w_mlp1, w_mlp2, w_branch, w_out, w_cq, w_ckv, w_co]
    w_in_h = w_in[0].astype(BF16)
    cos2, sin2 = _rotary_tables()
    dil_rows = _dil_offset_rows(t5_bias)
    rpb = jnp.pad(na_rpb.astype(F32), ((0, 0), (0, 0), (0, 2 * SUBLANES - NA_RPB_ROWS), (0, LANES - NA_RPB_COLS)))

    for layer in range(DEPTH):
        u, rest = _normmm(xs, norm_mix_g, layer, w_in_h, tm=1024, tn=1536, name="in_proj",
                          cast=(layer, stacked[1:]))
        w_mlp1_h, w_mlp2_h, w_branch_h, w_out_h, w_cq_h, w_ckv_h, w_co_h = rest
        o_a = _na_attention(u, rpb, layer)
        o_b = _retention(u, ret_decay, layer, cos2, sin2)
        o_c = _dil_attention(u, dil_rows)
        kv, _ = _normmm(mem2, norm_mem_g, layer, w_ckv_h, tm=BATCH * MEM_LEN, tn=512, name="mem_kv_proj")
        xs = _postmix(o_a, o_b, o_c, w_branch_h, u, w_out_h, xs, norm_cross_g, layer, w_cq_h, kv, w_co_h, tm=256)
        last = layer + 1 == DEPTH
        xs, half = _mlp(xs, norm_mlp_g, layer, w_mlp1_h, w_mlp2_h, tm=512, tf=1024,
                        cast=None if last else (layer + 1, stacked[:1]), final_g=final_norm_g if last else None)
        if not last:
            w_in_h, = half
    return xs.reshape(BATCH, SEQ, D_MODEL)
```

```python
import functools

import numpy as np
import jax
import jax.numpy as jnp
from jax import lax
from jax.experimental import pallas as pl
from jax.experimental.pallas import tpu as pltpu

D_MODEL = 2048
BATCH = 2
SEQ = 4096
DEPTH = 4
MEM_LEN = 256
HEAD_DIM = 128
GRID_W = 64
NA_HEADS = 6
NA_ROWS = 8
NA_COLS = 16
RET_HEADS = 4
RET_DK = 128
RET_DV = 256
RET_BLK = 256
ROPE_BASE = 10000.0
DIL_HEADS = 6
DIL_PAIRS = ((128, 1), (512, 4), (2048, 16))
T5_BUCKETS = 32
T5_MAX_DIST = 1024
CROSS_HEADS = 4
D_FF = 4 * D_MODEL
EPS = 1e-6
NEG = -1e30

NA_W = NA_HEADS * HEAD_DIM
RET_QK_W = RET_HEADS * RET_DK
RET_V_W = RET_HEADS * RET_DV
DIL_W = DIL_HEADS * HEAD_DIM
CROSS_W = CROSS_HEADS * HEAD_DIM
IN_SPLITS = (NA_W, NA_W, NA_W, RET_QK_W, RET_QK_W, RET_V_W, RET_V_W,
             DIL_W, DIL_W, DIL_W, D_MODEL, D_MODEL, D_MODEL)
IN_W = sum(IN_SPLITS)
(OFF_QA, OFF_KA, OFF_VA, OFF_QB, OFF_KB, OFF_VB, OFF_GB,
 OFF_QC, OFF_KC, OFF_VC, OFF_SA, OFF_SB, OFF_SC) = [int(o) for o in np.cumsum((0,) + IN_SPLITS[:-1])]

M_TOK = BATCH * SEQ
ATT_SCALE = HEAD_DIM ** -0.5
LOG2E = float(np.log2(np.e))
QK_SCALE2 = ATT_SCALE * LOG2E

V7X_VMEM_BYTES = 64 * 1024 * 1024
LANES = 128
SUBLANES = 8

NA_QROWS = 4
NA_KROWS = NA_QROWS + NA_ROWS
NA_QBLK = NA_QROWS * GRID_W
NA_KBLK = NA_KROWS * GRID_W
GRID_H = SEQ // GRID_W
NA_NBLK = GRID_H // NA_QROWS

DIL_BLK = 256
DIL_REACH = (max(w for w, _ in DIL_PAIRS) // 2 + DIL_BLK - 1) // DIL_BLK
DIL_NDELTA = 2 * DIL_REACH + 1
DIL_NBLK = SEQ // DIL_BLK

F32 = jnp.float32
BF16 = jnp.bfloat16


def _nbytes(shape, dtype):
    return int(np.prod(shape)) * jnp.dtype(dtype).itemsize


def _params(semantics, pipelined, resident):
    need = 2 * sum(_nbytes(s, d) for s, d in pipelined) + sum(_nbytes(s, d) for s, d in resident)
    assert need < V7X_VMEM_BYTES, need
    return pltpu.CompilerParams(dimension_semantics=semantics,
                                vmem_limit_bytes=min(V7X_VMEM_BYTES, need + need // 4))


def _rms(x, g):
    ms = jnp.mean(x * x, axis=-1, keepdims=True)
    return x * lax.rsqrt(ms + EPS) * g


def _sigmoid(x):
    return 1.0 / (1.0 + jnp.exp(-x))


def _dot(a, b):
    return jnp.dot(a, b, preferred_element_type=F32)


def _dot_nt(a, b):
    return lax.dot_general(a, b, (((1,), (1,)), ((), ())), preferred_element_type=F32)


def _dot_tn(a, b):
    return lax.dot_general(a, b, (((0,), (0,)), ((), ())), preferred_element_type=F32)


CAST_CHUNK_BYTES = 1 << 19


def _cast_rows(rows, cols, nsteps):
    r = 2 * SUBLANES
    while rows % r or rows // r > nsteps or (r * cols * 4 < CAST_CHUNK_BYTES and r < rows):
        r += 2 * SUBLANES
    return r


def _cast_chunk(step, nsteps, layer, srcs, dsts, stages, halves, sem_in, sem_out):
    slot = step % 2
    every = range(len(srcs))
    rows = [st.shape[1] for st in stages]
    nchunk = [dsts[k].shape[0] // rows[k] for k in every]

    def in_copy(k, s, sl):
        return pltpu.make_async_copy(srcs[k].at[layer, pl.ds(pl.multiple_of(s * rows[k], rows[k]), rows[k]), :],
                                     stages[k].at[sl], sem_in.at[k, sl])

    def out_copy(k, s, sl):
        return pltpu.make_async_copy(halves[k].at[sl], dsts[k].at[pl.ds(pl.multiple_of(s * rows[k], rows[k]), rows[k]), :],
                                     sem_out.at[k, sl])

    def cast(k):
        in_copy(k, step, slot).wait()
        halves[k][slot] = stages[k][slot].astype(BF16)

    @pl.when(step == 0)
    def _():
        for k in every:
            in_copy(k, 0, 0).start()

    full = [k for k in every if nchunk[k] == nsteps]
    for nc in sorted(set(nchunk)):
        group = [k for k in every if nchunk[k] == nc]

        @pl.when(step + 1 < nc)
        def _():
            for k in group:
                in_copy(k, step + 1, 1 - slot).start()

        @pl.when((step >= 2) & (step < nc + 2))
        def _():
            for k in group:
                out_copy(k, step - 2, slot).wait()

        if nc < nsteps:
            @pl.when(step < nc)
            def _():
                for k in group:
                    cast(k)
                    out_copy(k, step, slot).start()

    for k in full:
        cast(k)

    def epilogue():
        for k in full:
            out_copy(k, step, slot).start()

        late = [(k, c) for k in every for c in range(max(nchunk[k] - 2, 0), nchunk[k]) if c + 2 > nsteps - 1]
        if late:
            @pl.when(step == nsteps - 1)
            def _():
                for k, c in late:
                    out_copy(k, c, c % 2).wait()

    return epilogue


def _call_with_cast(body, *, grid, in_specs, out_spec, out_shape, scratch, pipelined, temporaries, operands, name,
                    cast=None):
    layer, srcs = cast if cast is not None else (None, [])
    n, n_in, n_scr = len(srcs), len(in_specs), len(scratch)
    nsteps = grid[0] * grid[1]
    chunks = [(_cast_rows(w.shape[1], w.shape[2], nsteps), w.shape[2]) for w in srcs]
    cast_scratch = [((2,) + c, F32) for c in chunks] + [((2,) + c, BF16) for c in chunks]

    def wrapped(*refs):
        ins, srcs_r, out = refs[:n_in], refs[n_in:n_in + n], refs[n_in + n]
        dsts = refs[n_in + n + 1:n_in + 2 * n + 1]
        rest = refs[n_in + 2 * n + 1:]
        after = None
        if n:
            stages, halves, (sem_in, sem_out) = rest[n_scr:n_scr + n], rest[n_scr + n:n_scr + 2 * n], rest[-2:]
            step = pl.program_id(0) * grid[1] + pl.program_id(1)
            after = _cast_chunk(step, nsteps, layer, srcs_r, dsts, stages, halves, sem_in, sem_out)
        body(*ins, out, *rest[:n_scr])
        if after is not None:
            after()

    hbm = pl.BlockSpec(memory_space=pltpu.HBM)
    out = pl.pallas_call(
        wrapped,
        grid=grid,
        in_specs=list(in_specs) + [hbm] * n,
        out_specs=[out_spec] + [hbm] * n,
        out_shape=[out_shape] + [jax.ShapeDtypeStruct(w.shape[1:], BF16) for w in srcs],
        scratch_shapes=[pltpu.VMEM(sh, d) for sh, d in scratch + cast_scratch]
        + ([pltpu.SemaphoreType.DMA((n, 2))] * 2 if n else []),
        compiler_params=_params(("arbitrary", "arbitrary") if n else ("parallel", "arbitrary"),
                                pipelined, scratch + cast_scratch + temporaries),
        name=name,
    )(*operands, *srcs)
    return out[0], out[1:]


def _normmm_body(x_ref, g_ref, w_ref, o_ref, xn_ref):
    @pl.when(pl.program_id(1) == 0)
    def _():
        xn_ref[...] = _rms(x_ref[...], g_ref[...]).astype(BF16)

    o_ref[...] = _dot(xn_ref[...], w_ref[...]).astype(o_ref.dtype)


def _normmm(x, g, w, *, tm, tn, name, cast=None):
    m, k = x.shape
    n = w.shape[-1]
    return _call_with_cast(
        _normmm_body,
        grid=(m // tm, n // tn),
        in_specs=[pl.BlockSpec((tm, k), lambda i, j: (i, 0)),
                  pl.BlockSpec((1, k), lambda i, j: (0, 0)),
                  pl.BlockSpec((k, tn), lambda i, j: (0, j))],
        out_spec=pl.BlockSpec((tm, tn), lambda i, j: (i, j)),
        out_shape=jax.ShapeDtypeStruct((m, n), BF16),
        scratch=[((tm, k), BF16)],
        pipelined=[((tm, k), F32), ((1, k), F32), ((k, tn), BF16), ((tm, tn), BF16)],
        temporaries=[((tm, tn), F32)],
        operands=[x, g.reshape(1, k), w], name=name, cast=cast)


NA_RPB_ROWS = 2 * NA_ROWS - 1
NA_RPB_COLS = 2 * NA_COLS - 1
NA_TILE_ROWS = ((0, 0), (NA_QROWS, 0), (GRID_H - NA_QROWS, GRID_H - NA_KROWS))


def _na_tiles_body(rpb_ref, o_ref):
    shape = (GRID_W, LANES)
    lane = lax.broadcasted_iota(jnp.int32, shape, 1)
    cq = lax.broadcasted_iota(jnp.int32, shape, 0)
    ck = lane & (GRID_W - 1)
    c0 = jnp.clip(cq - NA_COLS // 2, 0, GRID_W - NA_COLS)
    col_ok = (ck >= c0) & (ck < c0 + NA_COLS)
    left = lane < GRID_W
    neg = jnp.full(shape, NEG, F32)
    lo, hi = [], []
    for a in range(NA_RPB_ROWS):
        row = jnp.broadcast_to(rpb_ref[a:a + 1, :], shape)
        lo.append(pltpu.roll(row, LANES - (NA_COLS - 1), 1, stride=1, stride_axis=0))
        hi.append(pltpu.roll(row, GRID_W - (NA_COLS - 1), 1, stride=1, stride_axis=0))
    for t, (qrow0, krow0) in enumerate(NA_TILE_ROWS):
        for rq in range(NA_QROWS):
            r = qrow0 + rq
            r0 = min(max(r - NA_ROWS // 2, 0), GRID_H - NA_ROWS)
            for pair in range(NA_KROWS // 2):
                halves = []
                for side, table in ((0, lo), (1, hi)):
                    rk = krow0 + 2 * pair + side
                    halves.append(table[rk - r + NA_ROWS - 1] if r0 <= rk < r0 + NA_ROWS else neg)
                blk = jnp.where(col_ok, jnp.where(left, halves[0], halves[1]) * LOG2E, NEG)
                o_ref[t, rq * GRID_W:(rq + 1) * GRID_W, pair * LANES:(pair + 1) * LANES] = blk


def _na_body(q_ref, k_ref, v_ref, rpb_ref, o_ref, s_ref, bias_ref):
    @pl.when(pl.program_id(1) == 0)
    def _():
        _na_tiles_body(rpb_ref, bias_ref)

    def starts(i):
        qrow = i * NA_QROWS
        krow = jnp.clip(qrow - NA_ROWS // 2, 0, GRID_H - NA_KROWS)
        return pl.multiple_of(qrow * GRID_W, NA_QBLK), pl.multiple_of(krow * GRID_W, NA_QBLK)

    def logits(i, buf):
        qs, ks = starts(i)
        tile = jnp.where(i == 0, 0, jnp.where(i == NA_NBLK - 1, 2, 1))
        s_ref[buf] = (_dot_nt(q_ref[pl.ds(qs, NA_QBLK), :], k_ref[pl.ds(ks, NA_KBLK), :]) * QK_SCALE2
                      + bias_ref[tile])

    def finish(i, buf):
        qs, ks = starts(i)
        s = s_ref[buf]
        m = jnp.max(s, axis=-1, keepdims=True)
        p = jnp.exp2(s - m)
        l = jnp.sum(p, axis=-1, keepdims=True)
        o = _dot(p.astype(BF16), v_ref[pl.ds(ks, NA_KBLK), :]) * (1.0 / l)
        o_ref[pl.ds(qs, NA_QBLK), :] = o.astype(o_ref.dtype)

    logits(0, 0)

    def pair(ii, carry):
        i = 2 * ii
        logits(i + 1, 1)
        finish(i, 0)
        logits(jnp.minimum(i + 2, NA_NBLK - 1), 0)
        finish(i + 1, 1)
        return carry

    lax.fori_loop(0, NA_NBLK // 2, pair, 0, unroll=2)


def _na_attention(u, rpb):
    padded = jnp.pad(rpb.astype(F32), ((0, 0), (0, 2 * SUBLANES - NA_RPB_ROWS), (0, LANES - NA_RPB_COLS)))
    qb, kb, vb = OFF_QA // HEAD_DIM, OFF_KA // HEAD_DIM, OFF_VA // HEAD_DIM
    blk = (SEQ, HEAD_DIM)
    return pl.pallas_call(
        _na_body,
        grid=(NA_HEADS, BATCH),
        in_specs=[pl.BlockSpec(blk, lambda h, b: (b, qb + h)),
                  pl.BlockSpec(blk, lambda h, b: (b, kb + h)),
                  pl.BlockSpec(blk, lambda h, b: (b, vb + h)),
                  pl.BlockSpec((None, 2 * SUBLANES, LANES), lambda h, b: (h, 0, 0))],
        out_specs=pl.BlockSpec(blk, lambda h, b: (b, h)),
        out_shape=jax.ShapeDtypeStruct((M_TOK, NA_W), BF16),
        scratch_shapes=[pltpu.VMEM((2, NA_QBLK, NA_KBLK), F32), pltpu.VMEM((3, NA_QBLK, NA_KBLK), F32)],
        compiler_params=_params(
            ("arbitrary", "arbitrary"),
            [(blk, BF16)] * 4,
            [((NA_QBLK, NA_KBLK), F32)] * 8),
        name="na_attention",
    )(u, u, u, padded)


def _t5_bucket(rel):
    nb = T5_BUCKETS // 2
    ret = (rel > 0).astype(np.int32) * nb
    n = np.abs(rel)
    max_exact = nb // 2
    large = max_exact + (np.log(np.maximum(n, 1) / max_exact) / np.log(T5_MAX_DIST / max_exact)
                         * (nb - max_exact)).astype(np.int32)
    large = np.minimum(large, nb - 1)
    return (ret + np.where(n < max_exact, n, large)).astype(np.int32)


def _dil_tiles_body(f_ref, o_ref):
    for d in range(DIL_NDELTA):
        row = jnp.broadcast_to(f_ref[d:d + 1, :], (DIL_BLK, 2 * DIL_BLK))
        o_ref[d] = pltpu.roll(row, DIL_BLK, 1, stride=1, stride_axis=0)[:, :DIL_BLK]
    o_ref[DIL_NDELTA] = jnp.full((DIL_BLK, DIL_BLK), NEG, F32)


def _dil_offset_rows(t5_bias):
    delta = np.arange(-DIL_REACH, DIL_REACH + 1)[:, None]
    off = delta * DIL_BLK + np.arange(-DIL_BLK, DIL_BLK)[None, :]
    count = np.zeros(off.shape, np.int32)
    for w, d in DIL_PAIRS:
        count += ((off % d == 0) & (np.abs(off) <= w // 2)).astype(np.int32)
    bucket = _t5_bucket(np.clip(off, -T5_MAX_DIST, T5_MAX_DIST))
    bias = jnp.take(t5_bias.T.astype(F32), jnp.asarray(bucket.reshape(-1)), axis=1).reshape((DIL_HEADS,) + off.shape)
    logc = jnp.log(jnp.asarray(np.maximum(count, 1), F32))
    f = jnp.where(jnp.asarray(count > 0)[None], (bias + logc[None]) * LOG2E, NEG)
    return jnp.pad(f, ((0, 0), (0, 2 * SUBLANES - DIL_NDELTA), (0, 0)))


def _dil_body(q_ref, k_ref, v_ref, f_ref, o_ref, s_ref, vx_ref, t_ref):
    @pl.when(pl.program_id(1) == 0)
    def _():
        _dil_tiles_body(f_ref, t_ref)

    vx_ref[:, :HEAD_DIM] = v_ref[...]
    lane = lax.broadcasted_iota(jnp.int32, (SEQ, HEAD_DIM), 1)
    vx_ref[:, HEAD_DIM:] = jnp.where(lane == 0, 1.0, 0.0).astype(BF16)

    def key_start(i, d):
        return pl.multiple_of(jnp.clip(i + (d - DIL_REACH), 0, DIL_NBLK - 1) * DIL_BLK, DIL_BLK)

    def logits(i, buf):
        q = q_ref[pl.ds(pl.multiple_of(i * DIL_BLK, DIL_BLK), DIL_BLK), :]
        for d in range(DIL_NDELTA):
            j = i + (d - DIL_REACH)
            tile = jnp.where((j >= 0) & (j < DIL_NBLK), d, DIL_NDELTA)
            s_ref[buf, :, d * DIL_BLK:(d + 1) * DIL_BLK] = (
                _dot_nt(q, k_ref[pl.ds(key_start(i, d), DIL_BLK), :]) * QK_SCALE2 + t_ref[tile])

    def finish(i, buf):
        half = DIL_BLK // 2
        for r0 in (0, half):
            m = jnp.max(s_ref[buf, r0:r0 + half, :], axis=-1, keepdims=True)
            acc = jnp.zeros((half, 2 * HEAD_DIM), F32)
            for d in range(DIL_NDELTA):
                p = jnp.exp2(s_ref[buf, r0:r0 + half, d * DIL_BLK:(d + 1) * DIL_BLK] - m)
                acc = acc + _dot(p.astype(BF16), vx_ref[pl.ds(key_start(i, d), DIL_BLK), :])
            out = acc[:, :HEAD_DIM] * (1.0 / acc[:, HEAD_DIM:HEAD_DIM + 1])
            o_ref[pl.ds(pl.multiple_of(i * DIL_BLK + r0, half), half), :] = out.astype(o_ref.dtype)

    logits(0, 0)

    def pair(ii, carry):
        i = 2 * ii
        logits(i + 1, 1)
        finish(i, 0)
        logits(jnp.minimum(i + 2, DIL_NBLK - 1), 0)
        finish(i + 1, 1)
        return carry

    lax.fori_loop(0, DIL_NBLK // 2, pair, 0, unroll=2)


def _dil_attention(u, offset_rows):
    qb, kb, vb = OFF_QC // HEAD_DIM, OFF_KC // HEAD_DIM, OFF_VC // HEAD_DIM
    blk = (SEQ, HEAD_DIM)
    return pl.pallas_call(
        _dil_body,
        grid=(DIL_HEADS, BATCH),
        in_specs=[pl.BlockSpec(blk, lambda h, b: (b, qb + h)),
                  pl.BlockSpec(blk, lambda h, b: (b, kb + h)),
                  pl.BlockSpec(blk, lambda h, b: (b, vb + h)),
                  pl.BlockSpec((None, 2 * SUBLANES, 2 * DIL_BLK), lambda h, b: (h, 0, 0))],
        out_specs=pl.BlockSpec(blk, lambda h, b: (b, h)),
        out_shape=jax.ShapeDtypeStruct((M_TOK, DIL_W), BF16),
        scratch_shapes=[pltpu.VMEM((2, DIL_BLK, DIL_NDELTA * DIL_BLK), F32), pltpu.VMEM((SEQ, 2 * HEAD_DIM), BF16),
                        pltpu.VMEM((DIL_NDELTA + 1, DIL_BLK, DIL_BLK), F32)],
        compiler_params=_params(
            ("arbitrary", "arbitrary"),
            [(blk, BF16)] * 4,
            [((DIL_BLK, DIL_NDELTA * DIL_BLK), F32)] * 3 + [((SEQ, 2 * HEAD_DIM), BF16),
                                                            ((DIL_NDELTA + 1, DIL_BLK, DIL_BLK), F32),
                                                            ((DIL_BLK, 2 * DIL_BLK), F32)]),
        name="dil_attention",
    )(u, u, u, offset_rows)


def _rotary_tables():
    inv_freq = jnp.asarray((ROPE_BASE ** (-np.arange(0, RET_DK, 2, dtype=np.float32) / RET_DK)).astype(np.float32))
    ang = jnp.arange(SEQ, dtype=F32)[:, None] * inv_freq[None, :]
    cos, sin = jnp.cos(ang), jnp.sin(ang)
    return jnp.concatenate([cos, cos], axis=-1), jnp.concatenate([-sin, sin], axis=-1)


def _ret_tables(dec_ref, h, d_ref, xz_ref, gc_ref):
    c = RET_BLK
    row = lax.broadcasted_iota(jnp.int32, (c, c), 0)
    col = lax.broadcasted_iota(jnp.int32, (c, c), 1)
    diff = (row - col).astype(F32)
    log_f = -jnp.exp(jnp.full((c, c), dec_ref[0, h], F32))
    log_b = -jnp.exp(jnp.full((c, c), dec_ref[1, h], F32))
    d_ref[...] = jnp.where(diff >= 0.0, jnp.exp(jnp.maximum(diff, 0.0) * log_f),
                           jnp.exp(jnp.maximum(-diff, 0.0) * log_b))
    j = lax.broadcasted_iota(jnp.int32, (c, RET_DV), 0).astype(F32)
    log_f2 = -jnp.exp(jnp.full((c, RET_DV), dec_ref[0, h], F32))
    log_b2 = -jnp.exp(jnp.full((c, RET_DV), dec_ref[1, h], F32))
    xz_ref[0] = jnp.exp((j + 1.0) * log_f2)
    xz_ref[1] = jnp.exp((c - 1.0 - j) * log_f2)
    xz_ref[2] = jnp.exp((c - j) * log_b2)
    xz_ref[3] = jnp.exp(j * log_b2)
    gc_ref[0] = jnp.exp(c * -jnp.exp(jnp.full((RET_DK, RET_DV), dec_ref[0, h], F32)))
    gc_ref[1] = jnp.exp(c * -jnp.exp(jnp.full((RET_DK, RET_DV), dec_ref[1, h], F32)))


def _ret_body(dec_ref, q_ref, k_ref, v_ref, gate_ref, cos_ref, sin_ref, o_ref,
              qr_ref, kr_ref, y_ref, d_ref, xz_ref, gc_ref, sf_ref, sb_ref):
    c = RET_BLK
    nblk = SEQ // c
    _ret_tables(dec_ref, pl.program_id(0), d_ref, xz_ref, gc_ref)
    sf_ref[...] = jnp.zeros_like(sf_ref)
    sb_ref[...] = jnp.zeros_like(sb_ref)

    def rotate(x, cs, sn):
        return x * cs + pltpu.roll(x, RET_DK // 2, 1) * sn

    def up(i, carry):
        sl = pl.ds(pl.multiple_of(i * c, c), c)
        cs, sn = cos_ref[sl, :], sin_ref[sl, :]
        q = rotate(q_ref[sl, :].astype(F32), cs, sn).astype(BF16)
        k = (rotate(k_ref[sl, :].astype(F32), cs, sn) * (RET_DK ** -0.5)).astype(BF16)
        qr_ref[sl, :] = q
        kr_ref[sl, :] = k
        v = v_ref[sl, :]
        inner = (_dot_nt(q, k) * d_ref[...]).astype(BF16)
        state = sf_ref[...]
        y_ref[sl, :] = _dot(inner, v) + _dot(q, state.astype(BF16)) * xz_ref[0]
        sf_ref[...] = state * gc_ref[0] + _dot_tn(k, (v.astype(F32) * xz_ref[1]).astype(BF16))
        return carry

    lax.fori_loop(0, nblk, up, 0, unroll=8)

    def down(ii, carry):
        sl = pl.ds(pl.multiple_of((nblk - 1 - ii) * c, c), c)
        q, k, v = qr_ref[sl, :], kr_ref[sl, :], v_ref[sl, :]
        state = sb_ref[...]
        y = y_ref[sl, :] + _dot(q, state.astype(BF16)) * xz_ref[2]
        sb_ref[...] = state * gc_ref[1] + _dot_tn(k, (v.astype(F32) * xz_ref[3]).astype(BF16))
        mu = jnp.mean(y, axis=-1, keepdims=True)
        yc = y - mu
        var = jnp.mean(yc * yc, axis=-1, keepdims=True)
        gate = gate_ref[sl, :].astype(F32)
        o_ref[sl, :] = (gate * _sigmoid(gate) * (yc * lax.rsqrt(var + EPS))).astype(o_ref.dtype)
        return carry

    lax.fori_loop(0, nblk, down, 0, unroll=8)


def _retention(u, decay, cos2, sin2):
    qb, kb = OFF_QB // RET_DK, OFF_KB // RET_DK
    vb, gb = OFF_VB // RET_DV, OFF_GB // RET_DV
    qk_blk, v_blk = (SEQ, RET_DK), (SEQ, RET_DV)
    scratch = [(qk_blk, BF16), (qk_blk, BF16), (v_blk, F32), ((RET_BLK, RET_BLK), F32),
               ((4, RET_BLK, RET_DV), F32), ((2, RET_DK, RET_DV), F32),
               ((RET_DK, RET_DV), F32), ((RET_DK, RET_DV), F32)]
    return pl.pallas_call(
        _ret_body,
        grid=(RET_HEADS, BATCH),
        in_specs=[pl.BlockSpec(memory_space=pltpu.SMEM),
                  pl.BlockSpec(qk_blk, lambda h, b: (b, qb + h)),
                  pl.BlockSpec(qk_blk, lambda h, b: (b, kb + h)),
                  pl.BlockSpec(v_blk, lambda h, b: (b, vb + h)),
                  pl.BlockSpec(v_blk, lambda h, b: (b, gb + h)),
                  pl.BlockSpec(qk_blk, lambda h, b: (0, 0)),
                  pl.BlockSpec(qk_blk, lambda h, b: (0, 0))],
        out_specs=pl.BlockSpec(v_blk, lambda h, b: (b, h)),
        out_shape=jax.ShapeDtypeStruct((M_TOK, RET_V_W), BF16),
        scratch_shapes=[pltpu.VMEM(s, d) for s, d in scratch],
        compiler_params=_params(
            ("parallel", "parallel"),
            [(qk_blk, BF16)] * 2 + [(v_blk, BF16)] * 3 + [(qk_blk, F32)] * 2,
            scratch + [((RET_BLK, RET_DV), F32)] * 4),
        name="retention",
    )(decay, u, u, u, u, cos2, sin2)


def _postmix_body(oa_ref, ob_ref, oc_ref, wb_ref, sa_ref, sb_ref, sc_ref, wo_ref, x_ref,
                  g_ref, wq_ref, kv_ref, wco_ref, o_ref):
    def branch(o, row0, s):
        return _sigmoid(s[...].astype(F32)) * _dot(o[...], wb_ref[row0:row0 + o.shape[1], :])

    merged = branch(oa_ref, 0, sa_ref) + branch(ob_ref, NA_W, sb_ref) + branch(oc_ref, NA_W + RET_V_W, sc_ref)
    x1 = x_ref[...] + _dot(merged.astype(BF16), wo_ref[...])
    q = _dot(_rms(x1, g_ref[...]).astype(BF16), wq_ref[...]).astype(BF16)
    heads = []
    for h in range(CROSS_HEADS):
        lo = h * HEAD_DIM
        k = kv_ref[:, lo:lo + HEAD_DIM]
        v = kv_ref[:, CROSS_W + lo:CROSS_W + lo + HEAD_DIM]
        s = _dot_nt(q[:, lo:lo + HEAD_DIM], k) * QK_SCALE2
        m = jnp.max(s, axis=-1, keepdims=True)
        p = jnp.exp2(s - m)
        l = jnp.sum(p, axis=-1, keepdims=True)
        heads.append((_dot(p.astype(BF16), v) * (1.0 / l)).astype(BF16))
    o_ref[...] = x1 + _dot(jnp.concatenate(heads, axis=-1), wco_ref[...])


def _postmix(o_a, o_b, o_c, w_branch, u, w_out, x, g, w_cq, kv, w_co, *, tm):
    mix_w = NA_W + RET_V_W + DIL_W
    per_batch = SEQ // tm
    once = pl.Buffered(1)

    def rows(k):
        return pl.BlockSpec((tm, k), lambda i: (i, 0))

    def gate(off):
        return pl.BlockSpec((pl.Element(tm), pl.Element(D_MODEL)), lambda i: (i * tm, off))

    def weight(r, c):
        return pl.BlockSpec((r, c), lambda i: (0, 0), pipeline_mode=once)

    return pl.pallas_call(
        _postmix_body,
        grid=(M_TOK // tm,),
        in_specs=[rows(NA_W), rows(RET_V_W), rows(DIL_W), weight(mix_w, D_MODEL),
                  gate(OFF_SA), gate(OFF_SB), gate(OFF_SC), weight(D_MODEL, D_MODEL), rows(D_MODEL),
                  pl.BlockSpec((1, D_MODEL), lambda i: (0, 0), pipeline_mode=once),
                  weight(D_MODEL, CROSS_W),
                  pl.BlockSpec((MEM_LEN, 2 * CROSS_W), lambda i: (i // per_batch, 0)),
                  weight(CROSS_W, D_MODEL)],
        out_specs=rows(D_MODEL),
        out_shape=jax.ShapeDtypeStruct((M_TOK, D_MODEL), F32),
        compiler_params=_params(
            ("parallel",),
            [((tm, mix_w), BF16)] + [((tm, D_MODEL), BF16)] * 3 + [((tm, D_MODEL), F32)] * 2
            + [((MEM_LEN, 2 * CROSS_W), BF16)],
            [((mix_w + D_MODEL + 2 * CROSS_W, D_MODEL), BF16)] + [((tm, D_MODEL), F32)] * 6),
        name="post_mix",
    )(o_a, o_b, o_c, w_branch, u, u, u, w_out, x, g.reshape(1, D_MODEL), w_cq, kv, w_co)


def _mlp_body(*refs, final):
    if final:
        x_ref, g_ref, w1_ref, w2_ref, gf_ref, o_ref, xn_ref = refs
    else:
        x_ref, g_ref, w1_ref, w2_ref, o_ref, xn_ref = refs

    @pl.when(pl.program_id(1) == 0)
    def _():
        x = x_ref[...]
        xn_ref[...] = _rms(x, g_ref[...]).astype(BF16)
        o_ref[...] = x

    hid = jnp.square(jnp.maximum(_dot(xn_ref[...], w1_ref[...]), 0.0)).astype(BF16)
    o_ref[...] += _dot(hid, w2_ref[...])

    if final:
        @pl.when(pl.program_id(1) == pl.num_programs(1) - 1)
        def _():
            o_ref[...] = _rms(o_ref[...], gf_ref[...])


def _mlp(x, g, w1, w2, *, tm, tf, cast=None, final_g=None):
    m, k = x.shape
    vec = pl.BlockSpec((1, k), lambda i, f: (0, 0))
    final = final_g is not None
    return _call_with_cast(
        functools.partial(_mlp_body, final=final),
        grid=(m // tm, D_FF // tf),
        in_specs=[pl.BlockSpec((tm, k), lambda i, f: (i, 0)), vec,
                  pl.BlockSpec((k, tf), lambda i, f: (0, f)),
                  pl.BlockSpec((tf, k), lambda i, f: (f, 0))] + ([vec] if final else []),
        out_spec=pl.BlockSpec((tm, k), lambda i, f: (i, 0)),
        out_shape=jax.ShapeDtypeStruct((m, k), F32),
        scratch=[((tm, k), BF16)],
        pipelined=[((tm, k), F32), ((1, k), F32), ((k, tf), BF16), ((tf, k), BF16), ((tm, k), F32)],
        temporaries=[((tm, tf), F32), ((tm, tf), BF16), ((tm, k), F32)],
        operands=[x, g.reshape(1, k), w1, w2] + ([final_g.reshape(1, k)] if final else []), name="mlp", cast=cast)


def kernel(x, mem, t5_bias, norm_mix_g, w_in, na_rpb, ret_decay, w_branch, w_out, norm_cross_g, norm_mem_g,
           w_cq, w_ckv, w_co, norm_mlp_g, w_mlp1, w_mlp2, final_norm_g):
    xs = x.reshape(M_TOK, D_MODEL)
    mem2 = mem.reshape(BATCH * MEM_LEN, D_MODEL)
    stacked = [w_in, w_mlp1, w_mlp2, w_branch, w_out, w_cq, w_ckv, w_co]
    w_in_h = w_in[0].astype(BF16)
    cos2, sin2 = _rotary_tables()
    dil_rows = _dil_offset_rows(t5_bias)

    for layer in range(DEPTH):
        u, rest = _normmm(xs, norm_mix_g[layer], w_in_h, tm=1024, tn=1536, name="in_proj",
                          cast=(layer, stacked[1:]))
        w_mlp1_h, w_mlp2_h, w_branch_h, w_out_h, w_cq_h, w_ckv_h, w_co_h = rest
        o_a = _na_attention(u, na_rpb[layer])
        o_b = _retention(u, ret_decay[layer], cos2, sin2)
        o_c = _dil_attention(u, dil_rows)
        kv, _ = _normmm(mem2, norm_mem_g[layer], w_ckv_h, tm=BATCH * MEM_LEN, tn=512, name="mem_kv_proj")
        xs = _postmix(o_a, o_b, o_c, w_branch_h, u, w_out_h, xs, norm_cross_g[layer], w_cq_h, kv, w_co_h, tm=256)
        last = layer + 1 == DEPTH
        xs, half = _mlp(xs, norm_mlp_g[layer], w_mlp1_h, w_mlp2_h, tm=512, tf=1024,
                        cast=None if last else (layer + 1, stacked[:1]), final_g=final_norm_g if last else None)
        if not last:
            w_in_h, = half
    return xs.reshape(BATCH, SEQ, D_MODEL)
```

```python
import functools

import numpy as np
import jax
import jax.numpy as jnp
from jax import lax
from jax.experimental import pallas as pl
from jax.experimental.pallas import tpu as pltpu

D_MODEL = 2048
BATCH = 2
SEQ = 4096
DEPTH = 4
MEM_LEN = 256
HEAD_DIM = 128
GRID_W = 64
NA_HEADS = 6
NA_ROWS = 8
NA_COLS = 16
RET_HEADS = 4
RET_DK = 128
RET_DV = 256
RET_BLK = 256
ROPE_BASE = 10000.0
DIL_HEADS = 6
DIL_PAIRS = ((128, 1), (512, 4), (2048, 16))
T5_BUCKETS = 32
T5_MAX_DIST = 1024
CROSS_HEADS = 4
D_FF = 4 * D_MODEL
EPS = 1e-6
NEG = -1e30

NA_W = NA_HEADS * HEAD_DIM
RET_QK_W = RET_HEADS * RET_DK
RET_V_W = RET_HEADS * RET_DV
DIL_W = DIL_HEADS * HEAD_DIM
CROSS_W = CROSS_HEADS * HEAD_DIM
IN_SPLITS = (NA_W, NA_W, NA_W, RET_QK_W, RET_QK_W, RET_V_W, RET_V_W,
             DIL_W, DIL_W, DIL_W, D_MODEL, D_MODEL, D_MODEL)
IN_W = sum(IN_SPLITS)
(OFF_QA, OFF_KA, OFF_VA, OFF_QB, OFF_KB, OFF_VB, OFF_GB,
 OFF_QC, OFF_KC, OFF_VC, OFF_SA, OFF_SB, OFF_SC) = [int(o) for o in np.cumsum((0,) + IN_SPLITS[:-1])]

M_TOK = BATCH * SEQ
ATT_SCALE = HEAD_DIM ** -0.5
LOG2E = float(np.log2(np.e))
QK_SCALE2 = ATT_SCALE * LOG2E

V7X_VMEM_BYTES = 64 * 1024 * 1024
LANES = 128
SUBLANES = 8

NA_QROWS = 4
NA_KROWS = NA_QROWS + NA_ROWS
NA_QBLK = NA_QROWS * GRID_W
NA_KBLK = NA_KROWS * GRID_W
GRID_H = SEQ // GRID_W
NA_NBLK = GRID_H // NA_QROWS

DIL_BLK = 256
DIL_REACH = (max(w for w, _ in DIL_PAIRS) // 2 + DIL_BLK - 1) // DIL_BLK
DIL_NDELTA = 2 * DIL_REACH + 1
DIL_NBLK = SEQ // DIL_BLK

F32 = jnp.float32
BF16 = jnp.bfloat16


def _nbytes(shape, dtype):
    return int(np.prod(shape)) * jnp.dtype(dtype).itemsize


def _params(semantics, pipelined, resident):
    need = 2 * sum(_nbytes(s, d) for s, d in pipelined) + sum(_nbytes(s, d) for s, d in resident)
    assert need < V7X_VMEM_BYTES, need
    return pltpu.CompilerParams(dimension_semantics=semantics,
                                vmem_limit_bytes=min(V7X_VMEM_BYTES, need + need // 4))


def _rms(x, g):
    ms = jnp.mean(x * x, axis=-1, keepdims=True)
    return x * lax.rsqrt(ms + EPS) * g


def _sigmoid(x):
    return 1.0 / (1.0 + jnp.exp(-x))


def _dot(a, b):
    return jnp.dot(a, b, preferred_element_type=F32)


def _dot_nt(a, b):
    return lax.dot_general(a, b, (((1,), (1,)), ((), ())), preferred_element_type=F32)


def _dot_tn(a, b):
    return lax.dot_general(a, b, (((0,), (0,)), ((), ())), preferred_element_type=F32)


CAST_CHUNK_BYTES = 1 << 19


def _cast_rows(rows, cols, nsteps):
    r = 2 * SUBLANES
    while rows % r or rows // r > nsteps or (r * cols * 4 < CAST_CHUNK_BYTES and r < rows):
        r += 2 * SUBLANES
    return r


def _cast_chunk(step, nsteps, layer, srcs, dsts, stages, halves, sem_in, sem_out):
    slot = step % 2
    every = range(len(srcs))
    rows = [st.shape[1] for st in stages]
    nchunk = [dsts[k].shape[0] // rows[k] for k in every]

    def in_copy(k, s, sl):
        return pltpu.make_async_copy(srcs[k].at[layer, pl.ds(pl.multiple_of(s * rows[k], rows[k]), rows[k]), :],
                                     stages[k].at[sl], sem_in.at[k, sl])

    def out_copy(k, s, sl):
        return pltpu.make_async_copy(halves[k].at[sl], dsts[k].at[pl.ds(pl.multiple_of(s * rows[k], rows[k]), rows[k]), :],
                                     sem_out.at[k, sl])

    def cast(k):
        in_copy(k, step, slot).wait()
        halves[k][slot] = stages[k][slot].astype(BF16)

    @pl.when(step == 0)
    def _():
        for k in every:
            in_copy(k, 0, 0).start()

    full = [k for k in every if nchunk[k] == nsteps]
    for nc in sorted(set(nchunk)):
        group = [k for k in every if nchunk[k] == nc]

        @pl.when(step + 1 < nc)
        def _():
            for k in group:
                in_copy(k, step + 1, 1 - slot).start()

        @pl.when((step >= 2) & (step < nc + 2))
        def _():
            for k in group:
                out_copy(k, step - 2, slot).wait()

        if nc < nsteps:
            @pl.when(step < nc)
            def _():
                for k in group:
                    cast(k)
                    out_copy(k, step, slot).start()

    for k in full:
        cast(k)

    def epilogue():
        for k in full:
            out_copy(k, step, slot).start()

        late = [(k, c) for k in every for c in range(max(nchunk[k] - 2, 0), nchunk[k]) if c + 2 > nsteps - 1]
        if late:
            @pl.when(step == nsteps - 1)
            def _():
                for k, c in late:
                    out_copy(k, c, c % 2).wait()

    return epilogue


def _call_with_cast(body, *, grid, in_specs, out_spec, out_shape, scratch, pipelined, temporaries, operands, name,
                    cast=None):
    layer, srcs = cast if cast is not None else (None, [])
    n, n_in, n_scr = len(srcs), len(in_specs), len(scratch)
    nsteps = grid[0] * grid[1]
    chunks = [(_cast_rows(w.shape[1], w.shape[2], nsteps), w.shape[2]) for w in srcs]
    cast_scratch = [((2,) + c, F32) for c in chunks] + [((2,) + c, BF16) for c in chunks]

    def wrapped(*refs):
        ins, srcs_r, out = refs[:n_in], refs[n_in:n_in + n], refs[n_in + n]
        dsts = refs[n_in + n + 1:n_in + 2 * n + 1]
        rest = refs[n_in + 2 * n + 1:]
        after = None
        if n:
            stages, halves, (sem_in, sem_out) = rest[n_scr:n_scr + n], rest[n_scr + n:n_scr + 2 * n], rest[-2:]
            step = pl.program_id(0) * grid[1] + pl.program_id(1)
            after = _cast_chunk(step, nsteps, layer, srcs_r, dsts, stages, halves, sem_in, sem_out)
        body(*ins, out, *rest[:n_scr])
        if after is not None:
            after()

    hbm = pl.BlockSpec(memory_space=pltpu.HBM)
    out = pl.pallas_call(
        wrapped,
        grid=grid,
        in_specs=list(in_specs) + [hbm] * n,
        out_specs=[out_spec] + [hbm] * n,
        out_shape=[out_shape] + [jax.ShapeDtypeStruct(w.shape[1:], BF16) for w in srcs],
        scratch_shapes=[pltpu.VMEM(sh, d) for sh, d in scratch + cast_scratch]
        + ([pltpu.SemaphoreType.DMA((n, 2))] * 2 if n else []),
        compiler_params=_params(("arbitrary", "arbitrary") if n else ("parallel", "arbitrary"),
                                pipelined, scratch + cast_scratch + temporaries),
        name=name,
    )(*operands, *srcs)
    return out[0], out[1:]


def _normmm_body(x_ref, g_ref, w_ref, o_ref, xn_ref):
    @pl.when(pl.program_id(1) == 0)
    def _():
        xn_ref[...] = _rms(x_ref[...], g_ref[...]).astype(BF16)

    o_ref[...] = _dot(xn_ref[...], w_ref[...]).astype(o_ref.dtype)


def _normmm(x, g, w, *, tm, tn, name, cast=None):
    m, k = x.shape
    n = w.shape[-1]
    return _call_with_cast(
        _normmm_body,
        grid=(m // tm, n // tn),
        in_specs=[pl.BlockSpec((tm, k), lambda i, j: (i, 0)),
                  pl.BlockSpec((1, k), lambda i, j: (0, 0)),
                  pl.BlockSpec((k, tn), lambda i, j: (0, j))],
        out_spec=pl.BlockSpec((tm, tn), lambda i, j: (i, j)),
        out_shape=jax.ShapeDtypeStruct((m, n), BF16),
        scratch=[((tm, k), BF16)],
        pipelined=[((tm, k), F32), ((1, k), F32), ((k, tn), BF16), ((tm, tn), BF16)],
        temporaries=[((tm, tn), F32)],
        operands=[x, g.reshape(1, k), w], name=name, cast=cast)


NA_RPB_ROWS = 2 * NA_ROWS - 1
NA_RPB_COLS = 2 * NA_COLS - 1
NA_TILE_ROWS = ((0, 0), (NA_QROWS, 0), (GRID_H - NA_QROWS, GRID_H - NA_KROWS))


def _na_tiles_body(rpb_ref, o_ref):
    shape = (GRID_W, LANES)
    lane = lax.broadcasted_iota(jnp.int32, shape, 1)
    cq = lax.broadcasted_iota(jnp.int32, shape, 0)
    ck = lane & (GRID_W - 1)
    c0 = jnp.clip(cq - NA_COLS // 2, 0, GRID_W - NA_COLS)
    col_ok = (ck >= c0) & (ck < c0 + NA_COLS)
    left = lane < GRID_W
    neg = jnp.full(shape, NEG, F32)
    lo, hi = [], []
    for a in range(NA_RPB_ROWS):
        row = jnp.broadcast_to(rpb_ref[a:a + 1, :], shape)
        lo.append(pltpu.roll(row, LANES - (NA_COLS - 1), 1, stride=1, stride_axis=0))
        hi.append(pltpu.roll(row, GRID_W - (NA_COLS - 1), 1, stride=1, stride_axis=0))
    for t, (qrow0, krow0) in enumerate(NA_TILE_ROWS):
        for rq in range(NA_QROWS):
            r = qrow0 + rq
            r0 = min(max(r - NA_ROWS // 2, 0), GRID_H - NA_ROWS)
            for pair in range(NA_KROWS // 2):
                halves = []
                for side, table in ((0, lo), (1, hi)):
                    rk = krow0 + 2 * pair + side
                    halves.append(table[rk - r + NA_ROWS - 1] if r0 <= rk < r0 + NA_ROWS else neg)
                blk = jnp.where(col_ok, jnp.where(left, halves[0], halves[1]) * LOG2E, NEG)
                o_ref[t, rq * GRID_W:(rq + 1) * GRID_W, pair * LANES:(pair + 1) * LANES] = blk


def _na_body(q_ref, k_ref, v_ref, rpb_ref, o_ref, s_ref, bias_ref):
    @pl.when(pl.program_id(1) == 0)
    def _():
        _na_tiles_body(rpb_ref, bias_ref)

    def starts(i):
        qrow = i * NA_QROWS
        krow = jnp.clip(qrow - NA_ROWS // 2, 0, GRID_H - NA_KROWS)
        return pl.multiple_of(qrow * GRID_W, NA_QBLK), pl.multiple_of(krow * GRID_W, NA_QBLK)

    def logits(i, buf):
        qs, ks = starts(i)
        tile = jnp.where(i == 0, 0, jnp.where(i == NA_NBLK - 1, 2, 1))
        s_ref[buf] = (_dot_nt(q_ref[pl.ds(qs, NA_QBLK), :], k_ref[pl.ds(ks, NA_KBLK), :]) * QK_SCALE2
                      + bias_ref[tile])

    def finish(i, buf):
        qs, ks = starts(i)
        s = s_ref[buf]
        m = jnp.max(s, axis=-1, keepdims=True)
        p = jnp.exp2(s - m)
        l = jnp.sum(p, axis=-1, keepdims=True)
        o = _dot(p.astype(BF16), v_ref[pl.ds(ks, NA_KBLK), :]) * (1.0 / l)
        o_ref[pl.ds(qs, NA_QBLK), :] = o.astype(o_ref.dtype)

    logits(0, 0)

    def pair(ii, carry):
        i = 2 * ii
        logits(i + 1, 1)
        finish(i, 0)
        logits(jnp.minimum(i + 2, NA_NBLK - 1), 0)
        finish(i + 1, 1)
        return carry

    lax.fori_loop(0, NA_NBLK // 2, pair, 0, unroll=4)


def _na_attention(u, rpb):
    padded = jnp.pad(rpb.astype(F32), ((0, 0), (0, 2 * SUBLANES - NA_RPB_ROWS), (0, LANES - NA_RPB_COLS)))
    qb, kb, vb = OFF_QA // HEAD_DIM, OFF_KA // HEAD_DIM, OFF_VA // HEAD_DIM
    blk = (SEQ, HEAD_DIM)
    return pl.pallas_call(
        _na_body,
        grid=(NA_HEADS, BATCH),
        in_specs=[pl.BlockSpec(blk, lambda h, b: (b, qb + h)),
                  pl.BlockSpec(blk, lambda h, b: (b, kb + h)),
                  pl.BlockSpec(blk, lambda h, b: (b, vb + h)),
                  pl.BlockSpec((None, 2 * SUBLANES, LANES), lambda h, b: (h, 0, 0))],
        out_specs=pl.BlockSpec(blk, lambda h, b: (b, h)),
        out_shape=jax.ShapeDtypeStruct((M_TOK, NA_W), BF16),
        scratch_shapes=[pltpu.VMEM((2, NA_QBLK, NA_KBLK), F32), pltpu.VMEM((3, NA_QBLK, NA_KBLK), F32)],
        compiler_params=_params(
            ("arbitrary", "arbitrary"),
            [(blk, BF16)] * 4,
            [((NA_QBLK, NA_KBLK), F32)] * 8),
        name="na_attention",
    )(u, u, u, padded)


def _t5_bucket(rel):
    nb = T5_BUCKETS // 2
    ret = (rel > 0).astype(np.int32) * nb
    n = np.abs(rel)
    max_exact = nb // 2
    large = max_exact + (np.log(np.maximum(n, 1) / max_exact) / np.log(T5_MAX_DIST / max_exact)
                         * (nb - max_exact)).astype(np.int32)
    large = np.minimum(large, nb - 1)
    return (ret + np.where(n < max_exact, n, large)).astype(np.int32)


def _dil_tiles_body(f_ref, o_ref):
    for d in range(DIL_NDELTA):
        row = jnp.broadcast_to(f_ref[d:d + 1, :], (DIL_BLK, 2 * DIL_BLK))
        o_ref[d] = pltpu.roll(row, DIL_BLK, 1, stride=1, stride_axis=0)[:, :DIL_BLK]
    o_ref[DIL_NDELTA] = jnp.full((DIL_BLK, DIL_BLK), NEG, F32)


def _dil_offset_rows(t5_bias):
    delta = np.arange(-DIL_REACH, DIL_REACH + 1)[:, None]
    off = delta * DIL_BLK + np.arange(-DIL_BLK, DIL_BLK)[None, :]
    count = np.zeros(off.shape, np.int32)
    for w, d in DIL_PAIRS:
        count += ((off % d == 0) & (np.abs(off) <= w // 2)).astype(np.int32)
    bucket = _t5_bucket(np.clip(off, -T5_MAX_DIST, T5_MAX_DIST))
    bias = jnp.take(t5_bias.T.astype(F32), jnp.asarray(bucket.reshape(-1)), axis=1).reshape((DIL_HEADS,) + off.shape)
    logc = jnp.log(jnp.asarray(np.maximum(count, 1), F32))
    f = jnp.where(jnp.asarray(count > 0)[None], (bias + logc[None]) * LOG2E, NEG)
    return jnp.pad(f, ((0, 0), (0, 2 * SUBLANES - DIL_NDELTA), (0, 0)))


def _dil_body(q_ref, k_ref, v_ref, f_ref, o_ref, s_ref, vx_ref, t_ref):
    @pl.when(pl.program_id(1) == 0)
    def _():
        _dil_tiles_body(f_ref, t_ref)

    vx_ref[:, :HEAD_DIM] = v_ref[...]
    lane = lax.broadcasted_iota(jnp.int32, (SEQ, HEAD_DIM), 1)
    vx_ref[:, HEAD_DIM:] = jnp.where(lane == 0, 1.0, 0.0).astype(BF16)

    def key_start(i, d):
        return pl.multiple_of(jnp.clip(i + (d - DIL_REACH), 0, DIL_NBLK - 1) * DIL_BLK, DIL_BLK)

    def logits(i, buf):
        q = q_ref[pl.ds(pl.multiple_of(i * DIL_BLK, DIL_BLK), DIL_BLK), :]
        for d in range(DIL_NDELTA):
            j = i + (d - DIL_REACH)
            tile = jnp.where((j >= 0) & (j < DIL_NBLK), d, DIL_NDELTA)
            s_ref[buf, :, d * DIL_BLK:(d + 1) * DIL_BLK] = (
                _dot_nt(q, k_ref[pl.ds(key_start(i, d), DIL_BLK), :]) * QK_SCALE2 + t_ref[tile])

    def finish(i, buf):
        half = DIL_BLK // 2
        for r0 in (0, half):
            m = jnp.max(s_ref[buf, r0:r0 + half, :], axis=-1, keepdims=True)
            acc = jnp.zeros((half, 2 * HEAD_DIM), F32)
            for d in range(DIL_NDELTA):
                p = jnp.exp2(s_ref[buf, r0:r0 + half, d * DIL_BLK:(d + 1) * DIL_BLK] - m)
                acc = acc + _dot(p.astype(BF16), vx_ref[pl.ds(key_start(i, d), DIL_BLK), :])
            out = acc[:, :HEAD_DIM] * (1.0 / acc[:, HEAD_DIM:HEAD_DIM + 1])
            o_ref[pl.ds(pl.multiple_of(i * DIL_BLK + r0, half), half), :] = out.astype(o_ref.dtype)

    logits(0, 0)

    def pair(ii, carry):
        i = 2 * ii
        logits(i + 1, 1)
        finish(i, 0)
        logits(jnp.minimum(i + 2, DIL_NBLK - 1), 0)
        finish(i + 1, 1)
        return carry

    lax.fori_loop(0, DIL_NBLK // 2, pair, 0, unroll=4)


def _dil_attention(u, offset_rows):
    qb, kb, vb = OFF_QC // HEAD_DIM, OFF_KC // HEAD_DIM, OFF_VC // HEAD_DIM
    blk = (SEQ, HEAD_DIM)
    return pl.pallas_call(
        _dil_body,
        grid=(DIL_HEADS, BATCH),
        in_specs=[pl.BlockSpec(blk, lambda h, b: (b, qb + h)),
                  pl.BlockSpec(blk, lambda h, b: (b, kb + h)),
                  pl.BlockSpec(blk, lambda h, b: (b, vb + h)),
                  pl.BlockSpec((None, 2 * SUBLANES, 2 * DIL_BLK), lambda h, b: (h, 0, 0))],
        out_specs=pl.BlockSpec(blk, lambda h, b: (b, h)),
        out_shape=jax.ShapeDtypeStruct((M_TOK, DIL_W), BF16),
        scratch_shapes=[pltpu.VMEM((2, DIL_BLK, DIL_NDELTA * DIL_BLK), F32), pltpu.VMEM((SEQ, 2 * HEAD_DIM), BF16),
                        pltpu.VMEM((DIL_NDELTA + 1, DIL_BLK, DIL_BLK), F32)],
        compiler_params=_params(
            ("arbitrary", "arbitrary"),
            [(blk, BF16)] * 4,
            [((DIL_BLK, DIL_NDELTA * DIL_BLK), F32)] * 3 + [((SEQ, 2 * HEAD_DIM), BF16),
                                                            ((DIL_NDELTA + 1, DIL_BLK, DIL_BLK), F32),
                                                            ((DIL_BLK, 2 * DIL_BLK), F32)]),
        name="dil_attention",
    )(u, u, u, offset_rows)


def _rotary_tables():
    inv_freq = jnp.asarray((ROPE_BASE ** (-np.arange(0, RET_DK, 2, dtype=np.float32) / RET_DK)).astype(np.float32))
    ang = jnp.arange(SEQ, dtype=F32)[:, None] * inv_freq[None, :]
    cos, sin = jnp.cos(ang), jnp.sin(ang)
    return jnp.concatenate([cos, cos], axis=-1), jnp.concatenate([-sin, sin], axis=-1)


def _ret_tables(dec_ref, h, d_ref, xz_ref, gc_ref):
    c = RET_BLK
    row = lax.broadcasted_iota(jnp.int32, (c, c), 0)
    col = lax.broadcasted_iota(jnp.int32, (c, c), 1)
    diff = (row - col).astype(F32)
    log_f = -jnp.exp(jnp.full((c, c), dec_ref[0, h], F32))
    log_b = -jnp.exp(jnp.full((c, c), dec_ref[1, h], F32))
    d_ref[...] = jnp.where(diff >= 0.0, jnp.exp(jnp.maximum(diff, 0.0) * log_f),
                           jnp.exp(jnp.maximum(-diff, 0.0) * log_b))
    j = lax.broadcasted_iota(jnp.int32, (c, RET_DV), 0).astype(F32)
    log_f2 = -jnp.exp(jnp.full((c, RET_DV), dec_ref[0, h], F32))
    log_b2 = -jnp.exp(jnp.full((c, RET_DV), dec_ref[1, h], F32))
    xz_ref[0] = jnp.exp((j + 1.0) * log_f2)
    xz_ref[1] = jnp.exp((c - 1.0 - j) * log_f2)
    xz_ref[2] = jnp.exp((c - j) * log_b2)
    xz_ref[3] = jnp.exp(j * log_b2)
    gc_ref[0] = jnp.exp(c * -jnp.exp(jnp.full((RET_DK, RET_DV), dec_ref[0, h], F32)))
    gc_ref[1] = jnp.exp(c * -jnp.exp(jnp.full((RET_DK, RET_DV), dec_ref[1, h], F32)))


def _ret_body(dec_ref, q_ref, k_ref, v_ref, gate_ref, cos_ref, sin_ref, o_ref,
              qr_ref, kr_ref, y_ref, d_ref, xz_ref, gc_ref, sf_ref, sb_ref):
    c = RET_BLK
    nblk = SEQ // c
    _ret_tables(dec_ref, pl.program_id(0), d_ref, xz_ref, gc_ref)
    sf_ref[...] = jnp.zeros_like(sf_ref)
    sb_ref[...] = jnp.zeros_like(sb_ref)

    def rotate(x, cs, sn):
        return x * cs + pltpu.roll(x, RET_DK // 2, 1) * sn

    def up(i, carry):
        sl = pl.ds(pl.multiple_of(i * c, c), c)
        cs, sn = cos_ref[sl, :], sin_ref[sl, :]
        q = rotate(q_ref[sl, :].astype(F32), cs, sn).astype(BF16)
        k = (rotate(k_ref[sl, :].astype(F32), cs, sn) * (RET_DK ** -0.5)).astype(BF16)
        qr_ref[sl, :] = q
        kr_ref[sl, :] = k
        v = v_ref[sl, :]
        inner = (_dot_nt(q, k) * d_ref[...]).astype(BF16)
        state = sf_ref[...]
        y_ref[sl, :] = _dot(inner, v) + _dot(q, state.astype(BF16)) * xz_ref[0]
        sf_ref[...] = state * gc_ref[0] + _dot_tn(k, (v.astype(F32) * xz_ref[1]).astype(BF16))
        return carry

    lax.fori_loop(0, nblk, up, 0, unroll=8)

    def down(ii, carry):
        sl = pl.ds(pl.multiple_of((nblk - 1 - ii) * c, c), c)
        q, k, v = qr_ref[sl, :], kr_ref[sl, :], v_ref[sl, :]
        state = sb_ref[...]
        y = y_ref[sl, :] + _dot(q, state.astype(BF16)) * xz_ref[2]
        sb_ref[...] = state * gc_ref[1] + _dot_tn(k, (v.astype(F32) * xz_ref[3]).astype(BF16))
        mu = jnp.mean(y, axis=-1, keepdims=True)
        yc = y - mu
        var = jnp.mean(yc * yc, axis=-1, keepdims=True)
        gate = gate_ref[sl, :].astype(F32)
        o_ref[sl, :] = (gate * _sigmoid(gate) * (yc * lax.rsqrt(var + EPS))).astype(o_ref.dtype)
        return carry

    lax.fori_loop(0, nblk, down, 0, unroll=8)


def _retention(u, decay, cos2, sin2):
    qb, kb = OFF_QB // RET_DK, OFF_KB // RET_DK
    vb, gb = OFF_VB // RET_DV, OFF_GB // RET_DV
    qk_blk, v_blk = (SEQ, RET_DK), (SEQ, RET_DV)
    scratch = [(qk_blk, BF16), (qk_blk, BF16), (v_blk, F32), ((RET_BLK, RET_BLK), F32),
               ((4, RET_BLK, RET_DV), F32), ((2, RET_DK, RET_DV), F32),
               ((RET_DK, RET_DV), F32), ((RET_DK, RET_DV), F32)]
    return pl.pallas_call(
        _ret_body,
        grid=(RET_HEADS, BATCH),
        in_specs=[pl.BlockSpec(memory_space=pltpu.SMEM),
                  pl.BlockSpec(qk_blk, lambda h, b: (b, qb + h)),
                  pl.BlockSpec(qk_blk, lambda h, b: (b, kb + h)),
                  pl.BlockSpec(v_blk, lambda h, b: (b, vb + h)),
                  pl.BlockSpec(v_blk, lambda h, b: (b, gb + h)),
                  pl.BlockSpec(qk_blk, lambda h, b: (0, 0)),
                  pl.BlockSpec(qk_blk, lambda h, b: (0, 0))],
        out_specs=pl.BlockSpec(v_blk, lambda h, b: (b, h)),
        out_shape=jax.ShapeDtypeStruct((M_TOK, RET_V_W), BF16),
        scratch_shapes=[pltpu.VMEM(s, d) for s, d in scratch],
        compiler_params=_params(
            ("parallel", "parallel"),
            [(qk_blk, BF16)] * 2 + [(v_blk, BF16)] * 3 + [(qk_blk, F32)] * 2,
            scratch + [((RET_BLK, RET_DV), F32)] * 4),
        name="retention",
    )(decay, u, u, u, u, cos2, sin2)


def _postmix_body(oa_ref, ob_ref, oc_ref, wb_ref, sa_ref, sb_ref, sc_ref, wo_ref, x_ref,
                  g_ref, wq_ref, kv_ref, wco_ref, o_ref):
    def branch(o, row0, s):
        return _sigmoid(s[...].astype(F32)) * _dot(o[...], wb_ref[row0:row0 + o.shape[1], :])

    merged = branch(oa_ref, 0, sa_ref) + branch(ob_ref, NA_W, sb_ref) + branch(oc_ref, NA_W + RET_V_W, sc_ref)
    x1 = x_ref[...] + _dot(merged.astype(BF16), wo_ref[...])
    q = _dot(_rms(x1, g_ref[...]).astype(BF16), wq_ref[...]).astype(BF16)
    heads = []
    for h in range(CROSS_HEADS):
        lo = h * HEAD_DIM
        k = kv_ref[:, lo:lo + HEAD_DIM]
        v = kv_ref[:, CROSS_W + lo:CROSS_W + lo + HEAD_DIM]
        s = _dot_nt(q[:, lo:lo + HEAD_DIM], k) * QK_SCALE2
        m = jnp.max(s, axis=-1, keepdims=True)
        p = jnp.exp2(s - m)
        l = jnp.sum(p, axis=-1, keepdims=True)
        heads.append((_dot(p.astype(BF16), v) * (1.0 / l)).astype(BF16))
    o_ref[...] = x1 + _dot(jnp.concatenate(heads, axis=-1), wco_ref[...])


def _postmix(o_a, o_b, o_c, w_branch, u, w_out, x, g, w_cq, kv, w_co, *, tm):
    mix_w = NA_W + RET_V_W + DIL_W
    per_batch = SEQ // tm
    once = pl.Buffered(1)

    def rows(k):
        return pl.BlockSpec((tm, k), lambda i: (i, 0))

    def gate(off):
        return pl.BlockSpec((pl.Element(tm), pl.Element(D_MODEL)), lambda i: (i * tm, off))

    def weight(r, c):
        return pl.BlockSpec((r, c), lambda i: (0, 0), pipeline_mode=once)

    return pl.pallas_call(
        _postmix_body,
        grid=(M_TOK // tm,),
        in_specs=[rows(NA_W), rows(RET_V_W), rows(DIL_W), weight(mix_w, D_MODEL),
                  gate(OFF_SA), gate(OFF_SB), gate(OFF_SC), weight(D_MODEL, D_MODEL), rows(D_MODEL),
                  pl.BlockSpec((1, D_MODEL), lambda i: (0, 0), pipeline_mode=once),
                  weight(D_MODEL, CROSS_W),
                  pl.BlockSpec((MEM_LEN, 2 * CROSS_W), lambda i: (i // per_batch, 0)),
                  weight(CROSS_W, D_MODEL)],
        out_specs=rows(D_MODEL),
        out_shape=jax.ShapeDtypeStruct((M_TOK, D_MODEL), F32),
        compiler_params=_params(
            ("parallel",),
            [((tm, mix_w), BF16)] + [((tm, D_MODEL), BF16)] * 3 + [((tm, D_MODEL), F32)] * 2
            + [((MEM_LEN, 2 * CROSS_W), BF16)],
            [((mix_w + D_MODEL + 2 * CROSS_W, D_MODEL), BF16)] + [((tm, D_MODEL), F32)] * 6),
        name="post_mix",
    )(o_a, o_b, o_c, w_branch, u, u, u, w_out, x, g.reshape(1, D_MODEL), w_cq, kv, w_co)


def _mlp_body(*refs, final):
    if final:
        x_ref, g_ref, w1_ref, w2_ref, gf_ref, o_ref, xn_ref = refs
    else:
        x_ref, g_ref, w1_ref, w2_ref, o_ref, xn_ref = refs

    @pl.when(pl.program_id(1) == 0)
    def _():
        x = x_ref[...]
        xn_ref[...] = _rms(x, g_ref[...]).astype(BF16)
        o_ref[...] = x

    hid = jnp.square(jnp.maximum(_dot(xn_ref[...], w1_ref[...]), 0.0)).astype(BF16)
    o_ref[...] += _dot(hid, w2_ref[...])

    if final:
        @pl.when(pl.program_id(1) == pl.num_programs(1) - 1)
        def _():
            o_ref[...] = _rms(o_ref[...], gf_ref[...])


def _mlp(x, g, w1, w2, *, tm, tf, cast=None, final_g=None):
    m, k = x.shape
    vec = pl.BlockSpec((1, k), lambda i, f: (0, 0))
    final = final_g is not None
    return _call_with_cast(
        functools.partial(_mlp_body, final=final),
        grid=(m // tm, D_FF // tf),
        in_specs=[pl.BlockSpec((tm, k), lambda i, f: (i, 0)), vec,
                  pl.BlockSpec((k, tf), lambda i, f: (0, f)),
                  pl.BlockSpec((tf, k), lambda i, f: (f, 0))] + ([vec] if final else []),
        out_spec=pl.BlockSpec((tm, k), lambda i, f: (i, 0)),
        out_shape=jax.ShapeDtypeStruct((m, k), F32),
        scratch=[((tm, k), BF16)],
        pipelined=[((tm, k), F32), ((1, k), F32), ((k, tf), BF16), ((tf, k), BF16), ((tm, k), F32)],
        temporaries=[((tm, tf), F32), ((tm, tf), BF16), ((tm, k), F32)],
        operands=[x, g.reshape(1, k), w1, w2] + ([final_g.reshape(1, k)] if final else []), name="mlp", cast=cast)


def kernel(x, mem, t5_bias, norm_mix_g, w_in, na_rpb, ret_decay, w_branch, w_out, norm_cross_g, norm_mem_g,
           w_cq, w_ckv, w_co, norm_mlp_g, w_mlp1, w_mlp2, final_norm_g):
    xs = x.reshape(M_TOK, D_MODEL)
    mem2 = mem.reshape(BATCH * MEM_LEN, D_MODEL)
    stacked = [w_in, w_mlp1, w_mlp2, w_branch, w_out, w_cq, w_ckv, w_co]
    w_in_h = w_in[0].astype(BF16)
    cos2, sin2 = _rotary_tables()
    dil_rows = _dil_offset_rows(t5_bias)

    for layer in range(DEPTH):
        u, rest = _normmm(xs, norm_mix_g[layer], w_in_h, tm=1024, tn=1536, name="in_proj",
                          cast=(layer, stacked[1:]))
        w_mlp1_h, w_mlp2_h, w_branch_h, w_out_h, w_cq_h, w_ckv_h, w_co_h = rest
        o_a = _na_attention(u, na_rpb[layer])
        o_b = _retention(u, ret_decay[layer], cos2, sin2)
        o_c = _dil_attention(u, dil_rows)
        kv, _ = _normmm(mem2, norm_mem_g[layer], w_ckv_h, tm=BATCH * MEM_LEN, tn=512, name="mem_kv_proj")
        xs = _postmix(o_a, o_b, o_c, w_branch_h, u, w_out_h, xs, norm_cross_g[layer], w_cq_h, kv, w_co_h, tm=256)
        last = layer + 1 == DEPTH
        xs, half = _mlp(xs, norm_mlp_g[layer], w_mlp1_h, w_mlp2_h, tm=512, tf=1024,
                        cast=None if last else (layer + 1, stacked[:1]), final_g=final_norm_g if last else None)
        if not last:
            w_in_h, = half
    return xs.reshape(BATCH, SEQ, D_MODEL)
```
